```python
import jax, jax.numpy as jnp
from jax import lax
import numpy as np

D_MODEL = 2048
BATCH = 32
SEQ = 256
DEPTH = 2
DEC_BATCH = 8
DEC_SEQ = 1024
PAST_LEN = 256

GRID_W = 64
BRANCH_WIDTH = D_MODEL // 2
N_BRANCH = 3
ML_HEADS = 4
ML_DK = BRANCH_WIDTH // ML_HEADS
ML_DV = BRANCH_WIDTH // ML_HEADS
ML_WIDTH = ML_HEADS * ML_DV
ML_CHUNK = 64
HY_WIDTH = BRANCH_WIDTH
HY_ORDER = 2
HY_BANDS = 16
HY_FEAT = 1 + 2 * HY_BANDS
HY_FFN = 64
ATT_HEADS = 8
ATT_KV_HEADS = 2
HEAD_DIM = BRANCH_WIDTH // ATT_HEADS
ATT_WIDTH = ATT_HEADS * HEAD_DIM
KV_WIDTH = ATT_KV_HEADS * HEAD_DIM
ATT_BLOCK = 128
ROPE_FREQS = HEAD_DIM // 4
ROPE_BASE = 10000.0
PEER_HEADS = 8
N_KEYS = 128
N_EXPERTS = N_KEYS * N_KEYS
PEER_TOPK = 16
PEER_DKEY = 256
PEER_DHALF = PEER_DKEY // 2
PEER_BLOCK = 128
EPS = 1e-6
IN_SIZES = (ML_WIDTH, ML_WIDTH, ML_WIDTH, ML_WIDTH, 4 * ML_HEADS, 3 * HY_WIDTH,
            ATT_WIDTH, KV_WIDTH, KV_WIDTH, N_BRANCH * D_MODEL)
IN_OFFSETS = tuple(sum(IN_SIZES[:i + 1]) for i in range(len(IN_SIZES) - 1))
IN_COLS = sum(IN_SIZES)

kernel_name = 'hybrid_mlstm_hyena_gqa_peer_step'


def rmsnorm(x, g):
    xf = x.astype(jnp.float32)
    y = xf * lax.rsqrt(jnp.mean(xf * xf, axis=-1, keepdims=True) + EPS) * g.astype(jnp.float32)
    return y.astype(x.dtype)


def grid_rope_tables(L):
    rows = L // GRID_W
    row = jnp.repeat(jnp.arange(rows, dtype=jnp.float32), GRID_W)
    col = jnp.tile(jnp.arange(GRID_W, dtype=jnp.float32), rows)
    inv = ROPE_BASE ** (-2.0 * jnp.arange(ROPE_FREQS, dtype=jnp.float32) / (2 * ROPE_FREQS))
    ang = jnp.stack([row[:, None] * inv, col[:, None] * inv], axis=1)
    return jnp.cos(ang), jnp.sin(ang)


def apply_rope(x, cos, sin):
    B, L, H, _ = x.shape
    xr = x.astype(jnp.float32).reshape(B, L, H, 2, 2, ROPE_FREQS)
    x1, x2 = xr[..., 0, :], xr[..., 1, :]
    c, s = cos[None, :, None], sin[None, :, None]
    out = jnp.stack([x1 * c - x2 * s, x2 * c + x1 * s], axis=-2)
    return out.reshape(x.shape).astype(x.dtype)


def block_attention(q, k, v):
    B, Lq = q.shape[0], q.shape[1]
    nb = Lq // ATT_BLOCK
    groups = ATT_HEADS // ATT_KV_HEADS
    qb = q.reshape(B, nb, ATT_BLOCK, ATT_KV_HEADS, groups, HEAD_DIM).transpose(1, 0, 2, 3, 4, 5)
    scale = HEAD_DIM ** -0.5

    def one_block(q_blk):
        s = jnp.einsum('bqhgd,bkhd->bhgqk', q_blk, k, preferred_element_type=jnp.float32) * scale
        p = jax.nn.softmax(s, axis=-1)
        return jnp.einsum('bhgqk,bkhd->bqhgd', p.astype(v.dtype), v)

    o = lax.map(one_block, qb)
    return o.transpose(1, 0, 2, 3, 4, 5).reshape(B, Lq, ATT_WIDTH)


def mlstm_scan(q, k, v, ig, lf, C0, n0, m0):
    B, H, L, _ = q.shape
    nc = L // ML_CHUNK

    def chunks(a):
        return jnp.moveaxis(a.reshape(B, H, nc, ML_CHUNK, *a.shape[3:]), 2, 0)

    tril = jnp.tril(jnp.ones((ML_CHUNK, ML_CHUNK), dtype=bool))

    def step(carry, inp):
        C, n, m = carry
        qc, kc, vc, igc, lfc = inp
        b = jnp.cumsum(lfc, axis=-1)
        dmat = jnp.where(tril, b[..., :, None] - b[..., None, :] + igc[..., None, :], -jnp.inf)
        m_t = jnp.maximum(b + m[..., None], jnp.max(dmat, axis=-1))
        w_in = jnp.exp(dmat - m_t[..., None])
        w_prev = jnp.exp(b + m[..., None] - m_t)
        s = jnp.einsum('bhtd,bhsd->bhts', qc, kc) * w_in
        num = jnp.einsum('bhts,bhsv->bhtv', s, vc) + w_prev[..., None] * jnp.einsum('bhtk,bhkv->bhtv', qc, C)
        den = jnp.sum(s, axis=-1) + w_prev * jnp.einsum('bhtk,bhk->bht', qc, n)
        h = num / jnp.maximum(jnp.abs(den), jnp.exp(-m_t))[..., None]
        b_last = b[..., -1]
        lw = b_last[..., None] - b + igc
        m_new = jnp.maximum(b_last + m, jnp.max(lw, axis=-1))
        decay = jnp.exp(b_last + m - m_new)
        kw = kc * jnp.exp(lw - m_new[..., None])[..., None]
        C_new = decay[..., None, None] * C + jnp.einsum('bhsk,bhsv->bhkv', kw, vc)
        n_new = decay[..., None] * n + jnp.sum(kw, axis=2)
        return (C_new, n_new, m_new), h

    carry0 = (C0.astype(jnp.float32), n0.astype(jnp.float32), m0.astype(jnp.float32))
    final, hs = lax.scan(step, carry0, (chunks(q), chunks(k), chunks(v), chunks(ig), chunks(lf)))
    h = jnp.moveaxis(hs, 0, 2).reshape(B, H, L, -1)
    return h, final


def mlstm_bidir(q, k, v, o, gates, gate_bias, norm_g, state):
    f32 = jnp.float32
    B, L, _ = q.shape

    def heads(a):
        return a.astype(f32).reshape(B, L, ML_HEADS, -1).transpose(0, 2, 1, 3)

    qh, kh, vh = heads(q), heads(k) * (ML_DK ** -0.5), heads(v)
    g = gates.astype(f32).reshape(B, L, 2, 2, ML_HEADS) + gate_bias.astype(f32)
    g = g.transpose(2, 3, 0, 4, 1)
    ig, lf = g[:, 0], jax.nn.log_sigmoid(g[:, 1])
    C0, n0, m0 = state
    h_f, (C_f, n_f, m_f) = mlstm_scan(qh, kh, vh, ig[0], lf[0], C0[:, 0], n0[:, 0], m0[:, 0])
    rev = lambda a: jnp.flip(a, axis=2)
    h_b, (C_b, n_b, m_b) = mlstm_scan(rev(qh), rev(kh), rev(vh), jnp.flip(ig[1], -1), jnp.flip(lf[1], -1),
                                      C0[:, 1], n0[:, 1], m0[:, 1])
    h = (h_f + rev(h_b)).transpose(0, 2, 1, 3)
    h = h * lax.rsqrt(jnp.mean(h * h, axis=-1, keepdims=True) + EPS) * norm_g.astype(f32).reshape(ML_HEADS, ML_DV)
    out = jax.nn.sigmoid(o.astype(f32)) * h.reshape(B, L, ML_WIDTH)
    new_state = (jnp.stack([C_f, C_b], axis=1), jnp.stack([n_f, n_b], axis=1), jnp.stack([m_f, m_b], axis=1))
    return out.astype(q.dtype), new_state


def short_conv(z, w):
    w = w.astype(z.dtype)
    zp = jnp.pad(z, ((0, 0), (1, 1), (0, 0)))
    return zp[:, :-2] * w[0] + zp[:, 1:-1] * w[1] + zp[:, 2:] * w[2]


def hyena_filters(L, w1, b1, w2, b2, w3, decay):
    f32 = jnp.float32
    pos = jnp.arange(L, dtype=f32)
    t = pos / (L - 1)
    bands = jnp.arange(1, HY_BANDS + 1, dtype=f32)
    ang = (2.0 * np.pi / L) * pos[:, None] * bands[None, :]
    feat = jnp.concatenate([t[:, None], jnp.cos(ang), jnp.sin(ang)], axis=-1)
    h = jnp.sin(feat @ w1.astype(f32) + b1.astype(f32))
    h = jnp.sin(h @ w2.astype(f32) + b2.astype(f32))
    h = (h @ w3.astype(f32)).reshape(L, HY_ORDER, 2, HY_WIDTH)
    h = h * jnp.exp(-t[:, None, None, None] * decay.astype(f32))
    filt = jnp.concatenate([h[:, :, 0], jnp.zeros((1, HY_ORDER, HY_WIDTH), f32), h[:0:-1, :, 1]], axis=0)
    filt = filt * lax.rsqrt(jnp.sum(filt * filt, axis=0, keepdims=True) + EPS)
    return jnp.fft.rfft(filt, axis=0)


def fft_conv(z, filt_f):
    L = z.shape[1]
    zf = jnp.fft.rfft(z, n=2 * L, axis=1)
    return jnp.fft.irfft(zf * filt_f[None], n=2 * L, axis=1)[:, :L]


def hyena(u, conv_w, w1, b1, w2, b2, w3, decay, skip):
    L = u.shape[1]
    uc = short_conv(u, conv_w).astype(jnp.float32)
    v, x1, x2 = jnp.split(uc, 3, axis=-1)
    filt_f = hyena_filters(L, w1, b1, w2, b2, w3, decay)
    skip = skip.astype(jnp.float32)
    z = v
    for order, xg in enumerate((x1, x2)):
        z = xg * (fft_conv(z, filt_f[:, order]) + skip[order] * z)
    return z.astype(u.dtype)


def peer(h, wq, keys, U, V):
    B, L, D = h.shape
    T = B * L
    x = h.reshape(T, D)
    q = (x @ wq).reshape(T, PEER_HEADS, 2, PEER_DHALF)
    s = jnp.einsum('thcd,hcnd->thcn', q, keys, preferred_element_type=jnp.float32)
    sv, si = lax.top_k(s, PEER_TOPK)
    cand = (sv[:, :, 0, :, None] + sv[:, :, 1, None, :]).reshape(T, PEER_HEADS, PEER_TOPK * PEER_TOPK)
    cidx = (si[:, :, 0, :, None] * N_KEYS + si[:, :, 1, None, :]).reshape(T, PEER_HEADS, PEER_TOPK * PEER_TOPK)
    fv, fi = lax.top_k(cand, PEER_TOPK)
    nb = T // PEER_BLOCK
    eidx = jnp.take_along_axis(cidx, fi, axis=-1).reshape(nb, PEER_BLOCK, PEER_HEADS * PEER_TOPK)
    gate = jax.nn.softmax(fv, axis=-1).reshape(nb, PEER_BLOCK, PEER_HEADS * PEER_TOPK)
    xb = x.reshape(nb, PEER_BLOCK, D)

    def expert_block(args):
        xt, e, g = args
        a = jax.nn.gelu(jnp.einsum('td,ted->te', xt, jnp.take(U, e, axis=0), preferred_element_type=jnp.float32))
        return jnp.einsum('te,ted->td', (g * a).astype(V.dtype), jnp.take(V, e, axis=0))

    out = lax.map(expert_block, (xb, eidx, gate))
    return out.reshape(B, L, D).astype(h.dtype)


def trunk_layer(x, mod, p, ml_state, ext_kv, rope):
    B, L, _ = x.shape
    sh1, sc1, g1, sh2, sc2, g2 = jnp.split(mod, 6, axis=-1)
    h = rmsnorm(x, p['norm1_g']) * (1 + sc1) + sh1
    proj = h @ p['w_in']
    ml_q, ml_k, ml_v, ml_o, ml_g, hy_u, at_q, at_k, at_v, br_g = jnp.split(proj, IN_OFFSETS, axis=-1)
    y_ml, new_ml_state = mlstm_bidir(ml_q, ml_k, ml_v, ml_o, ml_g, p['ml_gate_bias'], p['ml_norm_g'], ml_state)
    y_hy = hyena(hy_u, p['hy_conv_w'], p['hy_w1'], p['hy_b1'], p['hy_w2'], p['hy_b2'], p['hy_w3'],
                 p['hy_decay'], p['hy_skip'])
    q = rmsnorm(at_q.reshape(B, L, ATT_HEADS, HEAD_DIM), p['q_norm_g'])
    k = rmsnorm(at_k.reshape(B, L, ATT_KV_HEADS, HEAD_DIM), p['k_norm_g'])
    v = at_v.reshape(B, L, ATT_KV_HEADS, HEAD_DIM)
    if ext_kv is None:
        y_at = block_attention(q, k, v)
    else:
        q = apply_rope(q, *rope)
        k = apply_rope(k, *rope)
        k_ctx, v_ctx = ext_kv
        y_at = block_attention(q, jnp.concatenate([k, k_ctx.astype(k.dtype)], axis=1),
                               jnp.concatenate([v, v_ctx.astype(v.dtype)], axis=1))
    y_br = jnp.stack([y_ml, y_hy, y_at], axis=2)
    p_br = jnp.einsum('blnw,nwd->blnd', y_br, p['w_branch'])
    gates = jax.nn.sigmoid(br_g.reshape(B, L, N_BRANCH, D_MODEL))
    mixed = jnp.sum(gates * p_br, axis=2) @ p['w_out']
    x = x + g1 * mixed
    h2 = rmsnorm(x, p['norm2_g']) * (1 + sc2) + sh2
    x = x + g2 * peer(h2, p['peer_wq'], p['peer_keys'], p['peer_u'], p['peer_v'])
    return x, (k, v), new_ml_state


def setup_inputs(seed: int = 0) -> dict:
    key = jax.random.key(seed)
    ks = jax.random.split(key, 40)
    f32 = jnp.float32

    def nrm(i, shape, scale=1.0):
        return scale * jax.random.normal(ks[i], shape, f32)

    fgate = jnp.linspace(3.0, 6.0, ML_HEADS, dtype=f32)
    ml_gate_bias = jnp.concatenate([nrm(12, (DEPTH, 2, 1, ML_HEADS), 0.1),
                                    fgate + nrm(13, (DEPTH, 2, 1, ML_HEADS), 0.1)], axis=2)
    hy_decay = jnp.linspace(3.0, 15.0, HY_WIDTH, dtype=f32) * jnp.exp(nrm(22, (DEPTH, HY_ORDER, 2, HY_WIDTH), 0.05))
    return {
        'x_prompt': nrm(0, (BATCH, SEQ, D_MODEL)),
        'x_sample': nrm(1, (DEC_BATCH, DEC_SEQ, D_MODEL)),
        'c': nrm(2, (DEC_BATCH, D_MODEL)),
        'cache_k': nrm(3, (DEC_BATCH, DEPTH, PAST_LEN, ATT_KV_HEADS, HEAD_DIM)),
        'cache_v': nrm(4, (DEC_BATCH, DEPTH, PAST_LEN, ATT_KV_HEADS, HEAD_DIM)),
        'state_C': nrm(5, (DEC_BATCH, DEPTH, 2, ML_HEADS, ML_DK, ML_DV), 0.1),
        'state_n': nrm(6, (DEC_BATCH, DEPTH, 2, ML_HEADS, ML_DK), 0.1),
        'state_m': 1.0 + nrm(7, (DEC_BATCH, DEPTH, 2, ML_HEADS), 0.5),
        'c_ctx': nrm(8, (D_MODEL,)),
        'w_ada': nrm(9, (DEPTH, D_MODEL, 6 * D_MODEL), 0.5 * D_MODEL ** -0.5),
        'b_ada': nrm(10, (DEPTH, 6 * D_MODEL), 0.02),
        'norm1_g': 1.0 + nrm(11, (DEPTH, D_MODEL), 0.05),
        'norm2_g': 1.0 + nrm(14, (DEPTH, D_MODEL), 0.05),
        'w_in': nrm(15, (DEPTH, D_MODEL, IN_COLS), D_MODEL ** -0.5),
        'ml_gate_bias': ml_gate_bias,
        'ml_norm_g': 1.0 + nrm(16, (DEPTH, ML_WIDTH), 0.05),
        'hy_conv_w': nrm(17, (DEPTH, 3, 3 * HY_WIDTH), 3 ** -0.5),
        'hy_w1': nrm(18, (DEPTH, HY_FEAT, HY_FFN), HY_FEAT ** -0.5),
        'hy_b1': nrm(19, (DEPTH, HY_FFN), 0.1),
        'hy_w2': nrm(20, (DEPTH, HY_FFN, HY_FFN), HY_FFN ** -0.5),
        'hy_b2': nrm(21, (DEPTH, HY_FFN), 0.1),
        'hy_w3': nrm(23, (DEPTH, HY_FFN, HY_ORDER * 2 * HY_WIDTH), HY_FFN ** -0.5),
        'hy_decay': hy_decay,
        'hy_skip': nrm(24, (DEPTH, HY_ORDER, HY_WIDTH), 0.5),
        'q_norm_g': 1.0 + nrm(25, (DEPTH, HEAD_DIM), 0.05),
        'k_norm_g': 1.0 + nrm(26, (DEPTH, HEAD_DIM), 0.05),
        'w_branch': nrm(27, (DEPTH, N_BRANCH, BRANCH_WIDTH, D_MODEL), BRANCH_WIDTH ** -0.5),
        'w_out': nrm(28, (DEPTH, D_MODEL, D_MODEL), D_MODEL ** -0.5),
        'peer_wq': nrm(29, (DEPTH, D_MODEL, PEER_HEADS * PEER_DKEY), D_MODEL ** -0.5),
        'peer_keys': nrm(30, (DEPTH, PEER_HEADS, 2, N_KEYS, PEER_DHALF), PEER_DHALF ** -0.5),
        'peer_u': nrm(31, (DEPTH, N_EXPERTS, D_MODEL), D_MODEL ** -0.5),
        'peer_v': nrm(32, (DEPTH, N_EXPERTS, D_MODEL)),
        'final_g': 1.0 + nrm(33, (D_MODEL,), 0.05),
    }


def reference(x_prompt, x_sample, c, cache_k, cache_v, state_C, state_n, state_m, c_ctx,
              w_ada, b_ada, norm1_g, norm2_g, w_in, ml_gate_bias, ml_norm_g, hy_conv_w,
              hy_w1, hy_b1, hy_w2, hy_b2, hy_w3, hy_decay, hy_skip, q_norm_g, k_norm_g,
              w_branch, w_out, peer_wq, peer_keys, peer_u, peer_v, final_g):
    layers = [dict(norm1_g=norm1_g[l], norm2_g=norm2_g[l], w_in=w_in[l], ml_gate_bias=ml_gate_bias[l],
                   ml_norm_g=ml_norm_g[l], hy_conv_w=hy_conv_w[l], hy_w1=hy_w1[l], hy_b1=hy_b1[l],
                   hy_w2=hy_w2[l], hy_b2=hy_b2[l], hy_w3=hy_w3[l], hy_decay=hy_decay[l], hy_skip=hy_skip[l],
                   q_norm_g=q_norm_g[l], k_norm_g=k_norm_g[l], w_branch=w_branch[l], w_out=w_out[l],
                   peer_wq=peer_wq[l], peer_keys=peer_keys[l], peer_u=peer_u[l], peer_v=peer_v[l])
              for l in range(DEPTH)]

    bp = x_prompt.shape[0]
    zero_state = (jnp.zeros((bp, 2, ML_HEADS, ML_DK, ML_DV), jnp.float32),
                  jnp.zeros((bp, 2, ML_HEADS, ML_DK), jnp.float32),
                  jnp.zeros((bp, 2, ML_HEADS), jnp.float32))
    xp = x_prompt
    ks, vs, Cs, ns, ms = [], [], [], [], []
    for l in range(DEPTH):
        mod = (jax.nn.silu(c_ctx) @ w_ada[l] + b_ada[l])[None, None, :]
        xp, (k_l, v_l), (C_l, n_l, m_l) = trunk_layer(xp, mod, layers[l], zero_state, None, None)
        ks.append(k_l)
        vs.append(v_l)
        Cs.append(C_l)
        ns.append(n_l)
        ms.append(m_l)

    rope = grid_rope_tables(x_sample.shape[1])
    xs = x_sample
    for l in range(DEPTH):
        mod = (jax.nn.silu(c) @ w_ada[l] + b_ada[l])[:, None, :]
        xs, _, _ = trunk_layer(xs, mod, layers[l], (state_C[:, l], state_n[:, l], state_m[:, l]),
                               (cache_k[:, l], cache_v[:, l]), rope)

    y_prompt = rmsnorm(xp, final_g)
    y_sample = rmsnorm(xs, final_g)
    new_k = jnp.stack(ks, axis=1)
    new_v = jnp.stack(vs, axis=1)
    new_C = jnp.stack(Cs, axis=1)
    new_n = jnp.stack(ns, axis=1)
    new_m = jnp.stack(ms, axis=1)
    return (y_prompt, y_sample, new_k, new_v, new_C, new_n, new_m)
```

```python
import functools

import numpy as np
import jax
import jax.numpy as jnp
from jax import lax
from jax.experimental import pallas as pl
from jax.experimental.pallas import tpu as pltpu

F32 = jnp.float32
BF16 = jnp.bfloat16
EPS = 1e-6

GRID_W = 64
ML_HEADS = 4
ATT_HEADS = 8
ATT_KV_HEADS = 2
ROPE_BASE = 10000.0
HY_BANDS = 16
HY_ORDER = 2
N_BRANCH = 3
PEER_HEADS = 8
PEER_TOPK = 16

VMEM_LIMIT_BYTES = 56 * 1024 * 1024
LANES = 128
ML_CHUNK = 256
GATE_PAD = 512


def _cparams(*sem):
    return pltpu.CompilerParams(dimension_semantics=sem, vmem_limit_bytes=VMEM_LIMIT_BYTES)


def _tile(n, pref):
    t = min(n, pref)
    while n % t:
        t //= 2
    return t


def _nt_dot(a, b):
    return lax.dot_general(a, b, (((1,), (1,)), ((), ())), preferred_element_type=F32)


def _ada_kernel(c_ref, w_ref, b_ref, o_ref):
    c = c_ref[...]
    a = c * jax.nn.sigmoid(c)
    o_ref[0] = jnp.dot(a.astype(BF16), w_ref[0].astype(BF16), preferred_element_type=F32) + b_ref[0]


def _ada(cond, w_ada, b_ada):
    depth, d, n = w_ada.shape
    r = cond.shape[0]
    tn = _tile(n, 1024)
    return pl.pallas_call(
        _ada_kernel,
        out_shape=jax.ShapeDtypeStruct((depth, r, n), F32),
        grid=(depth, n // tn),
        in_specs=[pl.BlockSpec((r, d), lambda l, j: (0, 0)),
                  pl.BlockSpec((1, d, tn), lambda l, j: (l, 0, j)),
                  pl.BlockSpec((1, 1, tn), lambda l, j: (l, 0, j))],
        out_specs=pl.BlockSpec((1, r, tn), lambda l, j: (l, 0, j)),
        compiler_params=_cparams("parallel", "parallel"),
        name="ada",
    )(cond, w_ada, b_ada.reshape(depth, 1, n))


def _modnorm_matmul_kernel(x_ref, mod_ref, g_ref, w_ref, o_ref, h_ref, *, shift_idx):
    @pl.when(pl.program_id(1) == 0)
    def _():
        x = x_ref[...]
        y = x * lax.rsqrt(jnp.mean(x * x, axis=-1, keepdims=True) + EPS) * g_ref[...]
        sh = mod_ref[0, shift_idx:shift_idx + 1, :]
        sc = mod_ref[0, shift_idx + 1:shift_idx + 2, :]
        h_ref[...] = (y * (1.0 + sc) + sh).astype(BF16)

    o_ref[...] = jnp.dot(h_ref[...], w_ref[...], preferred_element_type=F32).astype(o_ref.dtype)


def _modnorm_matmul(x, mod, g, w, seq_row, *, shift_idx, out_dtype, tm, tn):
    t, d = x.shape
    n = w.shape[1]
    return pl.pallas_call(
        functools.partial(_modnorm_matmul_kernel, shift_idx=shift_idx),
        out_shape=(jax.ShapeDtypeStruct((t, n), out_dtype), jax.ShapeDtypeStruct((t, d), BF16)),
        grid=(t // tm, n // tn),
        in_specs=[pl.BlockSpec((tm, d), lambda i, j: (i, 0)),
                  pl.BlockSpec((1, 6, d), lambda i, j: (seq_row(i), 0, 0)),
                  pl.BlockSpec((1, d), lambda i, j: (0, 0)),
                  pl.BlockSpec((d, tn), lambda i, j: (0, j))],
        out_specs=(pl.BlockSpec((tm, tn), lambda i, j: (i, j)),
                   pl.BlockSpec((tm, d), lambda i, j: (i, 0))),
        compiler_params=_cparams("parallel", "arbitrary"),
        name="modnorm_matmul",
    )(x, mod, g.reshape(1, d), w)


def _log_sigmoid(x):
    return jnp.minimum(x, 0.0) - jnp.log1p(jnp.exp(-jnp.abs(x)))


def _mlstm_chunk(qc, kc, vc, lf_col, ig_col, lf_row, ig_row, C, n, m, fwd, k_scale):
    tc = qc.shape[0]
    r = lax.broadcasted_iota(jnp.int32, (tc, tc), 0)
    c = lax.broadcasted_iota(jnp.int32, (tc, tc), 1)
    mask = (c <= r) if fwd else (c >= r)
    mask_t = (r <= c) if fwd else (r >= c)
    b_col = jnp.sum(jnp.where(mask, lf_row, 0.0), axis=1, keepdims=True)
    b_row = jnp.sum(jnp.where(mask_t, lf_col, 0.0), axis=0, keepdims=True)
    dmat = jnp.where(mask, b_col - b_row + ig_row, -jnp.inf)
    m_t = jnp.maximum(b_col + m, jnp.max(dmat, axis=1, keepdims=True))
    w_in = jnp.exp(dmat - m_t)
    w_prev = jnp.exp(b_col + m - m_t)
    qb, kb, vb = qc.astype(BF16), kc.astype(BF16), vc.astype(BF16)
    s = _nt_dot(qb, kb) * k_scale * w_in
    num = (jnp.dot(s.astype(BF16), vb, preferred_element_type=F32)
           + w_prev * jnp.dot(qb, C.astype(BF16), preferred_element_type=F32))
    den = jnp.sum(s, axis=1, keepdims=True) + w_prev * jnp.sum(qc * n, axis=1, keepdims=True)
    h = num / jnp.maximum(jnp.abs(den), jnp.exp(-m_t))
    b_last = jnp.sum(lf_row, axis=1, keepdims=True)
    lw_col = b_last - b_col + ig_col
    lw_row = b_last - b_row + ig_row
    m_new = jnp.maximum(b_last + m, jnp.max(lw_row, axis=1, keepdims=True))
    decay = jnp.exp(b_last + m - m_new)
    kw = kc * k_scale * jnp.exp(lw_col - m_new)
    C_new = decay * C + jnp.dot(kw.T.astype(BF16), vb, preferred_element_type=F32)
    n_new = decay * n + jnp.sum(kw, axis=0, keepdims=True)
    return h, C_new, n_new, m_new


def _mlstm_kernel(*refs, seq, chunk, has_state, emit_state, k_scale):
    q_ref, k_ref, v_ref, o_ref, g_ref, bias_ref, ng_ref = refs[:7]
    pos = 7
    if has_state:
        c0_ref, n0_ref, m0_ref = refs[pos:pos + 3]
        pos += 3
    y_ref = refs[pos]
    pos += 1
    if emit_state:
        c_out, n_out, m_out = refs[pos:pos + 3]
        pos += 3
    hacc = refs[pos]

    g = g_ref[...] + bias_ref[0]
    gt = g.T
    dk = q_ref.shape[1]
    nchunks = seq // chunk
    for d in range(2):
        fwd = d == 0
        ig_col_all = g[:, 2 * d:2 * d + 1]
        lf_col_all = _log_sigmoid(g[:, 2 * d + 1:2 * d + 2])
        ig_row_all = gt[2 * d:2 * d + 1, :]
        lf_row_all = _log_sigmoid(gt[2 * d + 1:2 * d + 2, :])
        if has_state:
            C = c0_ref[0, d, 0]
            n = n0_ref[0, d, 0]
            m = m0_ref[0, d, 0][:, 0:1]
        else:
            C = jnp.zeros((dk, v_ref.shape[1]), F32)
            n = jnp.zeros((1, dk), F32)
            m = jnp.zeros((1, 1), F32)
        order = range(nchunks) if fwd else range(nchunks - 1, -1, -1)
        for ci in order:
            lo, hi = ci * chunk, (ci + 1) * chunk
            h, C, n, m = _mlstm_chunk(
                q_ref[lo:hi, :], k_ref[lo:hi, :], v_ref[lo:hi, :],
                lf_col_all[lo:hi], ig_col_all[lo:hi], lf_row_all[:, lo:hi], ig_row_all[:, lo:hi],
                C, n, m, fwd, k_scale)
            if fwd:
                hacc[lo:hi, :] = h
            else:
                hacc[lo:hi, :] += h
        if emit_state:
            c_out[0, d, 0] = C
            n_out[0, d, 0] = n
            m_out[0, d, 0] = jnp.broadcast_to(m, (1, LANES))
    hh = hacc[...]
    hn = hh * lax.rsqrt(jnp.mean(hh * hh, axis=-1, keepdims=True) + EPS) * ng_ref[0]
    y_ref[...] = (jax.nn.sigmoid(o_ref[...]) * hn).astype(y_ref.dtype)


def _mlstm(proj, cols, row0, nb, seq, gate_bias, norm_g, state, emit_state):
    heads = ML_HEADS
    dk = (cols["ml_k"] - cols["ml_q"]) // heads
    chunk = min(ML_CHUNK, seq)
    rb = row0 // seq

    def col_spec(off, width):
        return pl.BlockSpec((seq, width), lambda b, h, off=off, width=width: (rb + b, off // width + h))

    in_specs = [col_spec(cols["ml_q"], dk), col_spec(cols["ml_k"], dk), col_spec(cols["ml_v"], dk),
                col_spec(cols["ml_o"], dk), col_spec(cols["ml_g"], LANES),
                pl.BlockSpec((1, 1, LANES), lambda b, h: (h, 0, 0)),
                pl.BlockSpec((1, 1, dk), lambda b, h: (h, 0, 0))]
    args = [proj, proj, proj, proj, proj, gate_bias, norm_g.reshape(heads, 1, dk)]
    has_state = state is not None
    if has_state:
        c0, n0, m0 = state
        in_specs += [pl.BlockSpec((1, 2, 1, dk, dk), lambda b, h: (b, 0, h, 0, 0)),
                     pl.BlockSpec((1, 2, 1, 1, dk), lambda b, h: (b, 0, h, 0, 0)),
                     pl.BlockSpec((1, 2, 1, 1, LANES), lambda b, h: (b, 0, h, 0, 0))]
        args += [c0, n0.reshape(nb, 2, heads, 1, dk),
                 jnp.broadcast_to(m0[..., None, None], (nb, 2, heads, 1, LANES))]
    out_shape = [jax.ShapeDtypeStruct((nb * seq, heads * dk), BF16)]
    out_specs = [pl.BlockSpec((seq, dk), lambda b, h: (b, h))]
    if emit_state:
        out_shape += [jax.ShapeDtypeStruct((nb, 2, heads, dk, dk), F32),
                      jax.ShapeDtypeStruct((nb, 2, heads, 1, dk), F32),
                      jax.ShapeDtypeStruct((nb, 2, heads, 1, LANES), F32)]
        out_specs += [pl.BlockSpec((1, 2, 1, dk, dk), lambda b, h: (b, 0, h, 0, 0)),
                      pl.BlockSpec((1, 2, 1, 1, dk), lambda b, h: (b, 0, h, 0, 0)),
                      pl.BlockSpec((1, 2, 1, 1, LANES), lambda b, h: (b, 0, h, 0, 0))]
    outs = pl.pallas_call(
        functools.partial(_mlstm_kernel, seq=seq, chunk=chunk, has_state=has_state,
                          emit_state=emit_state, k_scale=float(dk) ** -0.5),
        out_shape=tuple(out_shape),
        grid=(nb, heads),
        in_specs=in_specs,
        out_specs=tuple(out_specs),
        scratch_shapes=[pltpu.VMEM((seq, dk), F32)],
        compiler_params=_cparams("parallel", "parallel"),
        name="mlstm",
    )(*args)
    if emit_state:
        y, c_new, n_new, m_new = outs
        return y, (c_new, n_new[:, :, :, 0, :], m_new[:, :, :, 0, 0])
    return outs[0], None


def _dft_mats(seq):
    k = np.arange(seq, dtype=np.int64)
    ang = np.pi * ((k[:, None] * k[None, :]) % (2 * seq)).astype(np.float64) / seq
    cos, sin = np.cos(ang), np.sin(ang)
    alt = np.where(k % 2 == 0, 1.0, -1.0)
    fwd_b = -sin
    fwd_b[0, :] = alt
    fwd = np.concatenate([cos, fwd_b], axis=0)
    inv_a = cos.T / seq
    inv_a[:, 0] = 0.5 / seq
    inv_b = -sin.T / seq
    inv_b[:, 0] = alt * 0.5 / seq
    inv = np.concatenate([inv_a, inv_b], axis=1)
    return fwd.astype(np.float32), inv.astype(np.float32)


def _hy_features(seq):
    pos = np.arange(seq, dtype=np.float64)
    t = pos / (seq - 1)
    bands = np.arange(1, HY_BANDS + 1, dtype=np.float64)
    ang = (2.0 * np.pi / seq) * pos[:, None] * bands[None, :]
    feat = np.concatenate([t[:, None], np.cos(ang), np.sin(ang)], axis=-1)
    feat = np.pad(feat, ((0, 0), (0, LANES - feat.shape[1])))
    return feat.astype(np.float32), t.astype(np.float32)[:, None]


def _hyfilt_kernel(feat_ref, t_ref, w1_ref, b1_ref, w2_ref, b2_ref, w3f_ref, w3b_ref, decf_ref, decb_ref,
                   fwd_ref, p_ref, fi_ref, s_ref):
    hp = lax.Precision.HIGHEST
    h = jnp.sin(jnp.dot(feat_ref[...], w1_ref[...], precision=hp, preferred_element_type=F32) + b1_ref[...])
    h = jnp.sin(jnp.dot(h, w2_ref[...], precision=hp, preferred_element_type=F32) + b2_ref[...])
    t = t_ref[...]
    seq = t.shape[0]
    row = lax.broadcasted_iota(jnp.int32, (seq, 1), 0)
    hf = jnp.dot(h, w3f_ref[...], precision=hp, preferred_element_type=F32) * jnp.exp(-t * decf_ref[0, 0])
    hb = jnp.dot(h, w3b_ref[...], precision=hp, preferred_element_type=F32) * jnp.exp(-t * decb_ref[0, 0])
    hb = jnp.where(row == 0, 0.0, hb)
    nrm = lax.rsqrt(jnp.sum(hf * hf, axis=0, keepdims=True) + jnp.sum(hb * hb, axis=0, keepdims=True) + EPS)
    gp = (hf + hb) * nrm
    gm = (hf - hb) * nrm
    fa = jnp.dot(fwd_ref[0:seq, :], gp, precision=hp, preferred_element_type=F32)
    fb = jnp.dot(fwd_ref[seq:2 * seq, :], gm, precision=hp, preferred_element_type=F32)
    alt = jnp.where(row % 2 == 0, 1.0, -1.0)
    f_nyq = jnp.sum(alt * gp, axis=0, keepdims=True)
    p_ref[0] = fa
    fi_ref[0] = jnp.where(row == 0, 0.0, fb)
    s_ref[0] = jnp.where(row == 0, f_nyq, fa)


def _hyena_filters(seq, w1, b1, w2, b2, w3, decay, fwd_f32):
    feat_np, t_np = _hy_features(seq)
    nfeat, ffn = w1.shape
    width = decay.shape[-1]
    pf = LANES - ffn
    w1p = jnp.pad(w1, ((0, LANES - nfeat), (0, pf)))
    w2p = jnp.pad(w2, ((0, pf), (0, pf)))
    w3p = jnp.pad(w3, ((0, pf), (0, 0)))
    b1p = jnp.pad(b1, (0, pf)).reshape(1, LANES)
    b2p = jnp.pad(b2, (0, pf)).reshape(1, LANES)
    ct = _tile(width, 256)
    nct = width // ct
    dec = decay.reshape(HY_ORDER * 2, 1, width)
    full = lambda shape: pl.BlockSpec(shape, lambda o, j: (0,) * len(shape))
    out_sd = jax.ShapeDtypeStruct((HY_ORDER, seq, width), F32)
    out_spec = pl.BlockSpec((1, seq, ct), lambda o, j: (o, 0, j))
    return pl.pallas_call(
        _hyfilt_kernel,
        out_shape=(out_sd, out_sd, out_sd),
        grid=(HY_ORDER, nct),
        in_specs=[full(feat_np.shape), full(t_np.shape), full(w1p.shape), full(b1p.shape),
                  full(w2p.shape), full(b2p.shape),
                  pl.BlockSpec((LANES, ct), lambda o, j: (0, o * 2 * nct + j)),
                  pl.BlockSpec((LANES, ct), lambda o, j: (0, (o * 2 + 1) * nct + j)),
                  pl.BlockSpec((1, 1, ct), lambda o, j: (o * 2, 0, j)),
                  pl.BlockSpec((1, 1, ct), lambda o, j: (o * 2 + 1, 0, j)),
                  full(fwd_f32.shape)],
        out_specs=(out_spec, out_spec, out_spec),
        compiler_params=_cparams("parallel", "parallel"),
        name="hyena_filters",
    )(jnp.asarray(feat_np), jnp.asarray(t_np), w1p, b1p, w2p, b2p, w3p, w3p, dec, dec, fwd_f32)


def _short_conv(u, w):
    seq = u.shape[0]
    row = lax.broadcasted_iota(jnp.int32, (seq, 1), 0)
    prev = jnp.where(row == 0, 0.0, pltpu.roll(u, 1, 0))
    nxt = jnp.where(row == seq - 1, 0.0, pltpu.roll(u, seq - 1, 0))
    return prev * w[0:1, :] + u * w[1:2, :] + nxt * w[2:3, :]


def _hyena_kernel(uv_ref, u1_ref, u2_ref, cwv_ref, cw1_ref, cw2_ref, fwd_ref, inv_ref,
                  p_ref, fi_ref, s_ref, skip_ref, y_ref):
    seq = uv_ref.shape[0]
    z = _short_conv(uv_ref[...], cwv_ref[...])
    gates = (_short_conv(u1_ref[...], cw1_ref[...]), _short_conv(u2_ref[...], cw2_ref[...]))
    for order in range(HY_ORDER):
        zf = jnp.dot(fwd_ref[...], z.astype(BF16), preferred_element_type=F32)
        a, b = zf[:seq], zf[seq:]
        p, fi, s = p_ref[order], fi_ref[order], s_ref[order]
        ya = a * p - b * fi
        yb = a * fi + b * s
        conv = (jnp.dot(inv_ref[:, :seq], ya.astype(BF16), preferred_element_type=F32)
                + jnp.dot(inv_ref[:, seq:], yb.astype(BF16), preferred_element_type=F32))
        z = gates[order] * (conv + skip_ref[order:order + 1, :] * z)
    y_ref[...] = z.astype(y_ref.dtype)


def _hyena(proj, cols, row0, nb, seq, conv_w, filt, skip, fwd_bf, inv_bf):
    width = skip.shape[-1]
    ct = _tile(width, 256)
    nct = width // ct
    rb = row0 // seq
    off = cols["hy"]
    p_arr, fi_arr, s_arr = filt

    def u_spec(part):
        return pl.BlockSpec((seq, ct), lambda b, j, part=part: (rb + b, off // ct + part * nct + j))

    def cw_spec(part):
        return pl.BlockSpec((3, ct), lambda b, j, part=part: (0, part * nct + j))

    full = lambda shape: pl.BlockSpec(shape, lambda b, j: (0,) * len(shape))
    f_spec = pl.BlockSpec((HY_ORDER, seq, ct), lambda b, j: (0, 0, j))
    return pl.pallas_call(
        _hyena_kernel,
        out_shape=jax.ShapeDtypeStruct((nb * seq, width), BF16),
        grid=(nb, nct),
        in_specs=[u_spec(0), u_spec(1), u_spec(2), cw_spec(0), cw_spec(1), cw_spec(2),
                  full(fwd_bf.shape), full(inv_bf.shape), f_spec, f_spec, f_spec,
                  pl.BlockSpec((HY_ORDER, ct), lambda b, j: (0, j))],
        out_specs=pl.BlockSpec((seq, ct), lambda b, j: (b, j)),
        compiler_params=_cparams("parallel", "parallel"),
        name="hyena",
    )(proj, proj, proj, conv_w, conv_w, conv_w, fwd_bf, inv_bf, p_arr, fi_arr, s_arr, skip)


def _rope_tables(seq, head_dim):
    nfreq = head_dim // 4
    rows = seq // GRID_W
    row = np.repeat(np.arange(rows, dtype=np.float64), GRID_W)
    col = np.tile(np.arange(GRID_W, dtype=np.float64), rows)
    inv = (ROPE_BASE ** (-2.0 * np.arange(nfreq, dtype=np.float32) / (2 * nfreq))).astype(np.float64)
    ar, ac = row[:, None] * inv, col[:, None] * inv
    cos = np.concatenate([np.cos(ar), np.cos(ar), np.cos(ac), np.cos(ac)], axis=1)
    sin = np.concatenate([-np.sin(ar), np.sin(ar), -np.sin(ac), np.sin(ac)], axis=1)
    return cos.astype(np.float32), sin.astype(np.float32)


def _rope(x, cos, sin):
    hd = x.shape[1]
    q = hd // 4
    lane = lax.broadcasted_iota(jnp.int32, x.shape, 1)
    first = (lane % (2 * q)) < q
    partner = jnp.where(first, pltpu.roll(x, hd - q, 1), pltpu.roll(x, q, 1))
    return x * cos + partner * sin


def _attn_kernel(*refs, rope, has_ctx, emit_kv, groups, bq, scale):
    q_ref, k_ref, v_ref, qg_ref, kg_ref = refs[:5]
    pos = 5
    if rope:
        cos_ref, sin_ref = refs[pos:pos + 2]
        pos += 2
    if has_ctx:
        kc_ref, vc_ref = refs[pos:pos + 2]
        pos += 2
    y_ref = refs[pos]
    pos += 1
    if emit_kv:
        ko_ref, vo_ref = refs[pos:pos + 2]

    seq, hd = k_ref.shape
    k = k_ref[...]
    kn = k * lax.rsqrt(jnp.mean(k * k, axis=-1, keepdims=True) + EPS) * kg_ref[...]
    v = v_ref[...]
    if emit_kv:
        ko_ref[...] = kn
        vo_ref[...] = v
    if rope:
        kn = _rope(kn, cos_ref[...], sin_ref[...])
    kb, vb = kn.astype(BF16), v.astype(BF16)
    if has_ctx:
        kcb, vcb = kc_ref[0, 0].astype(BF16), vc_ref[0, 0].astype(BF16)
    for g in range(groups):
        for qi in range(seq // bq):
            lo, hi = qi * bq, (qi + 1) * bq
            q = q_ref[lo:hi, g * hd:(g + 1) * hd]
            qn = q * lax.rsqrt(jnp.mean(q * q, axis=-1, keepdims=True) + EPS) * qg_ref[...]
            if rope:
                qn = _rope(qn, cos_ref[lo:hi, :], sin_ref[lo:hi, :])
            qb = qn.astype(BF16)
            s1 = _nt_dot(qb, kb) * scale
            mx = jnp.max(s1, axis=-1, keepdims=True)
            if has_ctx:
                s2 = _nt_dot(qb, kcb) * scale
                mx = jnp.maximum(mx, jnp.max(s2, axis=-1, keepdims=True))
            p1 = jnp.exp(s1 - mx)
            den = jnp.sum(p1, axis=-1, keepdims=True)
            o = jnp.dot(p1.astype(BF16), vb, preferred_element_type=F32)
            if has_ctx:
                p2 = jnp.exp(s2 - mx)
                den = den + jnp.sum(p2, axis=-1, keepdims=True)
                o = o + jnp.dot(p2.astype(BF16), vcb, preferred_element_type=F32)
            y_ref[lo:hi, g * hd:(g + 1) * hd] = (o / den).astype(y_ref.dtype)


def _attention(proj, cols, row0, nb, seq, q_g, k_g, ctx, layer, rope, emit_kv):
    hd = (cols["at_v"] - cols["at_k"]) // ATT_KV_HEADS
    groups = ATT_HEADS // ATT_KV_HEADS
    gw = groups * hd
    rb = row0 // seq
    in_specs = [pl.BlockSpec((seq, gw), lambda b, h: (rb + b, cols["at_q"] // gw + h)),
                pl.BlockSpec((seq, hd), lambda b, h: (rb + b, cols["at_k"] // hd + h)),
                pl.BlockSpec((seq, hd), lambda b, h: (rb + b, cols["at_v"] // hd + h)),
                pl.BlockSpec((1, hd), lambda b, h: (0, 0)),
                pl.BlockSpec((1, hd), lambda b, h: (0, 0))]
    args = [proj, proj, proj, q_g.reshape(1, hd), k_g.reshape(1, hd)]
    if rope:
        cos_np, sin_np = _rope_tables(seq, hd)
        in_specs += [pl.BlockSpec((seq, hd), lambda b, h: (0, 0))] * 2
        args += [jnp.asarray(cos_np), jnp.asarray(sin_np)]
    has_ctx = ctx is not None
    if has_ctx:
        ck, cv = ctx
        past = ck.shape[2]
        in_specs += [pl.BlockSpec((1, 1, past, hd), lambda b, h: (b, layer, 0, h))] * 2
        args += [ck, cv]
    out_shape = [jax.ShapeDtypeStruct((nb * seq, ATT_HEADS * hd), BF16)]
    out_specs = [pl.BlockSpec((seq, gw), lambda b, h: (b, h))]
    if emit_kv:
        out_shape += [jax.ShapeDtypeStruct((nb * seq, ATT_KV_HEADS * hd), F32)] * 2
        out_specs += [pl.BlockSpec((seq, hd), lambda b, h: (b, h))] * 2
    outs = pl.pallas_call(
        functools.partial(_attn_kernel, rope=rope, has_ctx=has_ctx, emit_kv=emit_kv, groups=groups,
                          bq=min(seq, 256), scale=float(hd) ** -0.5),
        out_shape=tuple(out_shape),
        grid=(nb, ATT_KV_HEADS),
        in_specs=in_specs,
        out_specs=tuple(out_specs),
        compiler_params=_cparams("parallel", "parallel"),
        name="attention",
    )(*args)
    if emit_kv:
        return outs
    return outs[0], None, None


def _branch_kernel(y0_ref, y1_ref, y2_ref, g0_ref, g1_ref, g2_ref, w_ref, o_ref):
    acc = None
    for n, (y_ref, g_ref) in enumerate(((y0_ref, g0_ref), (y1_ref, g1_ref), (y2_ref, g2_ref))):
        p = jnp.dot(y_ref[...], w_ref[n], preferred_element_type=F32)
        term = jax.nn.sigmoid(g_ref[...]) * p
        acc = term if acc is None else acc + term
    o_ref[...] = acc.astype(o_ref.dtype)


def _branch_merge(ys, proj, cols, w_branch, tm, tn):
    t, bw = ys[0].shape
    d = w_branch.shape[2]
    goff = cols["br_g"]
    y_spec = pl.BlockSpec((tm, bw), lambda i, j: (i, 0))

    def g_spec(n):
        return pl.BlockSpec((tm, tn), lambda i, j, n=n: (i, (goff + n * d) // tn + j))

    return pl.pallas_call(
        _branch_kernel,
        out_shape=jax.ShapeDtypeStruct((t, d), BF16),
        grid=(t // tm, d // tn),
        in_specs=[y_spec, y_spec, y_spec, g_spec(0), g_spec(1), g_spec(2),
                  pl.BlockSpec((N_BRANCH, bw, tn), lambda i, j: (0, 0, j))],
        out_specs=pl.BlockSpec((tm, tn), lambda i, j: (i, j)),
        compiler_params=_cparams("parallel", "arbitrary"),
        name="branch_merge",
    )(*ys, proj, proj, proj, w_branch)


def _proj_residual_kernel(a_ref, w_ref, x_ref, mod_ref, o_ref, *, gate_idx):
    y = jnp.dot(a_ref[...], w_ref[...], preferred_element_type=F32)
    o_ref[...] = x_ref[...] + mod_ref[0, gate_idx:gate_idx + 1, :] * y


def _proj_residual(a, w, x, mod, seq_row, *, gate_idx, tm, tn):
    t, k = a.shape
    d = w.shape[1]
    return pl.pallas_call(
        functools.partial(_proj_residual_kernel, gate_idx=gate_idx),
        out_shape=jax.ShapeDtypeStruct((t, d), F32),
        grid=(t // tm, d // tn),
        in_specs=[pl.BlockSpec((tm, k), lambda i, j: (i, 0)),
                  pl.BlockSpec((k, tn), lambda i, j: (0, j)),
                  pl.BlockSpec((tm, tn), lambda i, j: (i, j)),
                  pl.BlockSpec((1, 6, tn), lambda i, j: (seq_row(i), 0, j))],
        out_specs=pl.BlockSpec((tm, tn), lambda i, j: (i, j)),
        compiler_params=_cparams("parallel", "arbitrary"),
        name="proj_residual",
    )(a, w, x, mod)


def _topk_rows(s, k):
    rows = s.shape[0]
    iota = lax.broadcasted_iota(jnp.int32, s.shape, 0).astype(F32)
    rank = jnp.full(s.shape, float(k), F32)
    vals = []
    for r in range(k):
        mx = jnp.max(s, axis=0, keepdims=True)
        idx = jnp.min(jnp.where(s == mx, iota, float(rows)), axis=0, keepdims=True)
        sel = iota == idx
        rank = jnp.where(sel, float(r), rank)
        vals.append(mx)
        s = jnp.where(sel, -jnp.inf, s)
    return jnp.concatenate(vals, axis=0), rank


def _peer_route_kernel(q_ref, keys_ref, u1_ref, cnt_ref, u2_ref, rk2_ref):
    k = PEER_TOPK
    dh = keys_ref.shape[3]
    for h in range(PEER_HEADS):
        s1 = _nt_dot(keys_ref[h, 0], q_ref[:, (2 * h) * dh:(2 * h + 1) * dh])
        s2 = _nt_dot(keys_ref[h, 1], q_ref[:, (2 * h + 1) * dh:(2 * h + 2) * dh])
        sv1, rk1 = _topk_rows(s1, k)
        sv2, rk2 = _topk_rows(s2, k)
        cand = jnp.concatenate([sv1[p:p + 1, :] + sv2 for p in range(k)], axis=0)
        fv, rkc = _topk_rows(cand, k)
        z = jnp.sum(jnp.exp(fv - fv[0:1, :]), axis=0, keepdims=True)
        sel = jnp.where(rkc < float(k), 1.0, 0.0)
        cnt_i = jnp.zeros_like(s1)
        for p in range(k):
            cnt_p = jnp.sum(sel[p * k:(p + 1) * k, :], axis=0, keepdims=True)
            cnt_i = cnt_i + jnp.where(rk1 == float(p), cnt_p, 0.0)
        u1_ref[h] = jnp.exp(s1 - sv1[0:1, :]) / z
        cnt_ref[h] = cnt_i
        u2_ref[h] = jnp.exp(s2 - sv2[0:1, :])
        rk2_ref[h] = rk2


def _peer_route(q, keys_bf, tr):
    t = q.shape[0]
    heads, _, nkeys, dh = keys_bf.shape
    sd = jax.ShapeDtypeStruct((heads, nkeys, t), F32)
    spec = pl.BlockSpec((heads, nkeys, tr), lambda i: (0, 0, i))
    return pl.pallas_call(
        _peer_route_kernel,
        out_shape=(sd, sd, sd, sd),
        grid=(t // tr,),
        in_specs=[pl.BlockSpec((tr, q.shape[1]), lambda i: (i, 0)),
                  pl.BlockSpec(keys_bf.shape, lambda i: (0, 0, 0, 0))],
        out_specs=(spec, spec, spec, spec),
        compiler_params=_cparams("parallel"),
        name="peer_route",
    )(q, keys_bf)


def _peer_dense_kernel(h_ref, u_ref, vt_ref, u1_ref, cnt_ref, u2_ref, rk2_ref, x_ref, mod_ref, fg_ref,
                       o_ref, acc_ref, *, final_norm):
    e = pl.program_id(1)

    @pl.when(e == 0)
    def _():
        acc_ref[...] = jnp.zeros_like(acc_ref)

    te = u_ref.shape[0]
    nkeys = u2_ref.shape[1]
    act = jax.nn.gelu(_nt_dot(u_ref[...], h_ref[...]), approximate=True)
    parts = []
    for r in range(te // nkeys):
        i = e * (te // nkeys) + r
        w = None
        for h in range(PEER_HEADS):
            u1 = u1_ref[h, pl.ds(i, 1), :]
            cn = cnt_ref[h, pl.ds(i, 1), :]
            term = u1 * jnp.where(rk2_ref[h] < cn, u2_ref[h], 0.0)
            w = term if w is None else w + term
        parts.append(w)
    w = parts[0] if len(parts) == 1 else jnp.concatenate(parts, axis=0)
    acc_ref[...] += jnp.dot(vt_ref[...], (w * act).astype(BF16), preferred_element_type=F32)

    @pl.when(e == pl.num_programs(1) - 1)
    def _():
        y = x_ref[...] + mod_ref[0, 5:6, :] * acc_ref[...].T
        if final_norm:
            y = y * lax.rsqrt(jnp.mean(y * y, axis=-1, keepdims=True) + EPS) * fg_ref[...]
        o_ref[...] = y


def _peer_dense(h2, u_bf, vt_bf, route, x, mod, final_g, seq_row, *, final_norm, tm, te):
    t, d = x.shape
    ne = u_bf.shape[0]
    heads, nkeys, _ = route[0].shape
    r_spec = pl.BlockSpec((heads, nkeys, tm), lambda i, e: (0, 0, i))
    return pl.pallas_call(
        functools.partial(_peer_dense_kernel, final_norm=final_norm),
        out_shape=jax.ShapeDtypeStruct((t, d), F32),
        grid=(t // tm, ne // te),
        in_specs=[pl.BlockSpec((tm, d), lambda i, e: (i, 0)),
                  pl.BlockSpec((te, d), lambda i, e: (e, 0)),
                  pl.BlockSpec((d, te), lambda i, e: (0, e)),
                  r_spec, r_spec, r_spec, r_spec,
                  pl.BlockSpec((tm, d), lambda i, e: (i, 0)),
                  pl.BlockSpec((1, 6, d), lambda i, e: (seq_row(i), 0, 0)),
                  pl.BlockSpec((1, d), lambda i, e: (0, 0))],
        out_specs=pl.BlockSpec((tm, d), lambda i, e: (i, 0)),
        scratch_shapes=[pltpu.VMEM((d, tm), F32)],
        compiler_params=_cparams("parallel", "arbitrary"),
        name="peer_dense",
    )(h2, u_bf, vt_bf, *route, x, mod, final_g.reshape(1, d))


def _in_layout(bw, kvw, d):
    sizes = (("ml_q", bw), ("ml_k", bw), ("ml_v", bw), ("ml_o", bw), ("hy", 3 * bw),
             ("at_q", bw), ("at_k", kvw), ("at_v", kvw), ("br_g", N_BRANCH * d), ("ml_g", GATE_PAD))
    cols, off = {}, 0
    for name, size in sizes:
        cols[name] = off
        off += size
    return cols, off


def _reorder_w_in(w, bw):
    d = w.shape[0]
    ngate = 4 * ML_HEADS
    gates = w[:, 4 * bw:4 * bw + ngate].reshape(d, 4, ML_HEADS)
    gates = jnp.transpose(gates, (0, 2, 1))
    gates = jnp.pad(gates, ((0, 0), (0, 0), (0, GATE_PAD // ML_HEADS - 4))).reshape(d, GATE_PAD)
    return jnp.concatenate([w[:, :4 * bw], w[:, 4 * bw + ngate:], gates], axis=1).astype(BF16)


def kernel(x_prompt, x_sample, c, cache_k, cache_v, state_C, state_n, state_m, c_ctx, w_ada, b_ada, norm1_g, norm2_g, w_in, ml_gate_bias, ml_norm_g, hy_conv_w, hy_w1, hy_b1, hy_w2, hy_b2, hy_w3, hy_decay, hy_skip, q_norm_g, k_norm_g, w_branch, w_out, peer_wq, peer_keys, peer_u, peer_v, final_g):
    nbp, seq_p, d = x_prompt.shape
    nbs, seq_s, _ = x_sample.shape
    depth = w_ada.shape[0]
    bw = d // 2
    kvw = cache_k.shape[3] * cache_k.shape[4]
    tp, ts = nbp * seq_p, nbs * seq_s
    cols, ncols = _in_layout(bw, kvw, d)
    assert w_in.shape[2] == ncols - GATE_PAD + 4 * ML_HEADS

    tm = _tile(seq_s, 512)
    assert tp % tm == 0
    tiles_p, tiles_per_seq = tp // tm, seq_s // tm

    def seq_row(i):
        return jnp.where(i < tiles_p, 0, 1 + (i - tiles_p) // tiles_per_seq)

    x = jnp.concatenate([x_prompt.reshape(tp, d), x_sample.reshape(ts, d)], axis=0)
    cond = jnp.concatenate([c_ctx[None, :], c], axis=0)
    nrow = cond.shape[0]
    cond = jnp.pad(cond, ((0, (-nrow) % 8), (0, 0)))
    mod_all = _ada(cond, w_ada, b_ada).reshape(depth, cond.shape[0], 6, d)

    ctx_k = cache_k.reshape(nbs, depth, cache_k.shape[2], kvw)
    ctx_v = cache_v.reshape(nbs, depth, cache_v.shape[2], kvw)
    dft = {}
    for seq in {seq_p, seq_s}:
        fwd_np, inv_np = _dft_mats(seq)
        dft[seq] = (jnp.asarray(fwd_np), jnp.asarray(fwd_np).astype(BF16), jnp.asarray(inv_np).astype(BF16))

    new_k, new_v, new_c, new_n, new_m = [], [], [], [], []
    for l in range(depth):
        mod = mod_all[l]
        w_in_l = _reorder_w_in(w_in[l], bw)
        proj, _ = _modnorm_matmul(x, mod, norm1_g[l], w_in_l, seq_row, shift_idx=0, out_dtype=F32,
                                  tm=tm, tn=_tile(ncols, 512))
        gb = ml_gate_bias[l].reshape(4, ML_HEADS).T
        gb = jnp.pad(gb, ((0, 0), (0, LANES - 4))).reshape(ML_HEADS, 1, LANES)

        ys = []
        for (row0, nb, seq, is_ctx) in ((0, nbp, seq_p, True), (tp, nbs, seq_s, False)):
            state = None if is_ctx else (state_C[:, l], state_n[:, l], state_m[:, l])
            y_ml, st = _mlstm(proj, cols, row0, nb, seq, gb, ml_norm_g[l], state, emit_state=is_ctx)
            fwd_f32, fwd_bf, inv_bf = dft[seq]
            filt = _hyena_filters(seq, hy_w1[l], hy_b1[l], hy_w2[l], hy_b2[l], hy_w3[l], hy_decay[l], fwd_f32)
            y_hy = _hyena(proj, cols, row0, nb, seq, hy_conv_w[l], filt, hy_skip[l], fwd_bf, inv_bf)
            y_at, k_l, v_l = _attention(proj, cols, row0, nb, seq, q_norm_g[l], k_norm_g[l],
                                        None if is_ctx else (ctx_k, ctx_v), l, rope=not is_ctx, emit_kv=is_ctx)
            ys.append((y_ml, y_hy, y_at))
            if is_ctx:
                new_k.append(k_l.reshape(nbp, seq_p, ATT_KV_HEADS, kvw // ATT_KV_HEADS))
                new_v.append(v_l.reshape(nbp, seq_p, ATT_KV_HEADS, kvw // ATT_KV_HEADS))
                new_c.append(st[0])
                new_n.append(st[1])
                new_m.append(st[2])
        y_all = [jnp.concatenate([ys[0][n], ys[1][n]], axis=0) for n in range(N_BRANCH)]
        mixed = _branch_merge(y_all, proj, cols, w_branch[l].astype(BF16), tm, 512)
        x = _proj_residual(mixed, w_out[l].astype(BF16), x, mod, seq_row, gate_idx=2, tm=tm, tn=512)

        q, h2 = _modnorm_matmul(x, mod, norm2_g[l], peer_wq[l].astype(BF16), seq_row, shift_idx=3,
                                out_dtype=BF16, tm=tm, tn=512)
        route = _peer_route(q, peer_keys[l].astype(BF16), _tile(tm, 256))
        x = _peer_dense(h2, peer_u[l].astype(BF16), peer_v[l].T.astype(BF16), route, x, mod, final_g,
                        seq_row, final_norm=(l == depth - 1), tm=tm, te=256)

    y_prompt = x[:tp].reshape(nbp, seq_p, d)
    y_sample = x[tp:].reshape(nbs, seq_s, d)
    return (y_prompt, y_sample, jnp.stack(new_k, axis=1), jnp.stack(new_v, axis=1),
            jnp.stack(new_c, axis=1), jnp.stack(new_n, axis=1), jnp.stack(new_m, axis=1))
```

```python
import functools

import numpy as np
import jax
import jax.numpy as jnp
from jax import lax
from jax.experimental import pallas as pl
from jax.experimental.pallas import tpu as pltpu

F32 = jnp.float32
BF16 = jnp.bfloat16
EPS = 1e-6

GRID_W = 64
ML_HEADS = 4
ATT_HEADS = 8
ATT_KV_HEADS = 2
ROPE_BASE = 10000.0
HY_BANDS = 16
HY_ORDER = 2
N_BRANCH = 3
PEER_HEADS = 8
PEER_TOPK = 16

VMEM_LIMIT_BYTES = 60 * 1024 * 1024
LANES = 128
BF16_ROWS = 16
ML_CHUNK = 256
TOKEN_TILE = 1024
ROUTE_TILE = 256
EXPERT_TILE = 1024
EXPERT_SUB = 256
GATE_PAD = 512


def _cparams(*sem):
    return pltpu.CompilerParams(dimension_semantics=sem, vmem_limit_bytes=VMEM_LIMIT_BYTES)


def _tile(n, pref):
    t = min(n, pref)
    while n % t:
        t //= 2
    return t


def _nt_dot(a, b):
    return lax.dot_general(a, b, (((1,), (1,)), ((), ())), preferred_element_type=F32)


def _ada_kernel(c_ref, w_ref, b_ref, o_ref):
    c = c_ref[...]
    a = c * jax.nn.sigmoid(c)
    o_ref[0] = jnp.dot(a.astype(BF16), w_ref[0].astype(BF16), preferred_element_type=F32) + b_ref[0]


def _ada(cond, w_ada, b_ada):
    depth, d, n = w_ada.shape
    r = cond.shape[0]
    tn = _tile(n, 1024)
    return pl.pallas_call(
        _ada_kernel,
        out_shape=jax.ShapeDtypeStruct((depth, r, n), F32),
        grid=(depth, n // tn),
        in_specs=[pl.BlockSpec((r, d), lambda l, j: (0, 0)),
                  pl.BlockSpec((1, d, tn), lambda l, j: (l, 0, j)),
                  pl.BlockSpec((1, 1, tn), lambda l, j: (l, 0, j))],
        out_specs=pl.BlockSpec((1, r, tn), lambda l, j: (l, 0, j)),
        compiler_params=_cparams("parallel", "parallel"),
        name="ada",
    )(cond, w_ada, b_ada.reshape(depth, 1, n))


def _modnorm_matmul_kernel(x_ref, mod_ref, g_ref, w_ref, o_ref, h_ref, *ht_ref, shift_idx):
    @pl.when(pl.program_id(1) == 0)
    def _():
        x = x_ref[...]
        y = x * lax.rsqrt(jnp.mean(x * x, axis=-1, keepdims=True) + EPS) * g_ref[...]
        sh = mod_ref[0, shift_idx:shift_idx + 1, :]
        sc = mod_ref[0, shift_idx + 1:shift_idx + 2, :]
        h = y * (1.0 + sc) + sh
        h_ref[...] = h.astype(BF16)
        if ht_ref:
            ht_ref[0][...] = h.T.astype(BF16)

    o_ref[...] = jnp.dot(h_ref[...], w_ref[...], preferred_element_type=F32).astype(o_ref.dtype)


def _modnorm_matmul(x, mod, g, w, seq_row, *, shift_idx, out_dtype, tm, tn, emit_ht=False):
    t, d = x.shape
    n = w.shape[1]
    out_shape = [jax.ShapeDtypeStruct((t, n), out_dtype), jax.ShapeDtypeStruct((t, d), BF16)]
    out_specs = [pl.BlockSpec((tm, tn), lambda i, j: (i, j)), pl.BlockSpec((tm, d), lambda i, j: (i, 0))]
    if emit_ht:
        out_shape.append(jax.ShapeDtypeStruct((d, t), BF16))
        out_specs.append(pl.BlockSpec((d, tm), lambda i, j: (0, i)))
    return pl.pallas_call(
        functools.partial(_modnorm_matmul_kernel, shift_idx=shift_idx),
        out_shape=tuple(out_shape),
        grid=(t // tm, n // tn),
        in_specs=[pl.BlockSpec((tm, d), lambda i, j: (i, 0)),
                  pl.BlockSpec((1, 6, d), lambda i, j: (seq_row(i), 0, 0)),
                  pl.BlockSpec((1, d), lambda i, j: (0, 0)),
                  pl.BlockSpec((d, tn), lambda i, j: (0, j))],
        out_specs=tuple(out_specs),
        compiler_params=_cparams("parallel", "arbitrary"),
        name="modnorm_matmul",
    )(x, mod, g.reshape(1, d), w)


def _log_sigmoid(x):
    return jnp.minimum(x, 0.0) - jnp.log1p(jnp.exp(-jnp.abs(x)))


def _mlstm_chunk(qc, kc, vc, lf_col, ig_col, lf_row, ig_row, C, n, m, fwd, k_scale):
    tc = qc.shape[0]
    r = lax.broadcasted_iota(jnp.int32, (tc, tc), 0)
    c = lax.broadcasted_iota(jnp.int32, (tc, tc), 1)
    mask = (c <= r) if fwd else (c >= r)
    mask_t = (r <= c) if fwd else (r >= c)
    b_col = jnp.sum(jnp.where(mask, lf_row, 0.0), axis=1, keepdims=True)
    b_row = jnp.sum(jnp.where(mask_t, lf_col, 0.0), axis=0, keepdims=True)
    dmat = jnp.where(mask, b_col - b_row + ig_row, -jnp.inf)
    m_t = jnp.maximum(b_col + m, jnp.max(dmat, axis=1, keepdims=True))
    w_in = jnp.exp(dmat - m_t)
    w_prev = jnp.exp(b_col + m - m_t)
    qb, kb, vb = qc.astype(BF16), kc.astype(BF16), vc.astype(BF16)
    s = _nt_dot(qb, kb) * k_scale * w_in
    num = (jnp.dot(s.astype(BF16), vb, preferred_element_type=F32)
           + w_prev * jnp.dot(qb, C.astype(BF16), preferred_element_type=F32))
    den = jnp.sum(s, axis=1, keepdims=True) + w_prev * jnp.sum(qc * n, axis=1, keepdims=True)
    h = num / jnp.maximum(jnp.abs(den), jnp.exp(-m_t))
    b_last = jnp.sum(lf_row, axis=1, keepdims=True)
    lw_col = b_last - b_col + ig_col
    lw_row = b_last - b_row + ig_row
    m_new = jnp.maximum(b_last + m, jnp.max(lw_row, axis=1, keepdims=True))
    decay = jnp.exp(b_last + m - m_new)
    kw = kc * k_scale * jnp.exp(lw_col - m_new)
    C_new = decay * C + jnp.dot(kw.T.astype(BF16), vb, preferred_element_type=F32)
    n_new = decay * n + jnp.sum(kw, axis=0, keepdims=True)
    return h, C_new, n_new, m_new


def _mlstm_kernel(*refs, seq, chunk, has_state, emit_state, k_scale):
    q_ref, k_ref, v_ref, o_ref, g_ref, bias_ref, ng_ref = refs[:7]
    pos = 7
    if has_state:
        c0_ref, n0_ref, m0_ref = refs[pos:pos + 3]
        pos += 3
    y_ref = refs[pos]
    pos += 1
    if emit_state:
        c_out, n_out, m_out = refs[pos:pos + 3]
        pos += 3
    hacc = refs[pos]

    g = g_ref[...] + bias_ref[0]
    gt = g.T
    dk = q_ref.shape[1]
    nchunks = seq // chunk
    for d in range(2):
        fwd = d == 0
        ig_col_all = g[:, 2 * d:2 * d + 1]
        lf_col_all = _log_sigmoid(g[:, 2 * d + 1:2 * d + 2])
        ig_row_all = gt[2 * d:2 * d + 1, :]
        lf_row_all = _log_sigmoid(gt[2 * d + 1:2 * d + 2, :])
        if has_state:
            C = c0_ref[0, d, 0]
            n = n0_ref[0, d, 0]
            m = m0_ref[0, d, 0][:, 0:1]
        else:
            C = jnp.zeros((dk, v_ref.shape[1]), F32)
            n = jnp.zeros((1, dk), F32)
            m = jnp.zeros((1, 1), F32)
        order = range(nchunks) if fwd else range(nchunks - 1, -1, -1)
        for ci in order:
            lo, hi = ci * chunk, (ci + 1) * chunk
            h, C, n, m = _mlstm_chunk(
                q_ref[lo:hi, :], k_ref[lo:hi, :], v_ref[lo:hi, :],
                lf_col_all[lo:hi], ig_col_all[lo:hi], lf_row_all[:, lo:hi], ig_row_all[:, lo:hi],
                C, n, m, fwd, k_scale)
            if fwd:
                hacc[lo:hi, :] = h
            else:
                hacc[lo:hi, :] += h
        if emit_state:
            c_out[0, d, 0] = C
            n_out[0, d, 0] = n
            m_out[0, d, 0] = jnp.broadcast_to(m, (1, LANES))
    hh = hacc[...]
    hn = hh * lax.rsqrt(jnp.mean(hh * hh, axis=-1, keepdims=True) + EPS) * ng_ref[0]
    y_ref[...] = (jax.nn.sigmoid(o_ref[...]) * hn).astype(y_ref.dtype)


def _mlstm(proj, cols, row0, nb, seq, gate_bias, norm_g, state, emit_state):
    heads = ML_HEADS
    dk = (cols["ml_k"] - cols["ml_q"]) // heads
    chunk = min(ML_CHUNK, seq)
    rb = row0 // seq

    def col_spec(off, width):
        return pl.BlockSpec((seq, width), lambda b, h, off=off, width=width: (rb + b, off // width + h))

    in_specs = [col_spec(cols["ml_q"], dk), col_spec(cols["ml_k"], dk), col_spec(cols["ml_v"], dk),
                col_spec(cols["ml_o"], dk), col_spec(cols["ml_g"], LANES),
                pl.BlockSpec((1, 1, LANES), lambda b, h: (h, 0, 0)),
                pl.BlockSpec((1, 1, dk), lambda b, h: (h, 0, 0))]
    args = [proj, proj, proj, proj, proj, gate_bias, norm_g.reshape(heads, 1, dk)]
    has_state = state is not None
    if has_state:
        c0, n0, m0 = state
        in_specs += [pl.BlockSpec((1, 2, 1, dk, dk), lambda b, h: (b, 0, h, 0, 0)),
                     pl.BlockSpec((1, 2, 1, 1, dk), lambda b, h: (b, 0, h, 0, 0)),
                     pl.BlockSpec((1, 2, 1, 1, LANES), lambda b, h: (b, 0, h, 0, 0))]
        args += [c0, n0.reshape(nb, 2, heads, 1, dk),
                 jnp.broadcast_to(m0[..., None, None], (nb, 2, heads, 1, LANES))]
    out_shape = [jax.ShapeDtypeStruct((nb * seq, heads * dk), BF16)]
    out_specs = [pl.BlockSpec((seq, dk), lambda b, h: (b, h))]
    if emit_state:
        out_shape += [jax.ShapeDtypeStruct((nb, 2, heads, dk, dk), F32),
                      jax.ShapeDtypeStruct((nb, 2, heads, 1, dk), F32),
                      jax.ShapeDtypeStruct((nb, 2, heads, 1, LANES), F32)]
        out_specs += [pl.BlockSpec((1, 2, 1, dk, dk), lambda b, h: (b, 0, h, 0, 0)),
                      pl.BlockSpec((1, 2, 1, 1, dk), lambda b, h: (b, 0, h, 0, 0)),
                      pl.BlockSpec((1, 2, 1, 1, LANES), lambda b, h: (b, 0, h, 0, 0))]
    outs = pl.pallas_call(
        functools.partial(_mlstm_kernel, seq=seq, chunk=chunk, has_state=has_state,
                          emit_state=emit_state, k_scale=float(dk) ** -0.5),
        out_shape=tuple(out_shape),
        grid=(nb, heads),
        in_specs=in_specs,
        out_specs=tuple(out_specs),
        scratch_shapes=[pltpu.VMEM((seq, dk), F32)],
        compiler_params=_cparams("parallel", "parallel"),
        name="mlstm",
    )(*args)
    if emit_state:
        y, c_new, n_new, m_new = outs
        return y, (c_new, n_new[:, :, :, 0, :], m_new[:, :, :, 0, 0])
    return outs[0], None


def _dft_mats(seq):
    k = np.arange(seq, dtype=np.int64)
    ang = np.pi * ((k[:, None] * k[None, :]) % (2 * seq)).astype(np.float64) / seq
    cos, sin = np.cos(ang), np.sin(ang)
    alt = np.where(k % 2 == 0, 1.0, -1.0)
    fwd_b = -sin
    fwd_b[0, :] = alt
    fwd = np.concatenate([cos, fwd_b], axis=0)
    inv_a = cos.T / seq
    inv_a[:, 0] = 0.5 / seq
    inv_b = -sin.T / seq
    inv_b[:, 0] = alt * 0.5 / seq
    inv = np.concatenate([inv_a, inv_b], axis=1)
    return fwd.astype(np.float32), inv.astype(np.float32)


def _hy_features(seq):
    pos = np.arange(seq, dtype=np.float64)
    t = pos / (seq - 1)
    bands = np.arange(1, HY_BANDS + 1, dtype=np.float64)
    ang = (2.0 * np.pi / seq) * pos[:, None] * bands[None, :]
    feat = np.concatenate([t[:, None], np.cos(ang), np.sin(ang)], axis=-1)
    feat = np.pad(feat, ((0, 0), (0, LANES - feat.shape[1])))
    return feat.astype(np.float32), t.astype(np.float32)[:, None]


def _hyfilt_kernel(feat_ref, t_ref, w1_ref, b1_ref, w2_ref, b2_ref, w3f_ref, w3b_ref, decf_ref, decb_ref,
                   fwd_ref, p_ref, fi_ref, s_ref):
    hp = lax.Precision.HIGHEST
    h = jnp.sin(jnp.dot(feat_ref[...], w1_ref[...], precision=hp, preferred_element_type=F32) + b1_ref[...])
    h = jnp.sin(jnp.dot(h, w2_ref[...], precision=hp, preferred_element_type=F32) + b2_ref[...])
    t = t_ref[...]
    seq = t.shape[0]
    row = lax.broadcasted_iota(jnp.int32, (seq, 1), 0)
    hf = jnp.dot(h, w3f_ref[...], precision=hp, preferred_element_type=F32) * jnp.exp(-t * decf_ref[0, 0])
    hb = jnp.dot(h, w3b_ref[...], precision=hp, preferred_element_type=F32) * jnp.exp(-t * decb_ref[0, 0])
    hb = jnp.where(row == 0, 0.0, hb)
    nrm = lax.rsqrt(jnp.sum(hf * hf, axis=0, keepdims=True) + jnp.sum(hb * hb, axis=0, keepdims=True) + EPS)
    gp = (hf + hb) * nrm
    gm = (hf - hb) * nrm
    fa = jnp.dot(fwd_ref[0:seq, :], gp, precision=hp, preferred_element_type=F32)
    fb = jnp.dot(fwd_ref[seq:2 * seq, :], gm, precision=hp, preferred_element_type=F32)
    alt = jnp.where(row % 2 == 0, 1.0, -1.0)
    f_nyq = jnp.sum(alt * gp, axis=0, keepdims=True)
    p_ref[0] = fa
    fi_ref[0] = jnp.where(row == 0, 0.0, fb)
    s_ref[0] = jnp.where(row == 0, f_nyq, fa)


def _hyena_filters(seq, w1, b1, w2, b2, w3, decay, fwd_f32):
    feat_np, t_np = _hy_features(seq)
    nfeat, ffn = w1.shape
    width = decay.shape[-1]
    pf = LANES - ffn
    w1p = jnp.pad(w1, ((0, LANES - nfeat), (0, pf)))
    w2p = jnp.pad(w2, ((0, pf), (0, pf)))
    w3p = jnp.pad(w3, ((0, pf), (0, 0)))
    b1p = jnp.pad(b1, (0, pf)).reshape(1, LANES)
    b2p = jnp.pad(b2, (0, pf)).reshape(1, LANES)
    ct = _tile(width, 256)
    nct = width // ct
    dec = decay.reshape(HY_ORDER * 2, 1, width)
    full = lambda shape: pl.BlockSpec(shape, lambda o, j: (0,) * len(shape))
    out_sd = jax.ShapeDtypeStruct((HY_ORDER, seq, width), F32)
    out_spec = pl.BlockSpec((1, seq, ct), lambda o, j: (o, 0, j))
    return pl.pallas_call(
        _hyfilt_kernel,
        out_shape=(out_sd, out_sd, out_sd),
        grid=(HY_ORDER, nct),
        in_specs=[full(feat_np.shape), full(t_np.shape), full(w1p.shape), full(b1p.shape),
                  full(w2p.shape), full(b2p.shape),
                  pl.BlockSpec((LANES, ct), lambda o, j: (0, o * 2 * nct + j)),
                  pl.BlockSpec((LANES, ct), lambda o, j: (0, (o * 2 + 1) * nct + j)),
                  pl.BlockSpec((1, 1, ct), lambda o, j: (o * 2, 0, j)),
                  pl.BlockSpec((1, 1, ct), lambda o, j: (o * 2 + 1, 0, j)),
                  full(fwd_f32.shape)],
        out_specs=(out_spec, out_spec, out_spec),
        compiler_params=_cparams("parallel", "parallel"),
        name="hyena_filters",
    )(jnp.asarray(feat_np), jnp.asarray(t_np), w1p, b1p, w2p, b2p, w3p, w3p, dec, dec, fwd_f32)


def _short_conv(u, w):
    seq = u.shape[0]
    row = lax.broadcasted_iota(jnp.int32, (seq, 1), 0)
    prev = jnp.where(row == 0, 0.0, pltpu.roll(u, 1, 0))
    nxt = jnp.where(row == seq - 1, 0.0, pltpu.roll(u, seq - 1, 0))
    return prev * w[0:1, :] + u * w[1:2, :] + nxt * w[2:3, :]


def _hyena_kernel(uv_ref, u1_ref, u2_ref, cwv_ref, cw1_ref, cw2_ref, fwd_ref, inv_ref,
                  p_ref, fi_ref, s_ref, skip_ref, y_ref):
    seq = uv_ref.shape[0]
    z = _short_conv(uv_ref[...], cwv_ref[...])
    gates = (_short_conv(u1_ref[...], cw1_ref[...]), _short_conv(u2_ref[...], cw2_ref[...]))
    for order in range(HY_ORDER):
        zf = jnp.dot(fwd_ref[...], z.astype(BF16), preferred_element_type=F32)
        a, b = zf[:seq], zf[seq:]
        p, fi, s = p_ref[order], fi_ref[order], s_ref[order]
        ya = a * p - b * fi
        yb = a * fi + b * s
        conv = (jnp.dot(inv_ref[:, :seq], ya.astype(BF16), preferred_element_type=F32)
                + jnp.dot(inv_ref[:, seq:], yb.astype(BF16), preferred_element_type=F32))
        z = gates[order] * (conv + skip_ref[order:order + 1, :] * z)
    y_ref[...] = z.astype(y_ref.dtype)


def _hyena(proj, cols, row0, nb, seq, conv_w, filt, skip, fwd_bf, inv_bf):
    width = skip.shape[-1]
    ct = _tile(width, 256)
    nct = width // ct
    rb = row0 // seq
    off = cols["hy"]
    p_arr, fi_arr, s_arr = filt

    def u_spec(part):
        return pl.BlockSpec((seq, ct), lambda b, j, part=part: (rb + b, off // ct + part * nct + j))

    def cw_spec(part):
        return pl.BlockSpec((3, ct), lambda b, j, part=part: (0, part * nct + j))

    full = lambda shape: pl.BlockSpec(shape, lambda b, j: (0,) * len(shape))
    f_spec = pl.BlockSpec((HY_ORDER, seq, ct), lambda b, j: (0, 0, j))
    return pl.pallas_call(
        _hyena_kernel,
        out_shape=jax.ShapeDtypeStruct((nb * seq, width), BF16),
        grid=(nb, nct),
        in_specs=[u_spec(0), u_spec(1), u_spec(2), cw_spec(0), cw_spec(1), cw_spec(2),
                  full(fwd_bf.shape), full(inv_bf.shape), f_spec, f_spec, f_spec,
                  pl.BlockSpec((HY_ORDER, ct), lambda b, j: (0, j))],
        out_specs=pl.BlockSpec((seq, ct), lambda b, j: (b, j)),
        compiler_params=_cparams("parallel", "parallel"),
        name="hyena",
    )(proj, proj, proj, conv_w, conv_w, conv_w, fwd_bf, inv_bf, p_arr, fi_arr, s_arr, skip)


def _rope_tables(seq, head_dim):
    nfreq = head_dim // 4
    rows = seq // GRID_W
    row = np.repeat(np.arange(rows, dtype=np.float64), GRID_W)
    col = np.tile(np.arange(GRID_W, dtype=np.float64), rows)
    inv = (ROPE_BASE ** (-2.0 * np.arange(nfreq, dtype=np.float32) / (2 * nfreq))).astype(np.float64)
    ar, ac = row[:, None] * inv, col[:, None] * inv
    cos = np.concatenate([np.cos(ar), np.cos(ar), np.cos(ac), np.cos(ac)], axis=1)
    sin = np.concatenate([-np.sin(ar), np.sin(ar), -np.sin(ac), np.sin(ac)], axis=1)
    return cos.astype(np.float32), sin.astype(np.float32)


def _rope(x, cos, sin):
    hd = x.shape[1]
    q = hd // 4
    lane = lax.broadcasted_iota(jnp.int32, x.shape, 1)
    first = (lane % (2 * q)) < q
    partner = jnp.where(first, pltpu.roll(x, hd - q, 1), pltpu.roll(x, q, 1))
    return x * cos + partner * sin


def _attn_kernel(*refs, rope, has_ctx, emit_kv, groups, bq, scale):
    q_ref, k_ref, v_ref, qg_ref, kg_ref = refs[:5]
    pos = 5
    if rope:
        cos_ref, sin_ref = refs[pos:pos + 2]
        pos += 2
    if has_ctx:
        kc_ref, vc_ref = refs[pos:pos + 2]
        pos += 2
    y_ref = refs[pos]
    pos += 1
    if emit_kv:
        ko_ref, vo_ref = refs[pos:pos + 2]

    seq, hd = k_ref.shape
    k = k_ref[...]
    kn = k * lax.rsqrt(jnp.mean(k * k, axis=-1, keepdims=True) + EPS) * kg_ref[...]
    v = v_ref[...]
    if emit_kv:
        ko_ref[...] = kn
        vo_ref[...] = v
    if rope:
        kn = _rope(kn, cos_ref[...], sin_ref[...])
    kb, vb = kn.astype(BF16), v.astype(BF16)
    if has_ctx:
        kcb, vcb = kc_ref[0, 0].astype(BF16), vc_ref[0, 0].astype(BF16)
    for g in range(groups):
        for qi in range(seq // bq):
            lo, hi = qi * bq, (qi + 1) * bq
            q = q_ref[lo:hi, g * hd:(g + 1) * hd]
            qn = q * lax.rsqrt(jnp.mean(q * q, axis=-1, keepdims=True) + EPS) * qg_ref[...]
            if rope:
                qn = _rope(qn, cos_ref[lo:hi, :], sin_ref[lo:hi, :])
            qb = qn.astype(BF16)
            s1 = _nt_dot(qb, kb) * scale
            mx = jnp.max(s1, axis=-1, keepdims=True)
            if has_ctx:
                s2 = _nt_dot(qb, kcb) * scale
                mx = jnp.maximum(mx, jnp.max(s2, axis=-1, keepdims=True))
            p1 = jnp.exp(s1 - mx)
            den = jnp.sum(p1, axis=-1, keepdims=True)
            o = jnp.dot(p1.astype(BF16), vb, preferred_element_type=F32)
            if has_ctx:
                p2 = jnp.exp(s2 - mx)
                den = den + jnp.sum(p2, axis=-1, keepdims=True)
                o = o + jnp.dot(p2.astype(BF16), vcb, preferred_element_type=F32)
            y_ref[lo:hi, g * hd:(g + 1) * hd] = (o / den).astype(y_ref.dtype)


def _attention(proj, cols, row0, nb, seq, q_g, k_g, ctx, layer, rope, emit_kv):
    hd = (cols["at_v"] - cols["at_k"]) // ATT_KV_HEADS
    groups = ATT_HEADS // ATT_KV_HEADS
    gw = groups * hd
    rb = row0 // seq
    in_specs = [pl.BlockSpec((seq, gw), lambda b, h: (rb + b, cols["at_q"] // gw + h)),
                pl.BlockSpec((seq, hd), lambda b, h: (rb + b, cols["at_k"] // hd + h)),
                pl.BlockSpec((seq, hd), lambda b, h: (rb + b, cols["at_v"] // hd + h)),
                pl.BlockSpec((1, hd), lambda b, h: (0, 0)),
                pl.BlockSpec((1, hd), lambda b, h: (0, 0))]
    args = [proj, proj, proj, q_g.reshape(1, hd), k_g.reshape(1, hd)]
    if rope:
        cos_np, sin_np = _rope_tables(seq, hd)
        in_specs += [pl.BlockSpec((seq, hd), lambda b, h: (0, 0))] * 2
        args += [jnp.asarray(cos_np), jnp.asarray(sin_np)]
    has_ctx = ctx is not None
    if has_ctx:
        ck, cv = ctx
        past = ck.shape[2]
        in_specs += [pl.BlockSpec((1, 1, past, hd), lambda b, h: (b, layer, 0, h))] * 2
        args += [ck, cv]
    out_shape = [jax.ShapeDtypeStruct((nb * seq, ATT_HEADS * hd), BF16)]
    out_specs = [pl.BlockSpec((seq, gw), lambda b, h: (b, h))]
    if emit_kv:
        out_shape += [jax.ShapeDtypeStruct((nb * seq, ATT_KV_HEADS * hd), F32)] * 2
        out_specs += [pl.BlockSpec((seq, hd), lambda b, h: (b, h))] * 2
    outs = pl.pallas_call(
        functools.partial(_attn_kernel, rope=rope, has_ctx=has_ctx, emit_kv=emit_kv, groups=groups,
                          bq=min(seq, 256), scale=float(hd) ** -0.5),
        out_shape=tuple(out_shape),
        grid=(nb, ATT_KV_HEADS),
        in_specs=in_specs,
        out_specs=tuple(out_specs),
        compiler_params=_cparams("parallel", "parallel"),
        name="attention",
    )(*args)
    if emit_kv:
        return outs
    return outs[0], None, None


def _branch_kernel(y0_ref, y1_ref, y2_ref, g0_ref, g1_ref, g2_ref, w_ref, o_ref):
    acc = None
    for n, (y_ref, g_ref) in enumerate(((y0_ref, g0_ref), (y1_ref, g1_ref), (y2_ref, g2_ref))):
        p = jnp.dot(y_ref[...], w_ref[n], preferred_element_type=F32)
        term = jax.nn.sigmoid(g_ref[...]) * p
        acc = term if acc is None else acc + term
    o_ref[...] = acc.astype(o_ref.dtype)


def _branch_merge(ys, proj, cols, w_branch, tm, tn):
    t, bw = ys[0].shape
    d = w_branch.shape[2]
    goff = cols["br_g"]
    y_spec = pl.BlockSpec((tm, bw), lambda i, j: (i, 0))

    def g_spec(n):
        return pl.BlockSpec((tm, tn), lambda i, j, n=n: (i, (goff + n * d) // tn + j))

    return pl.pallas_call(
        _branch_kernel,
        out_shape=jax.ShapeDtypeStruct((t, d), BF16),
        grid=(t // tm, d // tn),
        in_specs=[y_spec, y_spec, y_spec, g_spec(0), g_spec(1), g_spec(2),
                  pl.BlockSpec((N_BRANCH, bw, tn), lambda i, j: (0, 0, j))],
        out_specs=pl.BlockSpec((tm, tn), lambda i, j: (i, j)),
        compiler_params=_cparams("parallel", "arbitrary"),
        name="branch_merge",
    )(*ys, proj, proj, proj, w_branch)


def _proj_residual_kernel(a_ref, w_ref, x_ref, mod_ref, o_ref, *, gate_idx):
    y = jnp.dot(a_ref[...], w_ref[...], preferred_element_type=F32)
    o_ref[...] = x_ref[...] + mod_ref[0, gate_idx:gate_idx + 1, :] * y


def _proj_residual(a, w, x, mod, seq_row, *, gate_idx, tm, tn):
    t, k = a.shape
    d = w.shape[1]
    return pl.pallas_call(
        functools.partial(_proj_residual_kernel, gate_idx=gate_idx),
        out_shape=jax.ShapeDtypeStruct((t, d), F32),
        grid=(t // tm, d // tn),
        in_specs=[pl.BlockSpec((tm, k), lambda i, j: (i, 0)),
                  pl.BlockSpec((k, tn), lambda i, j: (0, j)),
                  pl.BlockSpec((tm, tn), lambda i, j: (i, j)),
                  pl.BlockSpec((1, 6, tn), lambda i, j: (seq_row(i), 0, j))],
        out_specs=pl.BlockSpec((tm, tn), lambda i, j: (i, j)),
        compiler_params=_cparams("parallel", "arbitrary"),
        name="proj_residual",
    )(a, w, x, mod)


def _topk_rows(s, k):
    rows = s.shape[0]
    iota = lax.broadcasted_iota(jnp.int32, s.shape, 0).astype(F32)
    rank = jnp.full(s.shape, float(k), F32)
    vals = []
    for r in range(k):
        mx = jnp.max(s, axis=0, keepdims=True)
        idx = jnp.min(jnp.where(s == mx, iota, float(rows)), axis=0, keepdims=True)
        sel = iota == idx
        rank = jnp.where(sel, float(r), rank)
        vals.append(mx)
        s = jnp.where(sel, -jnp.inf, s)
    return jnp.concatenate(vals, axis=0), rank


def _peer_route_kernel(q_ref, keys_ref, u1_ref, cnt_ref, u2_ref, rk2_ref):
    k = PEER_TOPK
    dh = keys_ref.shape[3]
    for h in range(PEER_HEADS):
        s1 = _nt_dot(keys_ref[h, 0], q_ref[:, (2 * h) * dh:(2 * h + 1) * dh])
        s2 = _nt_dot(keys_ref[h, 1], q_ref[:, (2 * h + 1) * dh:(2 * h + 2) * dh])
        sv1, rk1 = _topk_rows(s1, k)
        sv2, rk2 = _topk_rows(s2, k)
        row8 = lax.broadcasted_iota(jnp.int32, (8, s1.shape[1]), 0)
        groups = [sv1[0:1, :] + sv2, sv1[1:2, :] + sv2[0:8, :]]
        for p in range(2, 8):
            groups.append(jnp.where(row8 < k // (p + 1), sv1[p:p + 1, :] + sv2[0:8, :], -jnp.inf))
        groups.append(sv1[8:16, :] + sv2[0:1, :])
        fv, rkc = _topk_rows(jnp.concatenate(groups, axis=0), k)
        z = jnp.sum(jnp.exp(fv - fv[0:1, :]), axis=0, keepdims=True)
        sel = jnp.where(rkc < float(k), 1.0, 0.0)
        cnt_i = jnp.zeros_like(s1)
        starts = [0, 16] + [24 + 8 * (p - 2) for p in range(2, 8)]
        sizes = [16, 8] + [8] * 6
        for p in range(k):
            if p < 8:
                cnt_p = jnp.sum(sel[starts[p]:starts[p] + sizes[p], :], axis=0, keepdims=True)
            else:
                cnt_p = sel[72 + p - 8:72 + p - 7, :]
            cnt_i = cnt_i + jnp.where(rk1 == float(p), cnt_p, 0.0)
        u1_ref[h] = jnp.exp(s1 - sv1[0:1, :]) / z
        cnt_ref[h] = cnt_i
        u2_ref[h] = jnp.exp(s2 - sv2[0:1, :]).astype(u2_ref.dtype)
        rk2_ref[h] = rk2.astype(rk2_ref.dtype)


def _peer_route(q, keys_bf, tr):
    t = q.shape[0]
    heads, _, nkeys, dh = keys_bf.shape
    sd = jax.ShapeDtypeStruct((heads, nkeys, t), F32)
    sd16 = jax.ShapeDtypeStruct((heads, nkeys, t), BF16)
    spec = pl.BlockSpec((heads, nkeys, tr), lambda i: (0, 0, i))
    return pl.pallas_call(
        _peer_route_kernel,
        out_shape=(sd, sd, sd16, sd16),
        grid=(t // tr,),
        in_specs=[pl.BlockSpec((tr, q.shape[1]), lambda i: (i, 0)),
                  pl.BlockSpec(keys_bf.shape, lambda i: (0, 0, 0, 0))],
        out_specs=(spec, spec, spec, spec),
        compiler_params=_cparams("parallel"),
        name="peer_route",
    )(q, keys_bf)


def _peer_dense_kernel(ht_ref, u_ref, vt_ref, u1_ref, cnt_ref, u2_ref, rk2_ref, o_ref, *, sub, group):
    e = pl.program_id(1)

    @pl.when(e == 0)
    def _():
        o_ref[...] = jnp.zeros_like(o_ref)

    te = u_ref.shape[0]
    nkeys, tm = u2_ref.shape[1:]
    ht = ht_ref[...]
    nsub = te // sub
    pre = [jnp.dot(u_ref[0:sub, :], ht, preferred_element_type=F32)] + [None] * (nsub - 1)
    pending = []
    for s in range(nsub):
        if s + 1 < nsub:
            pre[s + 1] = jnp.dot(u_ref[(s + 1) * sub:(s + 2) * sub, :], ht, preferred_element_type=F32)
        act = jax.nn.gelu(pre[s], approximate=True).astype(BF16)
        parts = []
        for r in range(sub // nkeys):
            i = s * (sub // nkeys) + r
            w = None
            for h in range(PEER_HEADS):
                u1 = jnp.broadcast_to(u1_ref[h, i:i + 1, :], (BF16_ROWS, tm)).astype(BF16)[None]
                cn = jnp.broadcast_to(cnt_ref[h, i:i + 1, :], (BF16_ROWS, tm)).astype(BF16)[None]
                rk = rk2_ref[h].reshape(nkeys // BF16_ROWS, BF16_ROWS, tm)
                u2 = u2_ref[h].reshape(nkeys // BF16_ROWS, BF16_ROWS, tm)
                term = u1 * jnp.where(rk < cn, u2, jnp.zeros((), BF16))
                w = term if w is None else w + term
            parts.append(w.reshape(nkeys, tm))
        w = parts[0] if len(parts) == 1 else jnp.concatenate(parts, axis=0)
        pending.append(w * act)
        if len(pending) == group or s + 1 == nsub:
            lo = (s + 1 - len(pending)) * sub
            wa = pending[0] if len(pending) == 1 else jnp.concatenate(pending, axis=0)
            o_ref[...] += jnp.dot(vt_ref[:, lo:(s + 1) * sub], wa, preferred_element_type=F32)
            pending = []


def _peer_dense(h2t, u_bf, vt_bf, route, *, tm, te, sub, group):
    d, t = h2t.shape
    ne = u_bf.shape[0]
    heads, nkeys, _ = route[0].shape
    r_spec = pl.BlockSpec((heads, nkeys, tm), lambda i, e: (0, 0, i))
    k_spec = pl.BlockSpec((heads, te // nkeys, tm), lambda i, e: (0, e, i))
    return pl.pallas_call(
        functools.partial(_peer_dense_kernel, sub=sub, group=group),
        out_shape=jax.ShapeDtypeStruct((d, t), F32),
        grid=(t // tm, ne // te),
        in_specs=[pl.BlockSpec((d, tm), lambda i, e: (0, i)),
                  pl.BlockSpec((te, d), lambda i, e: (e, 0)),
                  pl.BlockSpec((d, te), lambda i, e: (0, e)),
                  k_spec, k_spec, r_spec, r_spec],
        out_specs=pl.BlockSpec((d, tm), lambda i, e: (0, i)),
        compiler_params=_cparams("parallel", "arbitrary"),
        name="peer_dense",
    )(h2t, u_bf, vt_bf, *route)


def _peer_residual_kernel(x_ref, yt_ref, mod_ref, fg_ref, o_ref, *, final_norm):
    y = x_ref[...] + mod_ref[0, 5:6, :] * yt_ref[...].T
    if final_norm:
        y = y * lax.rsqrt(jnp.mean(y * y, axis=-1, keepdims=True) + EPS) * fg_ref[...]
    o_ref[...] = y


def _peer_residual(x, yt, mod, final_g, seq_row, *, final_norm, tm):
    t, d = x.shape
    return pl.pallas_call(
        functools.partial(_peer_residual_kernel, final_norm=final_norm),
        out_shape=jax.ShapeDtypeStruct((t, d), F32),
        grid=(t // tm,),
        in_specs=[pl.BlockSpec((tm, d), lambda i: (i, 0)),
                  pl.BlockSpec((d, tm), lambda i: (0, i)),
                  pl.BlockSpec((1, 6, d), lambda i: (seq_row(i), 0, 0)),
                  pl.BlockSpec((1, d), lambda i: (0, 0))],
        out_specs=pl.BlockSpec((tm, d), lambda i: (i, 0)),
        compiler_params=_cparams("parallel"),
        name="peer_residual",
    )(x, yt, mod, final_g.reshape(1, d))


def _in_layout(bw, kvw, d):
    sizes = (("ml_q", bw), ("ml_k", bw), ("ml_v", bw), ("ml_o", bw), ("hy", 3 * bw),
             ("at_q", bw), ("at_k", kvw), ("at_v", kvw), ("br_g", N_BRANCH * d), ("ml_g", GATE_PAD))
    cols, off = {}, 0
    for name, size in sizes:
        cols[name] = off
        off += size
    return cols, off


def _reorder_w_in(w, bw):
    d = w.shape[0]
    ngate = 4 * ML_HEADS
    gates = w[:, 4 * bw:4 * bw + ngate].reshape(d, 4, ML_HEADS)
    gates = jnp.transpose(gates, (0, 2, 1))
    gates = jnp.pad(gates, ((0, 0), (0, 0), (0, GATE_PAD // ML_HEADS - 4))).reshape(d, GATE_PAD)
    return jnp.concatenate([w[:, :4 * bw], w[:, 4 * bw + ngate:], gates], axis=1).astype(BF16)


def kernel(x_prompt, x_sample, c, cache_k, cache_v, state_C, state_n, state_m, c_ctx, w_ada, b_ada, norm1_g, norm2_g, w_in, ml_gate_bias, ml_norm_g, hy_conv_w, hy_w1, hy_b1, hy_w2, hy_b2, hy_w3, hy_decay, hy_skip, q_norm_g, k_norm_g, w_branch, w_out, peer_wq, peer_keys, peer_u, peer_v, final_g):
    nbp, seq_p, d = x_prompt.shape
    nbs, seq_s, _ = x_sample.shape
    depth = w_ada.shape[0]
    bw = d // 2
    kvw = cache_k.shape[3] * cache_k.shape[4]
    tp, ts = nbp * seq_p, nbs * seq_s
    cols, ncols = _in_layout(bw, kvw, d)
    assert w_in.shape[2] == ncols - GATE_PAD + 4 * ML_HEADS

    tm = _tile(seq_s, TOKEN_TILE)
    tn = TOKEN_TILE
    assert tp % tm == 0 and PEER_TOPK == 16
    tiles_p, tiles_per_seq = tp // tm, seq_s // tm

    def seq_row(i):
        return jnp.where(i < tiles_p, 0, 1 + (i - tiles_p) // tiles_per_seq)

    x = jnp.concatenate([x_prompt.reshape(tp, d), x_sample.reshape(ts, d)], axis=0)
    cond = jnp.concatenate([c_ctx[None, :], c], axis=0)
    nrow = cond.shape[0]
    cond = jnp.pad(cond, ((0, (-nrow) % 8), (0, 0)))
    mod_all = _ada(cond, w_ada, b_ada).reshape(depth, cond.shape[0], 6, d)

    ctx_k = cache_k.reshape(nbs, depth, cache_k.shape[2], kvw)
    ctx_v = cache_v.reshape(nbs, depth, cache_v.shape[2], kvw)
    dft = {}
    for seq in {seq_p, seq_s}:
        fwd_np, inv_np = _dft_mats(seq)
        dft[seq] = (jnp.asarray(fwd_np), jnp.asarray(fwd_np).astype(BF16), jnp.asarray(inv_np).astype(BF16))

    new_k, new_v, new_c, new_n, new_m = [], [], [], [], []
    for l in range(depth):
        mod = mod_all[l]
        w_in_l = _reorder_w_in(w_in[l], bw)
        proj = _modnorm_matmul(x, mod, norm1_g[l], w_in_l, seq_row, shift_idx=0, out_dtype=F32,
                               tm=tm, tn=_tile(ncols, tn))[0]
        gb = ml_gate_bias[l].reshape(4, ML_HEADS).T
        gb = jnp.pad(gb, ((0, 0), (0, LANES - 4))).reshape(ML_HEADS, 1, LANES)

        ys = []
        for (row0, nb, seq, is_ctx) in ((0, nbp, seq_p, True), (tp, nbs, seq_s, False)):
            state = None if is_ctx else (state_C[:, l], state_n[:, l], state_m[:, l])
            y_ml, st = _mlstm(proj, cols, row0, nb, seq, gb, ml_norm_g[l], state, emit_state=is_ctx)
            fwd_f32, fwd_bf, inv_bf = dft[seq]
            filt = _hyena_filters(seq, hy_w1[l], hy_b1[l], hy_w2[l], hy_b2[l], hy_w3[l], hy_decay[l], fwd_f32)
            y_hy = _hyena(proj, cols, row0, nb, seq, hy_conv_w[l], filt, hy_skip[l], fwd_bf, inv_bf)
            y_at, k_l, v_l = _attention(proj, cols, row0, nb, seq, q_norm_g[l], k_norm_g[l],
                                        None if is_ctx else (ctx_k, ctx_v), l, rope=not is_ctx, emit_kv=is_ctx)
            ys.append((y_ml, y_hy, y_at))
            if is_ctx:
                new_k.append(k_l.reshape(nbp, seq_p, ATT_KV_HEADS, kvw // ATT_KV_HEADS))
                new_v.append(v_l.reshape(nbp, seq_p, ATT_KV_HEADS, kvw // ATT_KV_HEADS))
                new_c.append(st[0])
                new_n.append(st[1])
                new_m.append(st[2])
        y_all = [jnp.concatenate([ys[0][n], ys[1][n]], axis=0) for n in range(N_BRANCH)]
        mixed = _branch_merge(y_all, proj, cols, w_branch[l].astype(BF16), tm, tn // 2)
        x = _proj_residual(mixed, w_out[l].astype(BF16), x, mod, seq_row, gate_idx=2, tm=tm, tn=tn)

        q, _, h2t = _modnorm_matmul(x, mod, norm2_g[l], peer_wq[l].astype(BF16), seq_row, shift_idx=3,
                                    out_dtype=BF16, tm=tm, tn=tn, emit_ht=True)
        route = _peer_route(q, peer_keys[l].astype(BF16), _tile(tm, ROUTE_TILE))
        yt = _peer_dense(h2t, peer_u[l].astype(BF16), peer_v[l].T.astype(BF16), route,
                         tm=tm, te=EXPERT_TILE, sub=EXPERT_SUB, group=2)
        x = _peer_residual(x, yt, mod, final_g, seq_row, final_norm=(l == depth - 1), tm=tm)

    y_prompt = x[:tp].reshape(nbp, seq_p, d)
    y_sample = x[tp:].reshape(nbs, seq_s, d)
    return (y_prompt, y_sample, jnp.stack(new_k, axis=1), jnp.stack(new_v, axis=1),
            jnp.stack(new_c, axis=1), jnp.stack(new_n, axis=1), jnp.stack(new_m, axis=1))
```

```python
import functools

import numpy as np
import jax
import jax.numpy as jnp
from jax import lax
from jax.experimental import pallas as pl
from jax.experimental.pallas import tpu as pltpu

F32 = jnp.float32
BF16 = jnp.bfloat16
EPS = 1e-6

GRID_W = 64
ML_HEADS = 4
ATT_HEADS = 8
ATT_KV_HEADS = 2
ROPE_BASE = 10000.0
HY_BANDS = 16
HY_ORDER = 2
N_BRANCH = 3
PEER_HEADS = 8
PEER_TOPK = 16

VMEM_LIMIT_BYTES = 60 * 1024 * 1024
LANES = 128
BF16_ROWS = 16
ML_CHUNK = 256
TOKEN_TILE = 1024
ROUTE_TILE = 512
EXPERT_TILE = 1024
EXPERT_SUB = 256
EXPERT_GROUP = 2
GATE_PAD = 512


def _cparams(*sem):
    return pltpu.CompilerParams(dimension_semantics=sem, vmem_limit_bytes=VMEM_LIMIT_BYTES)


def _tile(n, pref):
    t = min(n, pref)
    while n % t:
        t //= 2
    return t


def _nt_dot(a, b):
    return lax.dot_general(a, b, (((1,), (1,)), ((), ())), preferred_element_type=F32)


def _ada_kernel(c_ref, w_ref, b_ref, o_ref):
    c = c_ref[...]
    a = c * jax.nn.sigmoid(c)
    o_ref[0] = jnp.dot(a.astype(BF16), w_ref[0].astype(BF16), preferred_element_type=F32) + b_ref[0]


def _ada(cond, w_ada, b_ada):
    depth, d, n = w_ada.shape
    r = cond.shape[0]
    tn = _tile(n, 1024)
    return pl.pallas_call(
        _ada_kernel,
        out_shape=jax.ShapeDtypeStruct((depth, r, n), F32),
        grid=(depth, n // tn),
        in_specs=[pl.BlockSpec((r, d), lambda l, j: (0, 0)),
                  pl.BlockSpec((1, d, tn), lambda l, j: (l, 0, j)),
                  pl.BlockSpec((1, 1, tn), lambda l, j: (l, 0, j))],
        out_specs=pl.BlockSpec((1, r, tn), lambda l, j: (l, 0, j)),
        compiler_params=_cparams("parallel", "parallel"),
        name="ada",
    )(cond, w_ada, b_ada.reshape(depth, 1, n))


def _modnorm_matmul_kernel(x_ref, mod_ref, g_ref, w_ref, o_ref, h_ref, *ht_ref, shift_idx):
    @pl.when(pl.program_id(1) == 0)
    def _():
        x = x_ref[...]
        y = x * lax.rsqrt(jnp.mean(x * x, axis=-1, keepdims=True) + EPS) * g_ref[...]
        sh = mod_ref[0, shift_idx:shift_idx + 1, :]
        sc = mod_ref[0, shift_idx + 1:shift_idx + 2, :]
        h = y * (1.0 + sc) + sh
        h_ref[...] = h.astype(BF16)
        if ht_ref:
            ht_ref[0][...] = h.T.astype(BF16)

    o_ref[...] = jnp.dot(h_ref[...], w_ref[...], preferred_element_type=F32).astype(o_ref.dtype)


def _modnorm_matmul(x, mod, g, w, layer, seq_row, *, shift_idx, out_dtype, tm, tn, emit_ht=False):
    t, d = x.shape
    n = w.shape[2]
    out_shape = [jax.ShapeDtypeStruct((t, n), out_dtype), jax.ShapeDtypeStruct((t, d), BF16)]
    out_specs = [pl.BlockSpec((tm, tn), lambda i, j: (i, j)), pl.BlockSpec((tm, d), lambda i, j: (i, 0))]
    if emit_ht:
        out_shape.append(jax.ShapeDtypeStruct((d, t), BF16))
        out_specs.append(pl.BlockSpec((d, tm), lambda i, j: (0, i)))
    return pl.pallas_call(
        functools.partial(_modnorm_matmul_kernel, shift_idx=shift_idx),
        out_shape=tuple(out_shape),
        grid=(t // tm, n // tn),
        in_specs=[pl.BlockSpec((tm, d), lambda i, j: (i, 0)),
                  pl.BlockSpec((1, 6, d), lambda i, j: (seq_row(i), 0, 0)),
                  pl.BlockSpec((1, d), lambda i, j: (0, 0)),
                  pl.BlockSpec((None, d, tn), lambda i, j: (layer, 0, j))],
        out_specs=tuple(out_specs),
        compiler_params=_cparams("parallel", "arbitrary"),
        name="modnorm_matmul",
    )(x, mod, g.reshape(1, d), w)


def _log_sigmoid(x):
    return jnp.minimum(x, 0.0) - jnp.log1p(jnp.exp(-jnp.abs(x)))


def _mlstm_chunk(qc, kc, vc, lf_col, ig_col, lf_row, ig_row, C, n, m, fwd, k_scale):
    tc = qc.shape[0]
    r = lax.broadcasted_iota(jnp.int32, (tc, tc), 0)
    c = lax.broadcasted_iota(jnp.int32, (tc, tc), 1)
    mask = (c <= r) if fwd else (c >= r)
    mask_t = (r <= c) if fwd else (r >= c)
    b_col = jnp.sum(jnp.where(mask, lf_row, 0.0), axis=1, keepdims=True)
    b_row = jnp.sum(jnp.where(mask_t, lf_col, 0.0), axis=0, keepdims=True)
    dmat = jnp.where(mask, b_col - b_row + ig_row, -jnp.inf)
    m_t = jnp.maximum(b_col + m, jnp.max(dmat, axis=1, keepdims=True))
    w_in = jnp.exp(dmat - m_t)
    w_prev = jnp.exp(b_col + m - m_t)
    qb, kb, vb = qc.astype(BF16), kc.astype(BF16), vc.astype(BF16)
    s = _nt_dot(qb, kb) * k_scale * w_in
    num = (jnp.dot(s.astype(BF16), vb, preferred_element_type=F32)
           + w_prev * jnp.dot(qb, C.astype(BF16), preferred_element_type=F32))
    den = jnp.sum(s, axis=1, keepdims=True) + w_prev * jnp.sum(qc * n, axis=1, keepdims=True)
    h = num / jnp.maximum(jnp.abs(den), jnp.exp(-m_t))
    b_last = jnp.sum(lf_row, axis=1, keepdims=True)
    lw_col = b_last - b_col + ig_col
    lw_row = b_last - b_row + ig_row
    m_new = jnp.maximum(b_last + m, jnp.max(lw_row, axis=1, keepdims=True))
    decay = jnp.exp(b_last + m - m_new)
    kw = kc * k_scale * jnp.exp(lw_col - m_new)
    C_new = decay * C + jnp.dot(kw.T.astype(BF16), vb, preferred_element_type=F32)
    n_new = decay * n + jnp.sum(kw, axis=0, keepdims=True)
    return h, C_new, n_new, m_new


def _mlstm_kernel(*refs, seq, chunk, has_state, emit_state, k_scale):
    q_ref, k_ref, v_ref, o_ref, g_ref, bias_ref, ng_ref = refs[:7]
    pos = 7
    if has_state:
        c0_ref, n0_ref, m0_ref = refs[pos:pos + 3]
        pos += 3
    y_ref = refs[pos]
    pos += 1
    if emit_state:
        c_out, n_out, m_out = refs[pos:pos + 3]
        pos += 3
    hacc = refs[pos]

    g = g_ref[...] + bias_ref[0]
    gt = g.T
    dk = q_ref.shape[1]
    nchunks = seq // chunk
    for d in range(2):
        fwd = d == 0
        ig_col_all = g[:, 2 * d:2 * d + 1]
        lf_col_all = _log_sigmoid(g[:, 2 * d + 1:2 * d + 2])
        ig_row_all = gt[2 * d:2 * d + 1, :]
        lf_row_all = _log_sigmoid(gt[2 * d + 1:2 * d + 2, :])
        if has_state:
            C = c0_ref[0, d, 0]
            n = n0_ref[0, d, 0]
            m = m0_ref[0, d, 0][:, 0:1]
        else:
            C = jnp.zeros((dk, v_ref.shape[1]), F32)
            n = jnp.zeros((1, dk), F32)
            m = jnp.zeros((1, 1), F32)
        order = range(nchunks) if fwd else range(nchunks - 1, -1, -1)
        for ci in order:
            lo, hi = ci * chunk, (ci + 1) * chunk
            h, C, n, m = _mlstm_chunk(
                q_ref[lo:hi, :], k_ref[lo:hi, :], v_ref[lo:hi, :],
                lf_col_all[lo:hi], ig_col_all[lo:hi], lf_row_all[:, lo:hi], ig_row_all[:, lo:hi],
                C, n, m, fwd, k_scale)
            if fwd:
                hacc[lo:hi, :] = h
            else:
                hacc[lo:hi, :] += h
        if emit_state:
            c_out[0, d, 0] = C
            n_out[0, d, 0] = n
            m_out[0, d, 0] = jnp.broadcast_to(m, (1, LANES))
    hh = hacc[...]
    hn = hh * lax.rsqrt(jnp.mean(hh * hh, axis=-1, keepdims=True) + EPS) * ng_ref[0]
    y_ref[...] = (jax.nn.sigmoid(o_ref[...]) * hn).astype(y_ref.dtype)


def _mlstm(proj, cols, row0, nb, seq, gate_bias, norm_g, state, emit_state):
    heads = ML_HEADS
    dk = (cols["ml_k"] - cols["ml_q"]) // heads
    chunk = min(ML_CHUNK, seq)
    rb = row0 // seq

    def col_spec(off, width):
        return pl.BlockSpec((seq, width), lambda b, h, off=off, width=width: (rb + b, off // width + h))

    in_specs = [col_spec(cols["ml_q"], dk), col_spec(cols["ml_k"], dk), col_spec(cols["ml_v"], dk),
                col_spec(cols["ml_o"], dk), col_spec(cols["ml_g"], LANES),
                pl.BlockSpec((1, 1, LANES), lambda b, h: (h, 0, 0)),
                pl.BlockSpec((1, 1, dk), lambda b, h: (h, 0, 0))]
    args = [proj, proj, proj, proj, proj, gate_bias, norm_g.reshape(heads, 1, dk)]
    has_state = state is not None
    if has_state:
        c0, n0, m0 = state
        in_specs += [pl.BlockSpec((1, 2, 1, dk, dk), lambda b, h: (b, 0, h, 0, 0)),
                     pl.BlockSpec((1, 2, 1, 1, dk), lambda b, h: (b, 0, h, 0, 0)),
                     pl.BlockSpec((1, 2, 1, 1, LANES), lambda b, h: (b, 0, h, 0, 0))]
        args += [c0, n0.reshape(nb, 2, heads, 1, dk),
                 jnp.broadcast_to(m0[..., None, None], (nb, 2, heads, 1, LANES))]
    out_shape = [jax.ShapeDtypeStruct((nb * seq, heads * dk), BF16)]
    out_specs = [pl.BlockSpec((seq, dk), lambda b, h: (b, h))]
    if emit_state:
        out_shape += [jax.ShapeDtypeStruct((nb, 2, heads, dk, dk), F32),
                      jax.ShapeDtypeStruct((nb, 2, heads, 1, dk), F32),
                      jax.ShapeDtypeStruct((nb, 2, heads, 1, LANES), F32)]
        out_specs += [pl.BlockSpec((1, 2, 1, dk, dk), lambda b, h: (b, 0, h, 0, 0)),
                      pl.BlockSpec((1, 2, 1, 1, dk), lambda b, h: (b, 0, h, 0, 0)),
                      pl.BlockSpec((1, 2, 1, 1, LANES), lambda b, h: (b, 0, h, 0, 0))]
    outs = pl.pallas_call(
        functools.partial(_mlstm_kernel, seq=seq, chunk=chunk, has_state=has_state,
                          emit_state=emit_state, k_scale=float(dk) ** -0.5),
        out_shape=tuple(out_shape),
        grid=(nb, heads),
        in_specs=in_specs,
        out_specs=tuple(out_specs),
        scratch_shapes=[pltpu.VMEM((seq, dk), F32)],
        compiler_params=_cparams("parallel", "parallel"),
        name="mlstm",
    )(*args)
    if emit_state:
        y, c_new, n_new, m_new = outs
        return y, (c_new, n_new[:, :, :, 0, :], m_new[:, :, :, 0, 0])
    return outs[0], None


def _dft_mats(seq):
    k = np.arange(seq, dtype=np.int64)
    ang = np.pi * ((k[:, None] * k[None, :]) % (2 * seq)).astype(np.float64) / seq
    cos, sin = np.cos(ang), np.sin(ang)
    alt = np.where(k % 2 == 0, 1.0, -1.0)
    fwd_b = -sin
    fwd_b[0, :] = alt
    fwd = np.concatenate([cos, fwd_b], axis=0)
    inv_a = cos.T / seq
    inv_a[:, 0] = 0.5 / seq
    inv_b = -sin.T / seq
    inv_b[:, 0] = alt * 0.5 / seq
    inv = np.concatenate([inv_a, inv_b], axis=1)
    return fwd.astype(np.float32), inv.astype(np.float32)


def _hy_features(seq):
    pos = np.arange(seq, dtype=np.float64)
    t = pos / (seq - 1)
    bands = np.arange(1, HY_BANDS + 1, dtype=np.float64)
    ang = (2.0 * np.pi / seq) * pos[:, None] * bands[None, :]
    feat = np.concatenate([t[:, None], np.cos(ang), np.sin(ang)], axis=-1)
    feat = np.pad(feat, ((0, 0), (0, LANES - feat.shape[1])))
    return feat.astype(np.float32), t.astype(np.float32)[:, None]


def _hyfilt_kernel(feat_ref, t_ref, w1_ref, b1_ref, w2_ref, b2_ref, w3f_ref, w3b_ref, decf_ref, decb_ref,
                   fwd_ref, p_ref, fi_ref, s_ref):
    hp = lax.Precision.HIGHEST
    h = jnp.sin(jnp.dot(feat_ref[...], w1_ref[...], precision=hp, preferred_element_type=F32) + b1_ref[...])
    h = jnp.sin(jnp.dot(h, w2_ref[...], precision=hp, preferred_element_type=F32) + b2_ref[...])
    t = t_ref[...]
    seq = t.shape[0]
    row = lax.broadcasted_iota(jnp.int32, (seq, 1), 0)
    hf = jnp.dot(h, w3f_ref[...], precision=hp, preferred_element_type=F32) * jnp.exp(-t * decf_ref[0, 0])
    hb = jnp.dot(h, w3b_ref[...], precision=hp, preferred_element_type=F32) * jnp.exp(-t * decb_ref[0, 0])
    hb = jnp.where(row == 0, 0.0, hb)
    nrm = lax.rsqrt(jnp.sum(hf * hf, axis=0, keepdims=True) + jnp.sum(hb * hb, axis=0, keepdims=True) + EPS)
    gp = (hf + hb) * nrm
    gm = (hf - hb) * nrm
    fa = jnp.dot(fwd_ref[0:seq, :], gp, precision=hp, preferred_element_type=F32)
    fb = jnp.dot(fwd_ref[seq:2 * seq, :], gm, precision=hp, preferred_element_type=F32)
    alt = jnp.where(row % 2 == 0, 1.0, -1.0)
    f_nyq = jnp.sum(alt * gp, axis=0, keepdims=True)
    p_ref[0] = fa
    fi_ref[0] = jnp.where(row == 0, 0.0, fb)
    s_ref[0] = jnp.where(row == 0, f_nyq, fa)


def _hyena_filters(seq, w1, b1, w2, b2, w3, decay, fwd_f32):
    feat_np, t_np = _hy_features(seq)
    nfeat, ffn = w1.shape
    width = decay.shape[-1]
    pf = LANES - ffn
    w1p = jnp.pad(w1, ((0, LANES - nfeat), (0, pf)))
    w2p = jnp.pad(w2, ((0, pf), (0, pf)))
    w3p = jnp.pad(w3, ((0, pf), (0, 0)))
    b1p = jnp.pad(b1, (0, pf)).reshape(1, LANES)
    b2p = jnp.pad(b2, (0, pf)).reshape(1, LANES)
    ct = _tile(width, 256)
    nct = width // ct
    dec = decay.reshape(HY_ORDER * 2, 1, width)
    full = lambda shape: pl.BlockSpec(shape, lambda o, j: (0,) * len(shape))
    out_sd = jax.ShapeDtypeStruct((HY_ORDER, seq, width), F32)
    out_spec = pl.BlockSpec((1, seq, ct), lambda o, j: (o, 0, j))
    return pl.pallas_call(
        _hyfilt_kernel,
        out_shape=(out_sd, out_sd, out_sd),
        grid=(HY_ORDER, nct),
        in_specs=[full(feat_np.shape), full(t_np.shape), full(w1p.shape), full(b1p.shape),
                  full(w2p.shape), full(b2p.shape),
                  pl.BlockSpec((LANES, ct), lambda o, j: (0, o * 2 * nct + j)),
                  pl.BlockSpec((LANES, ct), lambda o, j: (0, (o * 2 + 1) * nct + j)),
                  pl.BlockSpec((1, 1, ct), lambda o, j: (o * 2, 0, j)),
                  pl.BlockSpec((1, 1, ct), lambda o, j: (o * 2 + 1, 0, j)),
                  full(fwd_f32.shape)],
        out_specs=(out_spec, out_spec, out_spec),
        compiler_params=_cparams("parallel", "parallel"),
        name="hyena_filters",
    )(jnp.asarray(feat_np), jnp.asarray(t_np), w1p, b1p, w2p, b2p, w3p, w3p, dec, dec, fwd_f32)


def _short_conv(u, w):
    seq = u.shape[0]
    row = lax.broadcasted_iota(jnp.int32, (seq, 1), 0)
    prev = jnp.where(row == 0, 0.0, pltpu.roll(u, 1, 0))
    nxt = jnp.where(row == seq - 1, 0.0, pltpu.roll(u, seq - 1, 0))
    return prev * w[0:1, :] + u * w[1:2, :] + nxt * w[2:3, :]


def _hyena_kernel(uv_ref, u1_ref, u2_ref, cwv_ref, cw1_ref, cw2_ref, fwd_ref, inv_ref,
                  p_ref, fi_ref, s_ref, skip_ref, y_ref):
    seq = uv_ref.shape[0]
    z = _short_conv(uv_ref[...], cwv_ref[...])
    gates = (_short_conv(u1_ref[...], cw1_ref[...]), _short_conv(u2_ref[...], cw2_ref[...]))
    for order in range(HY_ORDER):
        zf = jnp.dot(fwd_ref[...], z.astype(BF16), preferred_element_type=F32)
        a, b = zf[:seq], zf[seq:]
        p, fi, s = p_ref[order], fi_ref[order], s_ref[order]
        ya = a * p - b * fi
        yb = a * fi + b * s
        conv = (jnp.dot(inv_ref[:, :seq], ya.astype(BF16), preferred_element_type=F32)
                + jnp.dot(inv_ref[:, seq:], yb.astype(BF16), preferred_element_type=F32))
        z = gates[order] * (conv + skip_ref[order:order + 1, :] * z)
    y_ref[...] = z.astype(y_ref.dtype)


def _hyena_channel_tile(seq, width):
    const = 2 * 2 * (2 * seq * seq * 2)
    per_channel = 20 * seq * 4
    fit = (VMEM_LIMIT_BYTES * 2 // 3 - const) // per_channel
    ct = LANES
    while ct * 2 <= min(fit, width):
        ct *= 2
    return _tile(width, ct)


def _hyena(proj, cols, row0, nb, seq, conv_w, filt, skip, fwd_bf, inv_bf):
    width = skip.shape[-1]
    ct = _hyena_channel_tile(seq, width)
    nct = width // ct
    rb = row0 // seq
    off = cols["hy"]
    p_arr, fi_arr, s_arr = filt

    def u_spec(part):
        return pl.BlockSpec((seq, ct), lambda j, b, part=part: (rb + b, off // ct + part * nct + j))

    def cw_spec(part):
        return pl.BlockSpec((3, ct), lambda j, b, part=part: (0, part * nct + j))

    full = lambda shape: pl.BlockSpec(shape, lambda j, b: (0,) * len(shape))
    f_spec = pl.BlockSpec((HY_ORDER, seq, ct), lambda j, b: (0, 0, j))
    return pl.pallas_call(
        _hyena_kernel,
        out_shape=jax.ShapeDtypeStruct((nb * seq, width), BF16),
        grid=(nct, nb),
        in_specs=[u_spec(0), u_spec(1), u_spec(2), cw_spec(0), cw_spec(1), cw_spec(2),
                  full(fwd_bf.shape), full(inv_bf.shape), f_spec, f_spec, f_spec,
                  pl.BlockSpec((HY_ORDER, ct), lambda j, b: (0, j))],
        out_specs=pl.BlockSpec((seq, ct), lambda j, b: (b, j)),
        compiler_params=_cparams("parallel", "parallel"),
        name="hyena",
    )(proj, proj, proj, conv_w, conv_w, conv_w, fwd_bf, inv_bf, p_arr, fi_arr, s_arr, skip)


def _rope_tables(seq, head_dim):
    nfreq = head_dim // 4
    rows = seq // GRID_W
    row = np.repeat(np.arange(rows, dtype=np.float64), GRID_W)
    col = np.tile(np.arange(GRID_W, dtype=np.float64), rows)
    inv = (ROPE_BASE ** (-2.0 * np.arange(nfreq, dtype=np.float32) / (2 * nfreq))).astype(np.float64)
    ar, ac = row[:, None] * inv, col[:, None] * inv
    cos = np.concatenate([np.cos(ar), np.cos(ar), np.cos(ac), np.cos(ac)], axis=1)
    sin = np.concatenate([-np.sin(ar), np.sin(ar), -np.sin(ac), np.sin(ac)], axis=1)
    return cos.astype(np.float32), sin.astype(np.float32)


def _rope(x, cos, sin):
    hd = x.shape[1]
    q = hd // 4
    lane = lax.broadcasted_iota(jnp.int32, x.shape, 1)
    first = (lane % (2 * q)) < q
    partner = jnp.where(first, pltpu.roll(x, hd - q, 1), pltpu.roll(x, q, 1))
    return x * cos + partner * sin


def _attn_kernel(*refs, rope, has_ctx, emit_kv, groups, bq, scale):
    q_ref, k_ref, v_ref, qg_ref, kg_ref = refs[:5]
    pos = 5
    if rope:
        cos_ref, sin_ref = refs[pos:pos + 2]
        pos += 2
    if has_ctx:
        kc_ref, vc_ref = refs[pos:pos + 2]
        pos += 2
    y_ref = refs[pos]
    pos += 1
    if emit_kv:
        ko_ref, vo_ref = refs[pos:pos + 2]

    seq, hd = k_ref.shape
    k = k_ref[...]
    kn = k * lax.rsqrt(jnp.mean(k * k, axis=-1, keepdims=True) + EPS) * kg_ref[...]
    v = v_ref[...]
    if emit_kv:
        ko_ref[...] = kn
        vo_ref[...] = v
    if rope:
        kn = _rope(kn, cos_ref[...], sin_ref[...])
    kb, vb = kn.astype(BF16), v.astype(BF16)
    if has_ctx:
        kcb, vcb = kc_ref[0, 0].astype(BF16), vc_ref[0, 0].astype(BF16)
    for g in range(groups):
        for qi in range(seq // bq):
            lo, hi = qi * bq, (qi + 1) * bq
            q = q_ref[lo:hi, g * hd:(g + 1) * hd]
            qn = q * lax.rsqrt(jnp.mean(q * q, axis=-1, keepdims=True) + EPS) * qg_ref[...]
            if rope:
                qn = _rope(qn, cos_ref[lo:hi, :], sin_ref[lo:hi, :])
            qb = qn.astype(BF16)
            s1 = _nt_dot(qb, kb) * scale
            mx = jnp.max(s1, axis=-1, keepdims=True)
            if has_ctx:
                s2 = _nt_dot(qb, kcb) * scale
                mx = jnp.maximum(mx, jnp.max(s2, axis=-1, keepdims=True))
            p1 = jnp.exp(s1 - mx)
            den = jnp.sum(p1, axis=-1, keepdims=True)
            o = jnp.dot(p1.astype(BF16), vb, preferred_element_type=F32)
            if has_ctx:
                p2 = jnp.exp(s2 - mx)
                den = den + jnp.sum(p2, axis=-1, keepdims=True)
                o = o + jnp.dot(p2.astype(BF16), vcb, preferred_element_type=F32)
            y_ref[lo:hi, g * hd:(g + 1) * hd] = (o / den).astype(y_ref.dtype)


def _attention(proj, cols, row0, nb, seq, q_g, k_g, ctx, layer, rope, emit_kv):
    hd = (cols["at_v"] - cols["at_k"]) // ATT_KV_HEADS
    groups = ATT_HEADS // ATT_KV_HEADS
    gw = groups * hd
    rb = row0 // seq
    in_specs = [pl.BlockSpec((seq, gw), lambda b, h: (rb + b, cols["at_q"] // gw + h)),
                pl.BlockSpec((seq, hd), lambda b, h: (rb + b, cols["at_k"] // hd + h)),
                pl.BlockSpec((seq, hd), lambda b, h: (rb + b, cols["at_v"] // hd + h)),
                pl.BlockSpec((1, hd), lambda b, h: (0, 0)),
                pl.BlockSpec((1, hd), lambda b, h: (0, 0))]
    args = [proj, proj, proj, q_g.reshape(1, hd), k_g.reshape(1, hd)]
    if rope:
        cos_np, sin_np = _rope_tables(seq, hd)
        in_specs += [pl.BlockSpec((seq, hd), lambda b, h: (0, 0))] * 2
        args += [jnp.asarray(cos_np), jnp.asarray(sin_np)]
    has_ctx = ctx is not None
    if has_ctx:
        ck, cv = ctx
        past = ck.shape[2]
        in_specs += [pl.BlockSpec((1, 1, past, hd), lambda b, h: (b, layer, 0, h))] * 2
        args += [ck, cv]
    out_shape = [jax.ShapeDtypeStruct((nb * seq, ATT_HEADS * hd), BF16)]
    out_specs = [pl.BlockSpec((seq, gw), lambda b, h: (b, h))]
    if emit_kv:
        out_shape += [jax.ShapeDtypeStruct((nb * seq, ATT_KV_HEADS * hd), F32)] * 2
        out_specs += [pl.BlockSpec((seq, hd), lambda b, h: (b, h))] * 2
    outs = pl.pallas_call(
        functools.partial(_attn_kernel, rope=rope, has_ctx=has_ctx, emit_kv=emit_kv, groups=groups,
                          bq=min(seq, 256), scale=float(hd) ** -0.5),
        out_shape=tuple(out_shape),
        grid=(nb, ATT_KV_HEADS),
        in_specs=in_specs,
        out_specs=tuple(out_specs),
        compiler_params=_cparams("parallel", "parallel"),
        name="attention",
    )(*args)
    if emit_kv:
        return outs
    return outs[0], None, None


def _branch_kernel(y0_ref, y1_ref, y2_ref, g0_ref, g1_ref, g2_ref, w_ref, o_ref):
    acc = None
    for n, (y_ref, g_ref) in enumerate(((y0_ref, g0_ref), (y1_ref, g1_ref), (y2_ref, g2_ref))):
        p = jnp.dot(y_ref[...], w_ref[n], preferred_element_type=F32)
        term = jax.nn.sigmoid(g_ref[...]) * p
        acc = term if acc is None else acc + term
    o_ref[...] = acc.astype(o_ref.dtype)


def _branch_merge(ys, proj, cols, w_branch, layer, tm, tn):
    t, bw = ys[0].shape
    d = w_branch.shape[3]
    goff = cols["br_g"]
    y_spec = pl.BlockSpec((tm, bw), lambda i, j: (i, 0))

    def g_spec(n):
        return pl.BlockSpec((tm, tn), lambda i, j, n=n: (i, (goff + n * d) // tn + j))

    return pl.pallas_call(
        _branch_kernel,
        out_shape=jax.ShapeDtypeStruct((t, d), BF16),
        grid=(t // tm, d // tn),
        in_specs=[y_spec, y_spec, y_spec, g_spec(0), g_spec(1), g_spec(2),
                  pl.BlockSpec((None, N_BRANCH, bw, tn), lambda i, j: (layer, 0, 0, j))],
        out_specs=pl.BlockSpec((tm, tn), lambda i, j: (i, j)),
        compiler_params=_cparams("parallel", "arbitrary"),
        name="branch_merge",
    )(*ys, proj, proj, proj, w_branch)


def _proj_residual_kernel(a_ref, w_ref, x_ref, mod_ref, o_ref, *, gate_idx):
    y = jnp.dot(a_ref[...], w_ref[...], preferred_element_type=F32)
    o_ref[...] = x_ref[...] + mod_ref[0, gate_idx:gate_idx + 1, :] * y


def _proj_residual(a, w, layer, x, mod, seq_row, *, gate_idx, tm, tn):
    t, k = a.shape
    d = w.shape[2]
    return pl.pallas_call(
        functools.partial(_proj_residual_kernel, gate_idx=gate_idx),
        out_shape=jax.ShapeDtypeStruct((t, d), F32),
        grid=(t // tm, d // tn),
        in_specs=[pl.BlockSpec((tm, k), lambda i, j: (i, 0)),
                  pl.BlockSpec((None, k, tn), lambda i, j: (layer, 0, j)),
                  pl.BlockSpec((tm, tn), lambda i, j: (i, j)),
                  pl.BlockSpec((1, 6, tn), lambda i, j: (seq_row(i), 0, j))],
        out_specs=pl.BlockSpec((tm, tn), lambda i, j: (i, j)),
        compiler_params=_cparams("parallel", "arbitrary"),
        name="proj_residual",
    )(a, w, x, mod)


def _topk_rows(s, k, exact_ties):
    rows = s.shape[0]
    iota = lax.broadcasted_iota(jnp.int32, s.shape, 0).astype(F32)
    rank = jnp.full(s.shape, float(k), F32)
    vals = []
    for r in range(k):
        mx = jnp.max(s, axis=0, keepdims=True)
        sel = s == mx
        if exact_ties:
            sel = iota == jnp.min(jnp.where(sel, iota, float(rows)), axis=0, keepdims=True)
        rank = jnp.where(sel, float(r), rank)
        vals.append(mx)
        s = jnp.where(sel, -jnp.inf, s)
    count = jnp.sum(jnp.where(rank < float(k), 1.0, 0.0), axis=0, keepdims=True)
    return jnp.concatenate(vals, axis=0), rank, count


def _route_head(s1, s2, exact_ties):
    k = PEER_TOPK
    sv1, rk1, c1 = _topk_rows(s1, k, exact_ties)
    sv2, rk2, c2 = _topk_rows(s2, k, exact_ties)
    row8 = lax.broadcasted_iota(jnp.int32, (8, s1.shape[1]), 0)
    groups = [sv1[0:1, :] + sv2, sv1[1:2, :] + sv2[0:8, :]]
    for p in range(2, 8):
        groups.append(jnp.where(row8 < k // (p + 1), sv1[p:p + 1, :] + sv2[0:8, :], -jnp.inf))
    groups.append(sv1[8:16, :] + sv2[0:1, :])
    fv, rkc, c3 = _topk_rows(jnp.concatenate(groups, axis=0), k, exact_ties)
    z = jnp.sum(jnp.exp(fv - fv[0:1, :]), axis=0, keepdims=True)
    sel = jnp.where(rkc < float(k), 1.0, 0.0)
    cnt_i = jnp.zeros_like(s1)
    starts = [0, 16] + [24 + 8 * (p - 2) for p in range(2, 8)]
    sizes = [16, 8] + [8] * 6
    for p in range(k):
        if p < 8:
            cnt_p = jnp.sum(sel[starts[p]:starts[p] + sizes[p], :], axis=0, keepdims=True)
        else:
            cnt_p = sel[72 + p - 8:72 + p - 7, :]
        cnt_i = cnt_i + jnp.where(rk1 == float(p), cnt_p, 0.0)
    u1 = jnp.exp(s1 - sv1[0:1, :]) / z
    u2 = jnp.exp(s2 - sv2[0:1, :])
    return u1, cnt_i, u2, rk2, jnp.max(jnp.maximum(jnp.maximum(c1, c2), c3))


def _peer_route_kernel(q_ref, keys_ref, u1_ref, cnt_ref, u2_ref, rk2_ref):
    dh = keys_ref.shape[3]
    s1 = _nt_dot(keys_ref[0, 0], q_ref[:, 0:dh])
    s2 = _nt_dot(keys_ref[0, 1], q_ref[:, dh:2 * dh])

    def emit(u1, cnt, u2, rk2):
        u1_ref[0] = u1
        cnt_ref[0] = cnt
        u2_ref[0] = u2.astype(u2_ref.dtype)
        rk2_ref[0] = rk2.astype(rk2_ref.dtype)

    *fast, most = _route_head(s1, s2, exact_ties=False)
    emit(*fast)

    @pl.when(most > float(PEER_TOPK))
    def _():
        emit(*_route_head(s1, s2, exact_ties=True)[:4])


def _peer_route(q, keys_bf, tr):
    t = q.shape[0]
    heads, _, nkeys, dh = keys_bf.shape
    sd = jax.ShapeDtypeStruct((heads, nkeys, t), F32)
    sd16 = jax.ShapeDtypeStruct((heads, nkeys, t), BF16)
    spec = pl.BlockSpec((1, nkeys, tr), lambda i, h: (h, 0, i))
    return pl.pallas_call(
        _peer_route_kernel,
        out_shape=(sd, sd, sd16, sd16),
        grid=(t // tr, heads),
        in_specs=[pl.BlockSpec((tr, 2 * dh), lambda i, h: (i, h)),
                  pl.BlockSpec((1, 2, nkeys, dh), lambda i, h: (h, 0, 0, 0))],
        out_specs=(spec, spec, spec, spec),
        compiler_params=_cparams("parallel", "parallel"),
        name="peer_route",
    )(q, keys_bf)


def _peer_dense_kernel(ht_ref, u_ref, vt_ref, u1_ref, cnt_ref, u2_ref, rk2_ref, o_ref, *, sub, group, lane_chunk):
    e = pl.program_id(1)

    @pl.when(e == 0)
    def _():
        o_ref[...] = jnp.zeros_like(o_ref)

    te = u_ref.shape[0]
    nkeys, tm = u2_ref.shape[1:]
    ht = ht_ref[...]
    nsub = te // sub
    pre = [jnp.dot(u_ref[0:sub, :], ht, preferred_element_type=F32)] + [None] * (nsub - 1)
    pending = []
    for s in range(nsub):
        if s + 1 < nsub:
            pre[s + 1] = jnp.dot(u_ref[(s + 1) * sub:(s + 2) * sub, :], ht, preferred_element_type=F32)
        act = jax.nn.gelu(pre[s], approximate=True).astype(BF16)
        rows = sub // nkeys
        ktiles = nkeys // BF16_ROWS
        chunks = []
        for c in range(tm // lane_chunk):
            lanes = slice(c * lane_chunk, (c + 1) * lane_chunk)
            w = [None] * rows
            for h in range(PEER_HEADS):
                rk = rk2_ref[h, :, lanes].reshape(ktiles, BF16_ROWS, lane_chunk)
                u2 = u2_ref[h, :, lanes].reshape(ktiles, BF16_ROWS, lane_chunk)
                for r in range(rows):
                    i = s * rows + r
                    u1 = jnp.broadcast_to(u1_ref[h, i:i + 1, lanes], (BF16_ROWS, lane_chunk)).astype(BF16)[None]
                    cn = jnp.broadcast_to(cnt_ref[h, i:i + 1, lanes], (BF16_ROWS, lane_chunk)).astype(BF16)[None]
                    term = u1 * jnp.where(rk < cn, u2, jnp.zeros((), BF16))
                    w[r] = term if w[r] is None else w[r] + term
            w = [wr.reshape(nkeys, lane_chunk) for wr in w]
            chunks.append(w[0] if rows == 1 else jnp.concatenate(w, axis=0))
        w = chunks[0] if len(chunks) == 1 else jnp.concatenate(chunks, axis=1)
        pending.append(w * act)
        if len(pending) == group or s + 1 == nsub:
            lo = (s + 1 - len(pending)) * sub
            wa = pending[0] if len(pending) == 1 else jnp.concatenate(pending, axis=0)
            o_ref[...] += jnp.dot(vt_ref[:, lo:(s + 1) * sub], wa, preferred_element_type=F32)
            pending = []


def _peer_dense(h2t, u_bf, vt_bf, layer, route, *, tm, te, sub, group, lane_chunk):
    d, t = h2t.shape
    ne = u_bf.shape[1]
    heads, nkeys, _ = route[0].shape
    r_spec = pl.BlockSpec((heads, nkeys, tm), lambda i, e: (0, 0, i))
    k_spec = pl.BlockSpec((heads, te // nkeys, tm), lambda i, e: (0, e, i))
    return pl.pallas_call(
        functools.partial(_peer_dense_kernel, sub=sub, group=group, lane_chunk=min(lane_chunk, tm)),
        out_shape=jax.ShapeDtypeStruct((d, t), F32),
        grid=(t // tm, ne // te),
        in_specs=[pl.BlockSpec((d, tm), lambda i, e: (0, i)),
                  pl.BlockSpec((None, te, d), lambda i, e: (layer, e, 0)),
                  pl.BlockSpec((None, d, te), lambda i, e: (layer, 0, e)),
                  k_spec, k_spec, r_spec, r_spec],
        out_specs=pl.BlockSpec((d, tm), lambda i, e: (0, i)),
        compiler_params=_cparams("parallel", "arbitrary"),
        name="peer_dense",
    )(h2t, u_bf, vt_bf, *route)


def _peer_residual_kernel(x_ref, yt_ref, mod_ref, fg_ref, o_ref, *, final_norm):
    y = x_ref[...] + mod_ref[0, 5:6, :] * yt_ref[...].T
    if final_norm:
        y = y * lax.rsqrt(jnp.mean(y * y, axis=-1, keepdims=True) + EPS) * fg_ref[...]
    o_ref[...] = y


def _peer_residual(x, yt, mod, final_g, seq_row, *, final_norm, tm, first_tile=0, ntiles=None):
    t, d = x.shape
    ntiles = t // tm if ntiles is None else ntiles
    return pl.pallas_call(
        functools.partial(_peer_residual_kernel, final_norm=final_norm),
        out_shape=jax.ShapeDtypeStruct((ntiles * tm, d), F32),
        grid=(ntiles,),
        in_specs=[pl.BlockSpec((tm, d), lambda i: (first_tile + i, 0)),
                  pl.BlockSpec((d, tm), lambda i: (0, first_tile + i)),
                  pl.BlockSpec((1, 6, d), lambda i: (seq_row(first_tile + i), 0, 0)),
                  pl.BlockSpec((1, d), lambda i: (0, 0))],
        out_specs=pl.BlockSpec((tm, d), lambda i: (i, 0)),
        compiler_params=_cparams("parallel"),
        name="peer_residual",
    )(x, yt, mod, final_g.reshape(1, d))


def _in_layout(bw, kvw, d):
    sizes = (("ml_q", bw), ("ml_k", bw), ("ml_v", bw), ("ml_o", bw), ("hy", 3 * bw),
             ("at_q", bw), ("at_k", kvw), ("at_v", kvw), ("br_g", N_BRANCH * d), ("ml_g", GATE_PAD))
    cols, off = {}, 0
    for name, size in sizes:
        cols[name] = off
        off += size
    return cols, off


def _reorder_w_in(w, bw):
    d = w.shape[0]
    ngate = 4 * ML_HEADS
    gates = w[:, 4 * bw:4 * bw + ngate].reshape(d, 4, ML_HEADS)
    gates = jnp.transpose(gates, (0, 2, 1))
    gates = jnp.pad(gates, ((0, 0), (0, 0), (0, GATE_PAD // ML_HEADS - 4))).reshape(d, GATE_PAD)
    return jnp.concatenate([w[:, :4 * bw], w[:, 4 * bw + ngate:], gates], axis=1).astype(BF16)


def kernel(x_prompt, x_sample, c, cache_k, cache_v, state_C, state_n, state_m, c_ctx, w_ada, b_ada, norm1_g, norm2_g, w_in, ml_gate_bias, ml_norm_g, hy_conv_w, hy_w1, hy_b1, hy_w2, hy_b2, hy_w3, hy_decay, hy_skip, q_norm_g, k_norm_g, w_branch, w_out, peer_wq, peer_keys, peer_u, peer_v, final_g):
    nbp, seq_p, d = x_prompt.shape
    nbs, seq_s, _ = x_sample.shape
    depth = w_ada.shape[0]
    bw = d // 2
    kvw = cache_k.shape[3] * cache_k.shape[4]
    tp, ts = nbp * seq_p, nbs * seq_s
    cols, ncols = _in_layout(bw, kvw, d)
    assert w_in.shape[2] == ncols - GATE_PAD + 4 * ML_HEADS

    tm = _tile(seq_s, TOKEN_TILE)
    tn = TOKEN_TILE
    assert tp % tm == 0 and PEER_TOPK == 16
    tiles_p, tiles_per_seq = tp // tm, seq_s // tm

    def seq_row(i):
        return jnp.where(i < tiles_p, 0, 1 + (i - tiles_p) // tiles_per_seq)

    x = jnp.concatenate([x_prompt.reshape(tp, d), x_sample.reshape(ts, d)], axis=0)
    cond = jnp.concatenate([c_ctx[None, :], c], axis=0)
    nrow = cond.shape[0]
    cond = jnp.pad(cond, ((0, (-nrow) % 8), (0, 0)))
    mod_all = _ada(cond, w_ada, b_ada).reshape(depth, cond.shape[0], 6, d)

    ctx_k = cache_k.reshape(nbs, depth, cache_k.shape[2], kvw)
    ctx_v = cache_v.reshape(nbs, depth, cache_v.shape[2], kvw)
    dft = {}
    for seq in {seq_p, seq_s}:
        fwd_np, inv_np = _dft_mats(seq)
        dft[seq] = (jnp.asarray(fwd_np), jnp.asarray(fwd_np).astype(BF16), jnp.asarray(inv_np).astype(BF16))

    w_in_bf = jax.vmap(lambda w: _reorder_w_in(w, bw))(w_in)
    w_branch_bf, w_out_bf, wq_bf = w_branch.astype(BF16), w_out.astype(BF16), peer_wq.astype(BF16)
    u_bf = peer_u.astype(BF16)
    vt_bf = jnp.swapaxes(peer_v, 1, 2).astype(BF16)
    keys_bf = peer_keys.astype(BF16)

    new_k, new_v, new_c, new_n, new_m = [], [], [], [], []
    for l in range(depth):
        mod = mod_all[l]
        proj = _modnorm_matmul(x, mod, norm1_g[l], w_in_bf, l, seq_row, shift_idx=0, out_dtype=F32,
                               tm=tm, tn=_tile(ncols, tn))[0]
        gb = ml_gate_bias[l].reshape(4, ML_HEADS).T
        gb = jnp.pad(gb, ((0, 0), (0, LANES - 4))).reshape(ML_HEADS, 1, LANES)

        ys = []
        for (row0, nb, seq, is_ctx) in ((0, nbp, seq_p, True), (tp, nbs, seq_s, False)):
            state = None if is_ctx else (state_C[:, l], state_n[:, l], state_m[:, l])
            y_ml, st = _mlstm(proj, cols, row0, nb, seq, gb, ml_norm_g[l], state, emit_state=is_ctx)
            fwd_f32, fwd_bf, inv_bf = dft[seq]
            filt = _hyena_filters(seq, hy_w1[l], hy_b1[l], hy_w2[l], hy_b2[l], hy_w3[l], hy_decay[l], fwd_f32)
            y_hy = _hyena(proj, cols, row0, nb, seq, hy_conv_w[l], filt, hy_skip[l], fwd_bf, inv_bf)
            y_at, k_l, v_l = _attention(proj, cols, row0, nb, seq, q_norm_g[l], k_norm_g[l],
                                        None if is_ctx else (ctx_k, ctx_v), l, rope=not is_ctx, emit_kv=is_ctx)
            ys.append((y_ml, y_hy, y_at))
            if is_ctx:
                new_k.append(k_l.reshape(nbp, seq_p, ATT_KV_HEADS, kvw // ATT_KV_HEADS))
                new_v.append(v_l.reshape(nbp, seq_p, ATT_KV_HEADS, kvw // ATT_KV_HEADS))
                new_c.append(st[0])
                new_n.append(st[1])
                new_m.append(st[2])
        y_all = [jnp.concatenate([ys[0][n], ys[1][n]], axis=0) for n in range(N_BRANCH)]
        mixed = _branch_merge(y_all, proj, cols, w_branch_bf, l, tm, tn // 2)
        x = _proj_residual(mixed, w_out_bf, l, x, mod, seq_row, gate_idx=2, tm=tm, tn=tn)

        q, _, h2t = _modnorm_matmul(x, mod, norm2_g[l], wq_bf, l, seq_row, shift_idx=3,
                                    out_dtype=BF16, tm=tm, tn=tn, emit_ht=True)
        route = _peer_route(q, keys_bf[l], _tile(tm, ROUTE_TILE))
        yt = _peer_dense(h2t, u_bf, vt_bf, l, route, tm=tm, te=EXPERT_TILE, sub=EXPERT_SUB,
                         group=EXPERT_GROUP, lane_chunk=tm)
        if l < depth - 1:
            x = _peer_residual(x, yt, mod, final_g, seq_row, final_norm=False, tm=tm)

    y_prompt = _peer_residual(x, yt, mod, final_g, seq_row, final_norm=True, tm=tm,
                              first_tile=0, ntiles=tiles_p).reshape(nbp, seq_p, d)
    y_sample = _peer_residual(x, yt, mod, final_g, seq_row, final_norm=True, tm=tm,
                              first_tile=tiles_p, ntiles=ts // tm).reshape(nbs, seq_s, d)
    return (y_prompt, y_sample, jnp.stack(new_k, axis=1), jnp.stack(new_v, axis=1),
            jnp.stack(new_c, axis=1), jnp.stack(new_n, axis=1), jnp.stack(new_m, axis=1))
```

```python
import functools

import numpy as np
import jax
import jax.numpy as jnp
from jax import lax
from jax.experimental import pallas as pl
from jax.experimental.pallas import tpu as pltpu

F32 = jnp.float32
BF16 = jnp.bfloat16
EPS = 1e-6

GRID_W = 64
ML_HEADS = 4
ATT_HEADS = 8
ATT_KV_HEADS = 2
ROPE_BASE = 10000.0
HY_BANDS = 16
HY_ORDER = 2
N_BRANCH = 3
PEER_HEADS = 8
PEER_TOPK = 16

VMEM_LIMIT_BYTES = 60 * 1024 * 1024
LANES = 128
BF16_ROWS = 16
ML_CHUNK = 256
TOKEN_TILE = 1024
ROUTE_TILE = 512
EXPERT_TILE = 1024
EXPERT_SUB = 256
EXPERT_GROUP = 4
GATE_PAD = 512


def _cparams(*sem):
    return pltpu.CompilerParams(dimension_semantics=sem, vmem_limit_bytes=VMEM_LIMIT_BYTES)


def _tile(n, pref):
    t = min(n, pref)
    while n % t:
        t //= 2
    return t


def _nt_dot(a, b):
    return lax.dot_general(a, b, (((1,), (1,)), ((), ())), preferred_element_type=F32)


def _ada_kernel(c_ref, w_ref, b_ref, o_ref):
    c = c_ref[...]
    a = c * jax.nn.sigmoid(c)
    o_ref[0] = jnp.dot(a.astype(BF16), w_ref[0].astype(BF16), preferred_element_type=F32) + b_ref[0]


def _ada(cond, w_ada, b_ada):
    depth, d, n = w_ada.shape
    r = cond.shape[0]
    tn = _tile(n, 1024)
    return pl.pallas_call(
        _ada_kernel,
        out_shape=jax.ShapeDtypeStruct((depth, r, n), F32),
        grid=(depth, n // tn),
        in_specs=[pl.BlockSpec((r, d), lambda l, j: (0, 0)),
                  pl.BlockSpec((1, d, tn), lambda l, j: (l, 0, j)),
                  pl.BlockSpec((1, 1, tn), lambda l, j: (l, 0, j))],
        out_specs=pl.BlockSpec((1, r, tn), lambda l, j: (l, 0, j)),
        compiler_params=_cparams("parallel", "parallel"),
        name="ada",
    )(cond, w_ada, b_ada.reshape(depth, 1, n))


def _modnorm_matmul_kernel(x_ref, mod_ref, g_ref, w_ref, o_ref, h_ref, *ht_ref, shift_idx):
    @pl.when(pl.program_id(1) == 0)
    def _():
        x = x_ref[...]
        y = x * lax.rsqrt(jnp.mean(x * x, axis=-1, keepdims=True) + EPS) * g_ref[...]
        sh = mod_ref[0, shift_idx:shift_idx + 1, :]
        sc = mod_ref[0, shift_idx + 1:shift_idx + 2, :]
        h = y * (1.0 + sc) + sh
        h_ref[...] = h.astype(BF16)
        if ht_ref:
            ht_ref[0][...] = h.T.astype(BF16)

    o_ref[...] = jnp.dot(h_ref[...], w_ref[...], preferred_element_type=F32).astype(o_ref.dtype)


def _modnorm_matmul(x, mod, g, w, layer, seq_row, *, shift_idx, out_dtype, tm, tn, emit_ht=False):
    t, d = x.shape
    n = w.shape[2]
    out_shape = [jax.ShapeDtypeStruct((t, n), out_dtype), jax.ShapeDtypeStruct((t, d), BF16)]
    out_specs = [pl.BlockSpec((tm, tn), lambda i, j: (i, j)), pl.BlockSpec((tm, d), lambda i, j: (i, 0))]
    if emit_ht:
        out_shape.append(jax.ShapeDtypeStruct((d, t), BF16))
        out_specs.append(pl.BlockSpec((d, tm), lambda i, j: (0, i)))
    return pl.pallas_call(
        functools.partial(_modnorm_matmul_kernel, shift_idx=shift_idx),
        out_shape=tuple(out_shape),
        grid=(t // tm, n // tn),
        in_specs=[pl.BlockSpec((tm, d), lambda i, j: (i, 0)),
                  pl.BlockSpec((1, 6, d), lambda i, j: (seq_row(i), 0, 0)),
                  pl.BlockSpec((1, d), lambda i, j: (0, 0)),
                  pl.BlockSpec((None, d, tn), lambda i, j: (layer, 0, j))],
        out_specs=tuple(out_specs),
        compiler_params=_cparams("parallel", "arbitrary"),
        name="modnorm_matmul",
    )(x, mod, g.reshape(1, d), w)


def _log_sigmoid(x):
    return jnp.minimum(x, 0.0) - jnp.log1p(jnp.exp(-jnp.abs(x)))


def _mlstm_chunk(qc, kc, vc, lf_col, ig_col, lf_row, ig_row, C, n, m, fwd, k_scale):
    tc = qc.shape[0]
    r = lax.broadcasted_iota(jnp.int32, (tc, tc), 0)
    c = lax.broadcasted_iota(jnp.int32, (tc, tc), 1)
    mask = (c <= r) if fwd else (c >= r)
    mask_t = (r <= c) if fwd else (r >= c)
    b_col = jnp.sum(jnp.where(mask, lf_row, 0.0), axis=1, keepdims=True)
    b_row = jnp.sum(jnp.where(mask_t, lf_col, 0.0), axis=0, keepdims=True)
    dmat = jnp.where(mask, b_col - b_row + ig_row, -jnp.inf)
    m_t = jnp.maximum(b_col + m, jnp.max(dmat, axis=1, keepdims=True))
    w_in = jnp.exp(dmat - m_t)
    w_prev = jnp.exp(b_col + m - m_t)
    qb, kb, vb = qc.astype(BF16), kc.astype(BF16), vc.astype(BF16)
    s = _nt_dot(qb, kb) * k_scale * w_in
    num = (jnp.dot(s.astype(BF16), vb, preferred_element_type=F32)
           + w_prev * jnp.dot(qb, C.astype(BF16), preferred_element_type=F32))
    den = jnp.sum(s, axis=1, keepdims=True) + w_prev * jnp.sum(qc * n, axis=1, keepdims=True)
    h = num / jnp.maximum(jnp.abs(den), jnp.exp(-m_t))
    b_last = jnp.sum(lf_row, axis=1, keepdims=True)
    lw_col = b_last - b_col + ig_col
    lw_row = b_last - b_row + ig_row
    m_new = jnp.maximum(b_last + m, jnp.max(lw_row, axis=1, keepdims=True))
    decay = jnp.exp(b_last + m - m_new)
    kw = kc * k_scale * jnp.exp(lw_col - m_new)
    C_new = decay * C + jnp.dot(kw.T.astype(BF16), vb, preferred_element_type=F32)
    n_new = decay * n + jnp.sum(kw, axis=0, keepdims=True)
    return h, C_new, n_new, m_new


def _shared_rows_out(total_rows, width, prev, in_specs, args):
    aliases = {}
    if prev is not None:
        in_specs.append(pl.BlockSpec(memory_space=pl.ANY))
        args.append(prev)
        aliases = {len(args) - 1: 0}
    return jax.ShapeDtypeStruct((total_rows, width), BF16), aliases


def _mlstm_kernel(*refs, seq, chunk, has_state, has_prev, emit_state, k_scale):
    q_ref, k_ref, v_ref, o_ref, g_ref, bias_ref, ng_ref = refs[:7]
    pos = 7
    if has_state:
        c0_ref, n0_ref, m0_ref = refs[pos:pos + 3]
        pos += 3
    pos += has_prev
    y_ref = refs[pos]
    pos += 1
    if emit_state:
        c_out, n_out, m_out = refs[pos:pos + 3]
        pos += 3
    hacc = refs[pos]

    g = g_ref[...] + bias_ref[0]
    gt = g.T
    dk = q_ref.shape[1]
    nchunks = seq // chunk
    for d in range(2):
        fwd = d == 0
        ig_col_all = g[:, 2 * d:2 * d + 1]
        lf_col_all = _log_sigmoid(g[:, 2 * d + 1:2 * d + 2])
        ig_row_all = gt[2 * d:2 * d + 1, :]
        lf_row_all = _log_sigmoid(gt[2 * d + 1:2 * d + 2, :])
        if has_state:
            C = c0_ref[0, d, 0]
            n = n0_ref[0, d, 0]
            m = m0_ref[0, d, 0][:, 0:1]
        else:
            C = jnp.zeros((dk, v_ref.shape[1]), F32)
            n = jnp.zeros((1, dk), F32)
            m = jnp.zeros((1, 1), F32)
        order = range(nchunks) if fwd else range(nchunks - 1, -1, -1)
        for ci in order:
            lo, hi = ci * chunk, (ci + 1) * chunk
            h, C, n, m = _mlstm_chunk(
                q_ref[lo:hi, :], k_ref[lo:hi, :], v_ref[lo:hi, :],
                lf_col_all[lo:hi], ig_col_all[lo:hi], lf_row_all[:, lo:hi], ig_row_all[:, lo:hi],
                C, n, m, fwd, k_scale)
            if fwd:
                hacc[lo:hi, :] = h
            else:
                hacc[lo:hi, :] += h
        if emit_state:
            c_out[0, d, 0] = C
            n_out[0, d, 0] = n
            m_out[0, d, 0] = jnp.broadcast_to(m, (1, LANES))
    hh = hacc[...]
    hn = hh * lax.rsqrt(jnp.mean(hh * hh, axis=-1, keepdims=True) + EPS) * ng_ref[0]
    y_ref[...] = (jax.nn.sigmoid(o_ref[...]) * hn).astype(y_ref.dtype)


def _mlstm(proj, cols, row0, nb, seq, gate_bias, norm_g, state, emit_state, y_prev=None):
    heads = ML_HEADS
    dk = (cols["ml_k"] - cols["ml_q"]) // heads
    chunk = min(ML_CHUNK, seq)
    rb = row0 // seq

    def col_spec(off, width):
        return pl.BlockSpec((seq, width), lambda b, h, off=off, width=width: (rb + b, off // width + h))

    in_specs = [col_spec(cols["ml_q"], dk), col_spec(cols["ml_k"], dk), col_spec(cols["ml_v"], dk),
                col_spec(cols["ml_o"], dk), col_spec(cols["ml_g"], LANES),
                pl.BlockSpec((1, 1, LANES), lambda b, h: (h, 0, 0)),
                pl.BlockSpec((1, 1, dk), lambda b, h: (h, 0, 0))]
    args = [proj, proj, proj, proj, proj, gate_bias, norm_g.reshape(heads, 1, dk)]
    has_state = state is not None
    if has_state:
        c0, n0, m0 = state
        in_specs += [pl.BlockSpec((1, 2, 1, dk, dk), lambda b, h: (b, 0, h, 0, 0)),
                     pl.BlockSpec((1, 2, 1, 1, dk), lambda b, h: (b, 0, h, 0, 0)),
                     pl.BlockSpec((1, 2, 1, 1, LANES), lambda b, h: (b, 0, h, 0, 0))]
        args += [c0, n0.reshape(nb, 2, heads, 1, dk),
                 jnp.broadcast_to(m0[..., None, None], (nb, 2, heads, 1, LANES))]
    y_shape, aliases = _shared_rows_out(proj.shape[0], heads * dk, y_prev, in_specs, args)
    out_shape = [y_shape]
    out_specs = [pl.BlockSpec((seq, dk), lambda b, h: (rb + b, h))]
    if emit_state:
        out_shape += [jax.ShapeDtypeStruct((nb, 2, heads, dk, dk), F32),
                      jax.ShapeDtypeStruct((nb, 2, heads, 1, dk), F32),
                      jax.ShapeDtypeStruct((nb, 2, heads, 1, LANES), F32)]
        out_specs += [pl.BlockSpec((1, 2, 1, dk, dk), lambda b, h: (b, 0, h, 0, 0)),
                      pl.BlockSpec((1, 2, 1, 1, dk), lambda b, h: (b, 0, h, 0, 0)),
                      pl.BlockSpec((1, 2, 1, 1, LANES), lambda b, h: (b, 0, h, 0, 0))]
    outs = pl.pallas_call(
        functools.partial(_mlstm_kernel, seq=seq, chunk=chunk, has_state=has_state,
                          has_prev=y_prev is not None, emit_state=emit_state, k_scale=float(dk) ** -0.5),
        out_shape=tuple(out_shape),
        grid=(nb, heads),
        in_specs=in_specs,
        out_specs=tuple(out_specs),
        input_output_aliases=aliases,
        scratch_shapes=[pltpu.VMEM((seq, dk), F32)],
        compiler_params=_cparams("parallel", "parallel"),
        name="mlstm",
    )(*args)
    if emit_state:
        y, c_new, n_new, m_new = outs
        return y, (c_new, n_new[:, :, :, 0, :], m_new[:, :, :, 0, 0])
    return outs[0], None


def _dft_mats(seq):
    k = np.arange(seq, dtype=np.int64)
    ang = np.pi * ((k[:, None] * k[None, :]) % (2 * seq)).astype(np.float64) / seq
    cos, sin = np.cos(ang), np.sin(ang)
    alt = np.where(k % 2 == 0, 1.0, -1.0)
    fwd_b = -sin
    fwd_b[0, :] = alt
    fwd = np.concatenate([cos, fwd_b], axis=0)
    inv_a = cos.T / seq
    inv_a[:, 0] = 0.5 / seq
    inv_b = -sin.T / seq
    inv_b[:, 0] = alt * 0.5 / seq
    inv = np.concatenate([inv_a, inv_b], axis=1)
    return fwd.astype(np.float32), inv.astype(np.float32)


def _hy_features(seq):
    pos = np.arange(seq, dtype=np.float64)
    t = pos / (seq - 1)
    bands = np.arange(1, HY_BANDS + 1, dtype=np.float64)
    ang = (2.0 * np.pi / seq) * pos[:, None] * bands[None, :]
    feat = np.concatenate([t[:, None], np.cos(ang), np.sin(ang)], axis=-1)
    feat = np.pad(feat, ((0, 0), (0, LANES - feat.shape[1])))
    return feat.astype(np.float32), t.astype(np.float32)[:, None]


def _hyfilt_kernel(feat_ref, t_ref, w1_ref, b1_ref, w2_ref, b2_ref, w3f_ref, w3b_ref, decf_ref, decb_ref,
                   fwd_ref, p_ref, fi_ref, s_ref):
    hp = lax.Precision.HIGHEST
    h = jnp.sin(jnp.dot(feat_ref[...], w1_ref[...], precision=hp, preferred_element_type=F32) + b1_ref[...])
    h = jnp.sin(jnp.dot(h, w2_ref[...], precision=hp, preferred_element_type=F32) + b2_ref[...])
    t = t_ref[...]
    seq = t.shape[0]
    row = lax.broadcasted_iota(jnp.int32, (seq, 1), 0)
    hf = jnp.dot(h, w3f_ref[...], precision=hp, preferred_element_type=F32) * jnp.exp(-t * decf_ref[0, 0])
    hb = jnp.dot(h, w3b_ref[...], precision=hp, preferred_element_type=F32) * jnp.exp(-t * decb_ref[0, 0])
    hb = jnp.where(row == 0, 0.0, hb)
    nrm = lax.rsqrt(jnp.sum(hf * hf, axis=0, keepdims=True) + jnp.sum(hb * hb, axis=0, keepdims=True) + EPS)
    gp = (hf + hb) * nrm
    gm = (hf - hb) * nrm
    fa = jnp.dot(fwd_ref[0:seq, :], gp, precision=hp, preferred_element_type=F32)
    fb = jnp.dot(fwd_ref[seq:2 * seq, :], gm, precision=hp, preferred_element_type=F32)
    alt = jnp.where(row % 2 == 0, 1.0, -1.0)
    f_nyq = jnp.sum(alt * gp, axis=0, keepdims=True)
    p_ref[0] = fa
    fi_ref[0] = jnp.where(row == 0, 0.0, fb)
    s_ref[0] = jnp.where(row == 0, f_nyq, fa)


def _hyena_filters(seq, w1, b1, w2, b2, w3, decay, fwd_f32):
    feat_np, t_np = _hy_features(seq)
    nfeat, ffn = w1.shape
    width = decay.shape[-1]
    pf = LANES - ffn
    w1p = jnp.pad(w1, ((0, LANES - nfeat), (0, pf)))
    w2p = jnp.pad(w2, ((0, pf), (0, pf)))
    w3p = jnp.pad(w3, ((0, pf), (0, 0)))
    b1p = jnp.pad(b1, (0, pf)).reshape(1, LANES)
    b2p = jnp.pad(b2, (0, pf)).reshape(1, LANES)
    ct = _tile(width, 256)
    nct = width // ct
    dec = decay.reshape(HY_ORDER * 2, 1, width)
    full = lambda shape: pl.BlockSpec(shape, lambda o, j: (0,) * len(shape))
    out_sd = jax.ShapeDtypeStruct((HY_ORDER, seq, width), F32)
    out_spec = pl.BlockSpec((1, seq, ct), lambda o, j: (o, 0, j))
    return pl.pallas_call(
        _hyfilt_kernel,
        out_shape=(out_sd, out_sd, out_sd),
        grid=(HY_ORDER, nct),
        in_specs=[full(feat_np.shape), full(t_np.shape), full(w1p.shape), full(b1p.shape),
                  full(w2p.shape), full(b2p.shape),
                  pl.BlockSpec((LANES, ct), lambda o, j: (0, o * 2 * nct + j)),
                  pl.BlockSpec((LANES, ct), lambda o, j: (0, (o * 2 + 1) * nct + j)),
                  pl.BlockSpec((1, 1, ct), lambda o, j: (o * 2, 0, j)),
                  pl.BlockSpec((1, 1, ct), lambda o, j: (o * 2 + 1, 0, j)),
                  full(fwd_f32.shape)],
        out_specs=(out_spec, out_spec, out_spec),
        compiler_params=_cparams("parallel", "parallel"),
        name="hyena_filters",
    )(jnp.asarray(feat_np), jnp.asarray(t_np), w1p, b1p, w2p, b2p, w3p, w3p, dec, dec, fwd_f32)


def _short_conv(u, w):
    seq = u.shape[0]
    row = lax.broadcasted_iota(jnp.int32, (seq, 1), 0)
    prev = jnp.where(row == 0, 0.0, pltpu.roll(u, 1, 0))
    nxt = jnp.where(row == seq - 1, 0.0, pltpu.roll(u, seq - 1, 0))
    return prev * w[0:1, :] + u * w[1:2, :] + nxt * w[2:3, :]


def _hyena_kernel(uv_ref, u1_ref, u2_ref, cwv_ref, cw1_ref, cw2_ref, fwd_ref, inv_ref,
                  p_ref, fi_ref, s_ref, skip_ref, *rest):
    y_ref = rest[-1]
    seq = uv_ref.shape[0]
    z = _short_conv(uv_ref[...], cwv_ref[...])
    gates = (_short_conv(u1_ref[...], cw1_ref[...]), _short_conv(u2_ref[...], cw2_ref[...]))
    for order in range(HY_ORDER):
        zf = jnp.dot(fwd_ref[...], z.astype(BF16), preferred_element_type=F32)
        a, b = zf[:seq], zf[seq:]
        p, fi, s = p_ref[order], fi_ref[order], s_ref[order]
        ya = a * p - b * fi
        yb = a * fi + b * s
        conv = (jnp.dot(inv_ref[:, :seq], ya.astype(BF16), preferred_element_type=F32)
                + jnp.dot(inv_ref[:, seq:], yb.astype(BF16), preferred_element_type=F32))
        z = gates[order] * (conv + skip_ref[order:order + 1, :] * z)
    y_ref[...] = z.astype(y_ref.dtype)


def _hyena_channel_tile(seq, width):
    const = 2 * 2 * (2 * seq * seq * 2)
    per_channel = 20 * seq * 4
    fit = (VMEM_LIMIT_BYTES * 2 // 3 - const) // per_channel
    ct = LANES
    while ct * 2 <= min(fit, width):
        ct *= 2
    return _tile(width, ct)


def _hyena(proj, cols, row0, nb, seq, conv_w, filt, skip, fwd_bf, inv_bf, y_prev=None):
    width = skip.shape[-1]
    ct = _hyena_channel_tile(seq, width)
    nct = width // ct
    rb = row0 // seq
    off = cols["hy"]
    p_arr, fi_arr, s_arr = filt

    def u_spec(part):
        return pl.BlockSpec((seq, ct), lambda j, b, part=part: (rb + b, off // ct + part * nct + j))

    def cw_spec(part):
        return pl.BlockSpec((3, ct), lambda j, b, part=part: (0, part * nct + j))

    full = lambda shape: pl.BlockSpec(shape, lambda j, b: (0,) * len(shape))
    f_spec = pl.BlockSpec((HY_ORDER, seq, ct), lambda j, b: (0, 0, j))
    in_specs = [u_spec(0), u_spec(1), u_spec(2), cw_spec(0), cw_spec(1), cw_spec(2),
                full(fwd_bf.shape), full(inv_bf.shape), f_spec, f_spec, f_spec,
                pl.BlockSpec((HY_ORDER, ct), lambda j, b: (0, j))]
    args = [proj, proj, proj, conv_w, conv_w, conv_w, fwd_bf, inv_bf, p_arr, fi_arr, s_arr, skip]
    y_shape, aliases = _shared_rows_out(proj.shape[0], width, y_prev, in_specs, args)
    return pl.pallas_call(
        _hyena_kernel,
        out_shape=y_shape,
        grid=(nct, nb),
        in_specs=in_specs,
        out_specs=pl.BlockSpec((seq, ct), lambda j, b: (rb + b, j)),
        input_output_aliases=aliases,
        compiler_params=_cparams("parallel", "parallel"),
        name="hyena",
    )(*args)


def _rope_tables(seq, head_dim):
    nfreq = head_dim // 4
    rows = seq // GRID_W
    row = np.repeat(np.arange(rows, dtype=np.float64), GRID_W)
    col = np.tile(np.arange(GRID_W, dtype=np.float64), rows)
    inv = (ROPE_BASE ** (-2.0 * np.arange(nfreq, dtype=np.float32) / (2 * nfreq))).astype(np.float64)
    ar, ac = row[:, None] * inv, col[:, None] * inv
    cos = np.concatenate([np.cos(ar), np.cos(ar), np.cos(ac), np.cos(ac)], axis=1)
    sin = np.concatenate([-np.sin(ar), np.sin(ar), -np.sin(ac), np.sin(ac)], axis=1)
    return cos.astype(np.float32), sin.astype(np.float32)


def _rope(x, cos, sin):
    hd = x.shape[1]
    q = hd // 4
    lane = lax.broadcasted_iota(jnp.int32, x.shape, 1)
    first = (lane % (2 * q)) < q
    partner = jnp.where(first, pltpu.roll(x, hd - q, 1), pltpu.roll(x, q, 1))
    return x * cos + partner * sin


def _attn_kernel(*refs, rope, has_ctx, has_prev, emit_kv, groups, bq, scale):
    q_ref, k_ref, v_ref, qg_ref, kg_ref = refs[:5]
    pos = 5
    if rope:
        cos_ref, sin_ref = refs[pos:pos + 2]
        pos += 2
    if has_ctx:
        kc_ref, vc_ref = refs[pos:pos + 2]
        pos += 2
    pos += has_prev
    y_ref = refs[pos]
    pos += 1
    if emit_kv:
        ko_ref, vo_ref = refs[pos:pos + 2]

    seq, hd = k_ref.shape
    k = k_ref[...]
    kn = k * lax.rsqrt(jnp.mean(k * k, axis=-1, keepdims=True) + EPS) * kg_ref[...]
    v = v_ref[...]
    if emit_kv:
        ko_ref[...] = kn
        vo_ref[...] = v
    if rope:
        kn = _rope(kn, cos_ref[...], sin_ref[...])
    kb, vb = kn.astype(BF16), v.astype(BF16)
    if has_ctx:
        kcb, vcb = kc_ref[0, 0].astype(BF16), vc_ref[0, 0].astype(BF16)
    for g in range(groups):
        for qi in range(seq // bq):
            lo, hi = qi * bq, (qi + 1) * bq
            q = q_ref[lo:hi, g * hd:(g + 1) * hd]
            qn = q * lax.rsqrt(jnp.mean(q * q, axis=-1, keepdims=True) + EPS) * qg_ref[...]
            if rope:
                qn = _rope(qn, cos_ref[lo:hi, :], sin_ref[lo:hi, :])
            qb = qn.astype(BF16)
            s1 = _nt_dot(qb, kb) * scale
            mx = jnp.max(s1, axis=-1, keepdims=True)
            if has_ctx:
                s2 = _nt_dot(qb, kcb) * scale
                mx = jnp.maximum(mx, jnp.max(s2, axis=-1, keepdims=True))
            p1 = jnp.exp(s1 - mx)
            den = jnp.sum(p1, axis=-1, keepdims=True)
            o = jnp.dot(p1.astype(BF16), vb, preferred_element_type=F32)
            if has_ctx:
                p2 = jnp.exp(s2 - mx)
                den = den + jnp.sum(p2, axis=-1, keepdims=True)
                o = o + jnp.dot(p2.astype(BF16), vcb, preferred_element_type=F32)
            y_ref[lo:hi, g * hd:(g + 1) * hd] = (o / den).astype(y_ref.dtype)


def _attention(proj, cols, row0, nb, seq, q_g, k_g, ctx, layer, rope, emit_kv, y_prev=None):
    hd = (cols["at_v"] - cols["at_k"]) // ATT_KV_HEADS
    groups = ATT_HEADS // ATT_KV_HEADS
    gw = groups * hd
    rb = row0 // seq
    in_specs = [pl.BlockSpec((seq, gw), lambda b, h: (rb + b, cols["at_q"] // gw + h)),
                pl.BlockSpec((seq, hd), lambda b, h: (rb + b, cols["at_k"] // hd + h)),
                pl.BlockSpec((seq, hd), lambda b, h: (rb + b, cols["at_v"] // hd + h)),
                pl.BlockSpec((1, hd), lambda b, h: (0, 0)),
                pl.BlockSpec((1, hd), lambda b, h: (0, 0))]
    args = [proj, proj, proj, q_g.reshape(1, hd), k_g.reshape(1, hd)]
    if rope:
        cos_np, sin_np = _rope_tables(seq, hd)
        in_specs += [pl.BlockSpec((seq, hd), lambda b, h: (0, 0))] * 2
        args += [jnp.asarray(cos_np), jnp.asarray(sin_np)]
    has_ctx = ctx is not None
    if has_ctx:
        ck, cv = ctx
        past = ck.shape[2]
        in_specs += [pl.BlockSpec((1, 1, past, hd), lambda b, h: (b, layer, 0, h))] * 2
        args += [ck, cv]
    y_shape, aliases = _shared_rows_out(proj.shape[0], ATT_HEADS * hd, y_prev, in_specs, args)
    out_shape = [y_shape]
    out_specs = [pl.BlockSpec((seq, gw), lambda b, h: (rb + b, h))]
    if emit_kv:
        out_shape += [jax.ShapeDtypeStruct((nb * seq, ATT_KV_HEADS * hd), F32)] * 2
        out_specs += [pl.BlockSpec((seq, hd), lambda b, h: (b, h))] * 2
    outs = pl.pallas_call(
        functools.partial(_attn_kernel, rope=rope, has_ctx=has_ctx, has_prev=y_prev is not None,
                          emit_kv=emit_kv, groups=groups, bq=min(seq, 256), scale=float(hd) ** -0.5),
        out_shape=tuple(out_shape),
        grid=(nb, ATT_KV_HEADS),
        in_specs=in_specs,
        out_specs=tuple(out_specs),
        input_output_aliases=aliases,
        compiler_params=_cparams("parallel", "parallel"),
        name="attention",
    )(*args)
    if emit_kv:
        return outs
    return outs[0], None, None


def _branch_kernel(y0_ref, y1_ref, y2_ref, g0_ref, g1_ref, g2_ref, w_ref, o_ref):
    acc = None
    for n, (y_ref, g_ref) in enumerate(((y0_ref, g0_ref), (y1_ref, g1_ref), (y2_ref, g2_ref))):
        p = jnp.dot(y_ref[...], w_ref[n], preferred_element_type=F32)
        term = jax.nn.sigmoid(g_ref[...]) * p
        acc = term if acc is None else acc + term
    o_ref[...] = acc.astype(o_ref.dtype)


def _branch_merge(ys, proj, cols, w_branch, layer, tm, tn):
    t, bw = ys[0].shape
    d = w_branch.shape[3]
    goff = cols["br_g"]
    y_spec = pl.BlockSpec((tm, bw), lambda i, j: (i, 0))

    def g_spec(n):
        return pl.BlockSpec((tm, tn), lambda i, j, n=n: (i, (goff + n * d) // tn + j))

    return pl.pallas_call(
        _branch_kernel,
        out_shape=jax.ShapeDtypeStruct((t, d), BF16),
        grid=(t // tm, d // tn),
        in_specs=[y_spec, y_spec, y_spec, g_spec(0), g_spec(1), g_spec(2),
                  pl.BlockSpec((None, N_BRANCH, bw, tn), lambda i, j: (layer, 0, 0, j))],
        out_specs=pl.BlockSpec((tm, tn), lambda i, j: (i, j)),
        compiler_params=_cparams("parallel", "arbitrary"),
        name="branch_merge",
    )(*ys, proj, proj, proj, w_branch)


def _proj_residual_kernel(a_ref, w_ref, x_ref, mod_ref, o_ref, *, gate_idx):
    y = jnp.dot(a_ref[...], w_ref[...], preferred_element_type=F32)
    o_ref[...] = x_ref[...] + mod_ref[0, gate_idx:gate_idx + 1, :] * y


def _proj_residual(a, w, layer, x, mod, seq_row, *, gate_idx, tm, tn):
    t, k = a.shape
    d = w.shape[2]
    return pl.pallas_call(
        functools.partial(_proj_residual_kernel, gate_idx=gate_idx),
        out_shape=jax.ShapeDtypeStruct((t, d), F32),
        grid=(t // tm, d // tn),
        in_specs=[pl.BlockSpec((tm, k), lambda i, j: (i, 0)),
                  pl.BlockSpec((None, k, tn), lambda i, j: (layer, 0, j)),
                  pl.BlockSpec((tm, tn), lambda i, j: (i, j)),
                  pl.BlockSpec((1, 6, tn), lambda i, j: (seq_row(i), 0, j))],
        out_specs=pl.BlockSpec((tm, tn), lambda i, j: (i, j)),
        compiler_params=_cparams("parallel", "arbitrary"),
        name="proj_residual",
    )(a, w, x, mod)


def _topk_rows(s, k, exact_ties):
    rows = s.shape[0]
    iota = lax.broadcasted_iota(jnp.int32, s.shape, 0).astype(F32)
    rank = jnp.full(s.shape, float(k), F32)
    vals = []
    for r in range(k):
        mx = jnp.max(s, axis=0, keepdims=True)
        sel = s == mx
        if exact_ties:
            sel = iota == jnp.min(jnp.where(sel, iota, float(rows)), axis=0, keepdims=True)
        rank = jnp.where(sel, float(r), rank)
        vals.append(mx)
        s = jnp.where(sel, -jnp.inf, s)
    count = jnp.sum(jnp.where(rank < float(k), 1.0, 0.0), axis=0, keepdims=True)
    return jnp.concatenate(vals, axis=0), rank, count


def _route_head(s1, s2, exact_ties):
    k = PEER_TOPK
    sv1, rk1, c1 = _topk_rows(s1, k, exact_ties)
    sv2, rk2, c2 = _topk_rows(s2, k, exact_ties)
    row8 = lax.broadcasted_iota(jnp.int32, (8, s1.shape[1]), 0)
    groups = [sv1[0:1, :] + sv2, sv1[1:2, :] + sv2[0:8, :]]
    for p in range(2, 8):
        groups.append(jnp.where(row8 < k // (p + 1), sv1[p:p + 1, :] + sv2[0:8, :], -jnp.inf))
    groups.append(sv1[8:16, :] + sv2[0:1, :])
    fv, rkc, c3 = _topk_rows(jnp.concatenate(groups, axis=0), k, exact_ties)
    z = jnp.sum(jnp.exp(fv - fv[0:1, :]), axis=0, keepdims=True)
    sel = jnp.where(rkc < float(k), 1.0, 0.0)
    cnt_i = jnp.zeros_like(s1)
    starts = [0, 16] + [24 + 8 * (p - 2) for p in range(2, 8)]
    sizes = [16, 8] + [8] * 6
    for p in range(k):
        if p < 8:
            cnt_p = jnp.sum(sel[starts[p]:starts[p] + sizes[p], :], axis=0, keepdims=True)
        else:
            cnt_p = sel[72 + p - 8:72 + p - 7, :]
        cnt_i = cnt_i + jnp.where(rk1 == float(p), cnt_p, 0.0)
    u1 = jnp.exp(s1 - sv1[0:1, :]) / z
    u2 = jnp.exp(s2 - sv2[0:1, :])
    return u1, cnt_i, u2, rk2, jnp.max(jnp.maximum(jnp.maximum(c1, c2), c3))


def _peer_route_kernel(q_ref, keys_ref, u1_ref, cnt_ref, u2_ref, rk2_ref):
    dh = keys_ref.shape[3]
    s1 = _nt_dot(keys_ref[0, 0], q_ref[:, 0:dh])
    s2 = _nt_dot(keys_ref[0, 1], q_ref[:, dh:2 * dh])

    def emit(u1, cnt, u2, rk2):
        u1_ref[0] = u1
        cnt_ref[0] = cnt
        u2_ref[0] = u2.astype(u2_ref.dtype)
        rk2_ref[0] = rk2.astype(rk2_ref.dtype)

    *fast, most = _route_head(s1, s2, exact_ties=False)
    emit(*fast)

    @pl.when(most > float(PEER_TOPK))
    def _():
        emit(*_route_head(s1, s2, exact_ties=True)[:4])


def _peer_route(q, keys_bf, tr):
    t = q.shape[0]
    heads, _, nkeys, dh = keys_bf.shape
    sd = jax.ShapeDtypeStruct((heads, nkeys, t), F32)
    sd16 = jax.ShapeDtypeStruct((heads, nkeys, t), BF16)
    spec = pl.BlockSpec((1, nkeys, tr), lambda i, h: (h, 0, i))
    return pl.pallas_call(
        _peer_route_kernel,
        out_shape=(sd, sd, sd16, sd16),
        grid=(t // tr, heads),
        in_specs=[pl.BlockSpec((tr, 2 * dh), lambda i, h: (i, h)),
                  pl.BlockSpec((1, 2, nkeys, dh), lambda i, h: (h, 0, 0, 0))],
        out_specs=(spec, spec, spec, spec),
        compiler_params=_cparams("parallel", "parallel"),
        name="peer_route",
    )(q, keys_bf)


def _peer_dense_kernel(ht_ref, u_ref, vt_ref, u1_ref, cnt_ref, u2_ref, rk2_ref, o_ref, w_ref, *, sub, group):
    e = pl.program_id(1)

    @pl.when(e == 0)
    def _():
        o_ref[...] = jnp.zeros_like(o_ref)

    te = u_ref.shape[0]
    nkeys, tm = u2_ref.shape[1:]
    ktiles = nkeys // BF16_ROWS
    for i in range(te // nkeys):
        w = None
        for h in range(PEER_HEADS):
            u1 = jnp.broadcast_to(u1_ref[h, i:i + 1, :], (BF16_ROWS, tm)).astype(BF16)[None]
            cn = jnp.broadcast_to(cnt_ref[h, i:i + 1, :], (BF16_ROWS, tm)).astype(BF16)[None]
            rk = rk2_ref[h].reshape(ktiles, BF16_ROWS, tm)
            u2 = u2_ref[h].reshape(ktiles, BF16_ROWS, tm)
            term = u1 * jnp.where(rk < cn, u2, jnp.zeros((), BF16))
            w = term if w is None else w + term
        w_ref[i * nkeys:(i + 1) * nkeys, :] = w.reshape(nkeys, tm)

    ht = ht_ref[...]
    nsub = te // sub
    pre = [jnp.dot(u_ref[s * sub:(s + 1) * sub, :], ht, preferred_element_type=F32) for s in range(nsub)]
    pending = []
    for s in range(nsub):
        act = jax.nn.gelu(pre[s].astype(BF16), approximate=True)
        pending.append(w_ref[s * sub:(s + 1) * sub, :] * act)
        if len(pending) == group or s + 1 == nsub:
            lo = (s + 1 - len(pending)) * sub
            wa = pending[0] if len(pending) == 1 else jnp.concatenate(pending, axis=0)
            o_ref[...] += jnp.dot(vt_ref[:, lo:(s + 1) * sub], wa, preferred_element_type=F32)
            pending = []


def _peer_dense(h2t, u_bf, vt_bf, layer, route, *, tm, te, sub, group):
    d, t = h2t.shape
    ne = u_bf.shape[1]
    heads, nkeys, _ = route[0].shape
    once = pl.Buffered(1)
    r_spec = pl.BlockSpec((heads, nkeys, tm), lambda i, e: (0, 0, i), pipeline_mode=once)
    k_spec = pl.BlockSpec((heads, te // nkeys, tm), lambda i, e: (0, e, i))
    return pl.pallas_call(
        functools.partial(_peer_dense_kernel, sub=sub, group=group),
        out_shape=jax.ShapeDtypeStruct((d, t), F32),
        grid=(t // tm, ne // te),
        scratch_shapes=[pltpu.VMEM((te, tm), BF16)],
        in_specs=[pl.BlockSpec((d, tm), lambda i, e: (0, i), pipeline_mode=once),
                  pl.BlockSpec((None, te, d), lambda i, e: (layer, e, 0)),
                  pl.BlockSpec((None, d, te), lambda i, e: (layer, 0, e)),
                  k_spec, k_spec, r_spec, r_spec],
        out_specs=pl.BlockSpec((d, tm), lambda i, e: (0, i)),
        compiler_params=_cparams("parallel", "arbitrary"),
        name="peer_dense",
    )(h2t, u_bf, vt_bf, *route)


def _peer_residual_kernel(x_ref, yt_ref, mod_ref, fg_ref, o_ref, *, final_norm):
    y = x_ref[...] + mod_ref[0, 5:6, :] * yt_ref[...].T
    if final_norm:
        y = y * lax.rsqrt(jnp.mean(y * y, axis=-1, keepdims=True) + EPS) * fg_ref[...]
    o_ref[...] = y


def _peer_residual(x, yt, mod, final_g, seq_row, *, final_norm, tm, first_tile=0, ntiles=None):
    t, d = x.shape
    ntiles = t // tm if ntiles is None else ntiles
    return pl.pallas_call(
        functools.partial(_peer_residual_kernel, final_norm=final_norm),
        out_shape=jax.ShapeDtypeStruct((ntiles * tm, d), F32),
        grid=(ntiles,),
        in_specs=[pl.BlockSpec((tm, d), lambda i: (first_tile + i, 0)),
                  pl.BlockSpec((d, tm), lambda i: (0, first_tile + i)),
                  pl.BlockSpec((1, 6, d), lambda i: (seq_row(first_tile + i), 0, 0)),
                  pl.BlockSpec((1, d), lambda i: (0, 0))],
        out_specs=pl.BlockSpec((tm, d), lambda i: (i, 0)),
        compiler_params=_cparams("parallel"),
        name="peer_residual",
    )(x, yt, mod, final_g.reshape(1, d))


def _in_layout(bw, kvw, d):
    sizes = (("ml_q", bw), ("ml_k", bw), ("ml_v", bw), ("ml_o", bw), ("hy", 3 * bw),
             ("at_q", bw), ("at_k", kvw), ("at_v", kvw), ("br_g", N_BRANCH * d), ("ml_g", GATE_PAD))
    cols, off = {}, 0
    for name, size in sizes:
        cols[name] = off
        off += size
    return cols, off


def _reorder_w_in(w, bw):
    d = w.shape[0]
    ngate = 4 * ML_HEADS
    gates = w[:, 4 * bw:4 * bw + ngate].reshape(d, 4, ML_HEADS)
    gates = jnp.transpose(gates, (0, 2, 1))
    gates = jnp.pad(gates, ((0, 0), (0, 0), (0, GATE_PAD // ML_HEADS - 4))).reshape(d, GATE_PAD)
    return jnp.concatenate([w[:, :4 * bw], w[:, 4 * bw + ngate:], gates], axis=1).astype(BF16)


def kernel(x_prompt, x_sample, c, cache_k, cache_v, state_C, state_n, state_m, c_ctx, w_ada, b_ada, norm1_g, norm2_g, w_in, ml_gate_bias, ml_norm_g, hy_conv_w, hy_w1, hy_b1, hy_w2, hy_b2, hy_w3, hy_decay, hy_skip, q_norm_g, k_norm_g, w_branch, w_out, peer_wq, peer_keys, peer_u, peer_v, final_g):
    nbp, seq_p, d = x_prompt.shape
    nbs, seq_s, _ = x_sample.shape
    depth = w_ada.shape[0]
    bw = d // 2
    kvw = cache_k.shape[3] * cache_k.shape[4]
    tp, ts = nbp * seq_p, nbs * seq_s
    cols, ncols = _in_layout(bw, kvw, d)
    assert w_in.shape[2] == ncols - GATE_PAD + 4 * ML_HEADS

    tm = _tile(seq_s, TOKEN_TILE)
    tn = TOKEN_TILE
    assert tp % tm == 0 and PEER_TOPK == 16
    tiles_p, tiles_per_seq = tp // tm, seq_s // tm

    def seq_row(i):
        return jnp.where(i < tiles_p, 0, 1 + (i - tiles_p) // tiles_per_seq)

    x = jnp.concatenate([x_prompt.reshape(tp, d), x_sample.reshape(ts, d)], axis=0)
    cond = jnp.concatenate([c_ctx[None, :], c], axis=0)
    nrow = cond.shape[0]
    cond = jnp.pad(cond, ((0, (-nrow) % 8), (0, 0)))
    mod_all = _ada(cond, w_ada, b_ada).reshape(depth, cond.shape[0], 6, d)

    ctx_k = cache_k.reshape(nbs, depth, cache_k.shape[2], kvw)
    ctx_v = cache_v.reshape(nbs, depth, cache_v.shape[2], kvw)
    dft = {}
    for seq in {seq_p, seq_s}:
        fwd_np, inv_np = _dft_mats(seq)
        dft[seq] = (jnp.asarray(fwd_np), jnp.asarray(fwd_np).astype(BF16), jnp.asarray(inv_np).astype(BF16))

    w_in_bf = jax.vmap(lambda w: _reorder_w_in(w, bw))(w_in)
    w_branch_bf, w_out_bf, wq_bf = w_branch.astype(BF16), w_out.astype(BF16), peer_wq.astype(BF16)
    u_bf = peer_u.astype(BF16)
    vt_bf = jnp.swapaxes(peer_v, 1, 2).astype(BF16)
    keys_bf = peer_keys.astype(BF16)

    new_k, new_v, new_c, new_n, new_m = [], [], [], [], []
    for l in range(depth):
        mod = mod_all[l]
        proj = _modnorm_matmul(x, mod, norm1_g[l], w_in_bf, l, seq_row, shift_idx=0, out_dtype=F32,
                               tm=tm, tn=_tile(ncols, tn))[0]
        gb = ml_gate_bias[l].reshape(4, ML_HEADS).T
        gb = jnp.pad(gb, ((0, 0), (0, LANES - 4))).reshape(ML_HEADS, 1, LANES)

        y_ml = y_hy = y_at = None
        for (row0, nb, seq, is_ctx) in ((0, nbp, seq_p, True), (tp, nbs, seq_s, False)):
            state = None if is_ctx else (state_C[:, l], state_n[:, l], state_m[:, l])
            y_ml, st = _mlstm(proj, cols, row0, nb, seq, gb, ml_norm_g[l], state, emit_state=is_ctx, y_prev=y_ml)
            fwd_f32, fwd_bf, inv_bf = dft[seq]
            filt = _hyena_filters(seq, hy_w1[l], hy_b1[l], hy_w2[l], hy_b2[l], hy_w3[l], hy_decay[l], fwd_f32)
            y_hy = _hyena(proj, cols, row0, nb, seq, hy_conv_w[l], filt, hy_skip[l], fwd_bf, inv_bf, y_prev=y_hy)
            y_at, k_l, v_l = _attention(proj, cols, row0, nb, seq, q_norm_g[l], k_norm_g[l],
                                        None if is_ctx else (ctx_k, ctx_v), l, rope=not is_ctx, emit_kv=is_ctx,
                                        y_prev=y_at)
            if is_ctx:
                new_k.append(k_l.reshape(nbp, seq_p, ATT_KV_HEADS, kvw // ATT_KV_HEADS))
                new_v.append(v_l.reshape(nbp, seq_p, ATT_KV_HEADS, kvw // ATT_KV_HEADS))
                new_c.append(st[0])
                new_n.append(st[1])
                new_m.append(st[2])
        mixed = _branch_merge((y_ml, y_hy, y_at), proj, cols, w_branch_bf, l, tm, tn // 2)
        x = _proj_residual(mixed, w_out_bf, l, x, mod, seq_row, gate_idx=2, tm=tm, tn=tn)

        q, _, h2t = _modnorm_matmul(x, mod, norm2_g[l], wq_bf, l, seq_row, shift_idx=3,
                                    out_dtype=BF16, tm=tm, tn=tn, emit_ht=True)
        route = _peer_route(q, keys_bf[l], _tile(tm, ROUTE_TILE))
        yt = _peer_dense(h2t, u_bf, vt_bf, l, route, tm=tm, te=EXPERT_TILE, sub=EXPERT_SUB,
                         group=EXPERT_GROUP)
        if l < depth - 1:
            x = _peer_residual(x, yt, mod, final_g, seq_row, final_norm=False, tm=tm)

    y_prompt = _peer_residual(x, yt, mod, final_g, seq_row, final_norm=True, tm=tm,
                              first_tile=0, ntiles=tiles_p).reshape(nbp, seq_p, d)
    y_sample = _peer_residual(x, yt, mod, final_g, seq_row, final_norm=True, tm=tm,
                              first_tile=tiles_p, ntiles=ts // tm).reshape(nbs, seq_s, d)
    return (y_prompt, y_sample, jnp.stack(new_k, axis=1), jnp.stack(new_v, axis=1),
            jnp.stack(new_c, axis=1), jnp.stack(new_n, axis=1), jnp.stack(new_m, axis=1))
```

```python
import functools

import numpy as np
import jax
import jax.numpy as jnp
from jax import lax
from jax.experimental import pallas as pl
from jax.experimental.pallas import tpu as pltpu

F32 = jnp.float32
BF16 = jnp.bfloat16
EPS = 1e-6

GRID_W = 64
ML_HEADS = 4
ATT_HEADS = 8
ATT_KV_HEADS = 2
ROPE_BASE = 10000.0
HY_BANDS = 16
HY_ORDER = 2
N_BRANCH = 3
PEER_HEADS = 8
PEER_TOPK = 16

VMEM_LIMIT_BYTES = 60 * 1024 * 1024
LANES = 128
BF16_ROWS = 16
ML_CHUNK = 256
TOKEN_TILE = 1024
ROUTE_TILE = 512
EXPERT_TILE = 1024
EXPERT_SUB = 256
EXPERT_GROUP = 4
GATE_PAD = 512


def _cparams(*sem):
    return pltpu.CompilerParams(dimension_semantics=sem, vmem_limit_bytes=VMEM_LIMIT_BYTES)


def _tile(n, pref):
    t = min(n, pref)
    while n % t:
        t //= 2
    return t


def _nt_dot(a, b):
    return lax.dot_general(a, b, (((1,), (1,)), ((), ())), preferred_element_type=F32)


def _ada_kernel(c_ref, w_ref, b_ref, o_ref):
    c = c_ref[...]
    a = c * jax.nn.sigmoid(c)
    o_ref[0] = jnp.dot(a.astype(BF16), w_ref[0].astype(BF16), preferred_element_type=F32) + b_ref[0]


def _ada(cond, w_ada, b_ada):
    depth, d, n = w_ada.shape
    r = cond.shape[0]
    tn = _tile(n, 1024)
    return pl.pallas_call(
        _ada_kernel,
        out_shape=jax.ShapeDtypeStruct((depth, r, n), F32),
        grid=(depth, n // tn),
        in_specs=[pl.BlockSpec((r, d), lambda l, j: (0, 0)),
                  pl.BlockSpec((1, d, tn), lambda l, j: (l, 0, j)),
                  pl.BlockSpec((1, 1, tn), lambda l, j: (l, 0, j))],
        out_specs=pl.BlockSpec((1, r, tn), lambda l, j: (l, 0, j)),
        compiler_params=_cparams("parallel", "parallel"),
        name="ada",
    )(cond, w_ada, b_ada.reshape(depth, 1, n))


def _modnorm_matmul_kernel(x_ref, mod_ref, g_ref, w_ref, o_ref, h_ref, *ht_ref, shift_idx):
    @pl.when(pl.program_id(1) == 0)
    def _():
        x = x_ref[...]
        y = x * lax.rsqrt(jnp.mean(x * x, axis=-1, keepdims=True) + EPS) * g_ref[...]
        sh = mod_ref[0, shift_idx:shift_idx + 1, :]
        sc = mod_ref[0, shift_idx + 1:shift_idx + 2, :]
        h = y * (1.0 + sc) + sh
        h_ref[...] = h.astype(BF16)
        if ht_ref:
            ht_ref[0][...] = h.T.astype(BF16)

    o_ref[...] = jnp.dot(h_ref[...], w_ref[...], preferred_element_type=F32).astype(o_ref.dtype)


def _modnorm_matmul(x, mod, g, w, layer, seq_row, *, shift_idx, out_dtype, tm, tn, emit_ht=False):
    t, d = x.shape
    n = w.shape[2]
    out_shape = [jax.ShapeDtypeStruct((t, n), out_dtype), jax.ShapeDtypeStruct((t, d), BF16)]
    out_specs = [pl.BlockSpec((tm, tn), lambda i, j: (i, j)), pl.BlockSpec((tm, d), lambda i, j: (i, 0))]
    if emit_ht:
        out_shape.append(jax.ShapeDtypeStruct((d, t), BF16))
        out_specs.append(pl.BlockSpec((d, tm), lambda i, j: (0, i)))
    return pl.pallas_call(
        functools.partial(_modnorm_matmul_kernel, shift_idx=shift_idx),
        out_shape=tuple(out_shape),
        grid=(t // tm, n // tn),
        in_specs=[pl.BlockSpec((tm, d), lambda i, j: (i, 0)),
                  pl.BlockSpec((1, 6, d), lambda i, j: (seq_row(i), 0, 0)),
                  pl.BlockSpec((1, d), lambda i, j: (0, 0)),
                  pl.BlockSpec((None, d, tn), lambda i, j: (layer, 0, j))],
        out_specs=tuple(out_specs),
        compiler_params=_cparams("parallel", "arbitrary"),
        name="modnorm_matmul",
    )(x, mod, g.reshape(1, d), w)


def _log_sigmoid(x):
    return jnp.minimum(x, 0.0) - jnp.log1p(jnp.exp(-jnp.abs(x)))


def _mlstm_chunk(qc, kc, vc, lf_col, ig_col, lf_row, ig_row, C, n, m, fwd, k_scale):
    tc = qc.shape[0]
    r = lax.broadcasted_iota(jnp.int32, (tc, tc), 0)
    c = lax.broadcasted_iota(jnp.int32, (tc, tc), 1)
    mask = (c <= r) if fwd else (c >= r)
    mask_t = (r <= c) if fwd else (r >= c)
    b_col = jnp.sum(jnp.where(mask, lf_row, 0.0), axis=1, keepdims=True)
    b_row = jnp.sum(jnp.where(mask_t, lf_col, 0.0), axis=0, keepdims=True)
    dmat = jnp.where(mask, b_col - b_row + ig_row, -jnp.inf)
    m_t = jnp.maximum(b_col + m, jnp.max(dmat, axis=1, keepdims=True))
    w_in = jnp.exp(dmat - m_t)
    w_prev = jnp.exp(b_col + m - m_t)
    qb, kb, vb = qc.astype(BF16), kc.astype(BF16), vc.astype(BF16)
    s = _nt_dot(qb, kb) * k_scale * w_in
    num = (jnp.dot(s.astype(BF16), vb, preferred_element_type=F32)
           + w_prev * jnp.dot(qb, C.astype(BF16), preferred_element_type=F32))
    den = jnp.sum(s, axis=1, keepdims=True) + w_prev * jnp.sum(qc * n, axis=1, keepdims=True)
    h = num / jnp.maximum(jnp.abs(den), jnp.exp(-m_t))
    b_last = jnp.sum(lf_row, axis=1, keepdims=True)
    lw_col = b_last - b_col + ig_col
    lw_row = b_last - b_row + ig_row
    m_new = jnp.maximum(b_last + m, jnp.max(lw_row, axis=1, keepdims=True))
    decay = jnp.exp(b_last + m - m_new)
    kw = kc * k_scale * jnp.exp(lw_col - m_new)
    C_new = decay * C + jnp.dot(kw.T.astype(BF16), vb, preferred_element_type=F32)
    n_new = decay * n + jnp.sum(kw, axis=0, keepdims=True)
    return h, C_new, n_new, m_new


def _shared_rows_out(total_rows, width, prev, in_specs, args):
    aliases = {}
    if prev is not None:
        in_specs.append(pl.BlockSpec(memory_space=pl.ANY))
        args.append(prev)
        aliases = {len(args) - 1: 0}
    return jax.ShapeDtypeStruct((total_rows, width), BF16), aliases


def _mlstm_kernel(*refs, seq, chunk, has_state, has_prev, emit_state, k_scale):
    q_ref, k_ref, v_ref, o_ref, g_ref, bias_ref, ng_ref = refs[:7]
    pos = 7
    if has_state:
        c0_ref, n0_ref, m0_ref = refs[pos:pos + 3]
        pos += 3
    pos += has_prev
    y_ref = refs[pos]
    pos += 1
    if emit_state:
        c_out, n_out, m_out = refs[pos:pos + 3]
        pos += 3
    hacc = refs[pos]

    g = g_ref[...] + bias_ref[0]
    gt = g.T
    dk = q_ref.shape[1]
    nchunks = seq // chunk
    for d in range(2):
        fwd = d == 0
        ig_col_all = g[:, 2 * d:2 * d + 1]
        lf_col_all = _log_sigmoid(g[:, 2 * d + 1:2 * d + 2])
        ig_row_all = gt[2 * d:2 * d + 1, :]
        lf_row_all = _log_sigmoid(gt[2 * d + 1:2 * d + 2, :])
        if has_state:
            C = c0_ref[0, d, 0]
            n = n0_ref[0, d, 0]
            m = m0_ref[0, d, 0][:, 0:1]
        else:
            C = jnp.zeros((dk, v_ref.shape[1]), F32)
            n = jnp.zeros((1, dk), F32)
            m = jnp.zeros((1, 1), F32)
        order = range(nchunks) if fwd else range(nchunks - 1, -1, -1)
        for ci in order:
            lo, hi = ci * chunk, (ci + 1) * chunk
            h, C, n, m = _mlstm_chunk(
                q_ref[lo:hi, :], k_ref[lo:hi, :], v_ref[lo:hi, :],
                lf_col_all[lo:hi], ig_col_all[lo:hi], lf_row_all[:, lo:hi], ig_row_all[:, lo:hi],
                C, n, m, fwd, k_scale)
            if fwd:
                hacc[lo:hi, :] = h
            else:
                hacc[lo:hi, :] += h
        if emit_state:
            c_out[0, d, 0] = C
            n_out[0, d, 0] = n
            m_out[0, d, 0] = jnp.broadcast_to(m, (1, LANES))
    hh = hacc[...]
    hn = hh * lax.rsqrt(jnp.mean(hh * hh, axis=-1, keepdims=True) + EPS) * ng_ref[0]
    y_ref[...] = (jax.nn.sigmoid(o_ref[...]) * hn).astype(y_ref.dtype)


def _mlstm(proj, cols, row0, nb, seq, gate_bias, norm_g, state, emit_state, y_prev=None):
    heads = ML_HEADS
    dk = (cols["ml_k"] - cols["ml_q"]) // heads
    chunk = min(ML_CHUNK, seq)
    rb = row0 // seq

    def col_spec(off, width):
        return pl.BlockSpec((seq, width), lambda b, h, off=off, width=width: (rb + b, off // width + h))

    in_specs = [col_spec(cols["ml_q"], dk), col_spec(cols["ml_k"], dk), col_spec(cols["ml_v"], dk),
                col_spec(cols["ml_o"], dk), col_spec(cols["ml_g"], LANES),
                pl.BlockSpec((1, 1, LANES), lambda b, h: (h, 0, 0)),
                pl.BlockSpec((1, 1, dk), lambda b, h: (h, 0, 0))]
    args = [proj, proj, proj, proj, proj, gate_bias, norm_g.reshape(heads, 1, dk)]
    has_state = state is not None
    if has_state:
        c0, n0, m0 = state
        in_specs += [pl.BlockSpec((1, 2, 1, dk, dk), lambda b, h: (b, 0, h, 0, 0)),
                     pl.BlockSpec((1, 2, 1, 1, dk), lambda b, h: (b, 0, h, 0, 0)),
                     pl.BlockSpec((1, 2, 1, 1, LANES), lambda b, h: (b, 0, h, 0, 0))]
        args += [c0, n0.reshape(nb, 2, heads, 1, dk),
                 jnp.broadcast_to(m0[..., None, None], (nb, 2, heads, 1, LANES))]
    y_shape, aliases = _shared_rows_out(proj.shape[0], heads * dk, y_prev, in_specs, args)
    out_shape = [y_shape]
    out_specs = [pl.BlockSpec((seq, dk), lambda b, h: (rb + b, h))]
    if emit_state:
        out_shape += [jax.ShapeDtypeStruct((nb, 2, heads, dk, dk), F32),
                      jax.ShapeDtypeStruct((nb, 2, heads, 1, dk), F32),
                      jax.ShapeDtypeStruct((nb, 2, heads, 1, LANES), F32)]
        out_specs += [pl.BlockSpec((1, 2, 1, dk, dk), lambda b, h: (b, 0, h, 0, 0)),
                      pl.BlockSpec((1, 2, 1, 1, dk), lambda b, h: (b, 0, h, 0, 0)),
                      pl.BlockSpec((1, 2, 1, 1, LANES), lambda b, h: (b, 0, h, 0, 0))]
    outs = pl.pallas_call(
        functools.partial(_mlstm_kernel, seq=seq, chunk=chunk, has_state=has_state,
                          has_prev=y_prev is not None, emit_state=emit_state, k_scale=float(dk) ** -0.5),
        out_shape=tuple(out_shape),
        grid=(nb, heads),
        in_specs=in_specs,
        out_specs=tuple(out_specs),
        input_output_aliases=aliases,
        scratch_shapes=[pltpu.VMEM((seq, dk), F32)],
        compiler_params=_cparams("parallel", "parallel"),
        name="mlstm",
    )(*args)
    if emit_state:
        y, c_new, n_new, m_new = outs
        return y, (c_new, n_new[:, :, :, 0, :], m_new[:, :, :, 0, 0])
    return outs[0], None


def _dft_mats(seq):
    k = np.arange(seq, dtype=np.int64)
    ang = np.pi * ((k[:, None] * k[None, :]) % (2 * seq)).astype(np.float64) / seq
    cos, sin = np.cos(ang), np.sin(ang)
    alt = np.where(k % 2 == 0, 1.0, -1.0)
    fwd_b = -sin
    fwd_b[0, :] = alt
    fwd = np.concatenate([cos, fwd_b], axis=0)
    inv_a = cos.T / seq
    inv_a[:, 0] = 0.5 / seq
    inv_b = -sin.T / seq
    inv_b[:, 0] = alt * 0.5 / seq
    inv = np.concatenate([inv_a, inv_b], axis=1)
    return fwd.astype(np.float32), inv.astype(np.float32)


def _hy_features(seq):
    pos = np.arange(seq, dtype=np.float64)
    t = pos / (seq - 1)
    bands = np.arange(1, HY_BANDS + 1, dtype=np.float64)
    ang = (2.0 * np.pi / seq) * pos[:, None] * bands[None, :]
    feat = np.concatenate([t[:, None], np.cos(ang), np.sin(ang)], axis=-1)
    feat = np.pad(feat, ((0, 0), (0, LANES - feat.shape[1])))
    return feat.astype(np.float32), t.astype(np.float32)[:, None]


def _dot_split(w_hi, w_lo, g):
    g_hi = g.astype(BF16)
    g_lo = (g - g_hi.astype(F32)).astype(BF16)
    return (jnp.dot(w_hi, g_hi, preferred_element_type=F32)
            + (jnp.dot(w_hi, g_lo, preferred_element_type=F32) + jnp.dot(w_lo, g_hi, preferred_element_type=F32)))


def _hyfilt_kernel(feat_ref, t_ref, w1_ref, b1_ref, w2_ref, b2_ref, w3f_ref, w3b_ref, decf_ref, decb_ref,
                   fwd_ref, fwd_lo_ref, p_ref, fi_ref, s_ref):
    hp = lax.Precision.HIGHEST
    h = jnp.sin(jnp.dot(feat_ref[...], w1_ref[...], precision=hp, preferred_element_type=F32) + b1_ref[...])
    h = jnp.sin(jnp.dot(h, w2_ref[...], precision=hp, preferred_element_type=F32) + b2_ref[...])
    t = t_ref[...]
    seq = t.shape[0]
    row = lax.broadcasted_iota(jnp.int32, (seq, 1), 0)
    hf = jnp.dot(h, w3f_ref[...], precision=hp, preferred_element_type=F32) * jnp.exp(-t * decf_ref[0, 0])
    hb = jnp.dot(h, w3b_ref[...], precision=hp, preferred_element_type=F32) * jnp.exp(-t * decb_ref[0, 0])
    hb = jnp.where(row == 0, 0.0, hb)
    nrm = lax.rsqrt(jnp.sum(hf * hf, axis=0, keepdims=True) + jnp.sum(hb * hb, axis=0, keepdims=True) + EPS)
    gp = (hf + hb) * nrm
    gm = (hf - hb) * nrm
    fa = _dot_split(fwd_ref[0:seq, :], fwd_lo_ref[0:seq, :], gp)
    fb = _dot_split(fwd_ref[seq:2 * seq, :], fwd_lo_ref[seq:2 * seq, :], gm)
    alt = jnp.where(row % 2 == 0, 1.0, -1.0)
    f_nyq = jnp.sum(alt * gp, axis=0, keepdims=True)
    p_ref[0] = fa
    fi_ref[0] = jnp.where(row == 0, 0.0, fb)
    s_ref[0] = jnp.where(row == 0, f_nyq, fa)


def _hyena_filters(seq, w1, b1, w2, b2, w3, decay, fwd_hi, fwd_lo):
    feat_np, t_np = _hy_features(seq)
    nfeat, ffn = w1.shape
    width = decay.shape[-1]
    pf = LANES - ffn
    w1p = jnp.pad(w1, ((0, LANES - nfeat), (0, pf)))
    w2p = jnp.pad(w2, ((0, pf), (0, pf)))
    w3p = jnp.pad(w3, ((0, pf), (0, 0)))
    b1p = jnp.pad(b1, (0, pf)).reshape(1, LANES)
    b2p = jnp.pad(b2, (0, pf)).reshape(1, LANES)
    ct = _tile(width, 256)
    nct = width // ct
    dec = decay.reshape(HY_ORDER * 2, 1, width)
    full = lambda shape: pl.BlockSpec(shape, lambda o, j: (0,) * len(shape))
    out_sd = jax.ShapeDtypeStruct((HY_ORDER, seq, width), F32)
    out_spec = pl.BlockSpec((1, seq, ct), lambda o, j: (o, 0, j))
    return pl.pallas_call(
        _hyfilt_kernel,
        out_shape=(out_sd, out_sd, out_sd),
        grid=(HY_ORDER, nct),
        in_specs=[full(feat_np.shape), full(t_np.shape), full(w1p.shape), full(b1p.shape),
                  full(w2p.shape), full(b2p.shape),
                  pl.BlockSpec((LANES, ct), lambda o, j: (0, o * 2 * nct + j)),
                  pl.BlockSpec((LANES, ct), lambda o, j: (0, (o * 2 + 1) * nct + j)),
                  pl.BlockSpec((1, 1, ct), lambda o, j: (o * 2, 0, j)),
                  pl.BlockSpec((1, 1, ct), lambda o, j: (o * 2 + 1, 0, j)),
                  full(fwd_hi.shape), full(fwd_lo.shape)],
        out_specs=(out_spec, out_spec, out_spec),
        compiler_params=_cparams("parallel", "parallel"),
        name="hyena_filters",
    )(jnp.asarray(feat_np), jnp.asarray(t_np), w1p, b1p, w2p, b2p, w3p, w3p, dec, dec, fwd_hi, fwd_lo)


def _short_conv(u, w):
    seq = u.shape[0]
    row = lax.broadcasted_iota(jnp.int32, (seq, 1), 0)
    prev = jnp.where(row == 0, 0.0, pltpu.roll(u, 1, 0))
    nxt = jnp.where(row == seq - 1, 0.0, pltpu.roll(u, seq - 1, 0))
    return prev * w[0:1, :] + u * w[1:2, :] + nxt * w[2:3, :]


def _hyena_kernel(uv_ref, u1_ref, u2_ref, cwv_ref, cw1_ref, cw2_ref, fwd_ref, inv_ref,
                  p_ref, fi_ref, s_ref, skip_ref, *rest):
    y_ref = rest[-1]
    seq = uv_ref.shape[0]
    z = _short_conv(uv_ref[...], cwv_ref[...])
    gates = (_short_conv(u1_ref[...], cw1_ref[...]), _short_conv(u2_ref[...], cw2_ref[...]))
    for order in range(HY_ORDER):
        zf = jnp.dot(fwd_ref[...], z.astype(BF16), preferred_element_type=F32)
        a, b = zf[:seq], zf[seq:]
        p, fi, s = p_ref[order], fi_ref[order], s_ref[order]
        ya = a * p - b * fi
        yb = a * fi + b * s
        conv = (jnp.dot(inv_ref[:, :seq], ya.astype(BF16), preferred_element_type=F32)
                + jnp.dot(inv_ref[:, seq:], yb.astype(BF16), preferred_element_type=F32))
        z = gates[order] * (conv + skip_ref[order:order + 1, :] * z)
    y_ref[...] = z.astype(y_ref.dtype)


def _hyena_channel_tile(seq, width):
    const = 2 * 2 * (2 * seq * seq * 2)
    per_channel = 20 * seq * 4
    fit = (VMEM_LIMIT_BYTES * 2 // 3 - const) // per_channel
    ct = LANES
    while ct * 2 <= min(fit, width):
        ct *= 2
    return _tile(width, ct)


def _hyena(proj, cols, row0, nb, seq, conv_w, filt, skip, fwd_bf, inv_bf, y_prev=None):
    width = skip.shape[-1]
    ct = _hyena_channel_tile(seq, width)
    nct = width // ct
    rb = row0 // seq
    off = cols["hy"]
    p_arr, fi_arr, s_arr = filt

    def u_spec(part):
        return pl.BlockSpec((seq, ct), lambda j, b, part=part: (rb + b, off // ct + part * nct + j))

    def cw_spec(part):
        return pl.BlockSpec((3, ct), lambda j, b, part=part: (0, part * nct + j))

    full = lambda shape: pl.BlockSpec(shape, lambda j, b: (0,) * len(shape))
    f_spec = pl.BlockSpec((HY_ORDER, seq, ct), lambda j, b: (0, 0, j))
    in_specs = [u_spec(0), u_spec(1), u_spec(2), cw_spec(0), cw_spec(1), cw_spec(2),
                full(fwd_bf.shape), full(inv_bf.shape), f_spec, f_spec, f_spec,
                pl.BlockSpec((HY_ORDER, ct), lambda j, b: (0, j))]
    args = [proj, proj, proj, conv_w, conv_w, conv_w, fwd_bf, inv_bf, p_arr, fi_arr, s_arr, skip]
    y_shape, aliases = _shared_rows_out(proj.shape[0], width, y_prev, in_specs, args)
    return pl.pallas_call(
        _hyena_kernel,
        out_shape=y_shape,
        grid=(nct, nb),
        in_specs=in_specs,
        out_specs=pl.BlockSpec((seq, ct), lambda j, b: (rb + b, j)),
        input_output_aliases=aliases,
        compiler_params=_cparams("parallel", "parallel"),
        name="hyena",
    )(*args)


def _rope_tables(seq, head_dim):
    nfreq = head_dim // 4
    rows = seq // GRID_W
    row = np.repeat(np.arange(rows, dtype=np.float64), GRID_W)
    col = np.tile(np.arange(GRID_W, dtype=np.float64), rows)
    inv = (ROPE_BASE ** (-2.0 * np.arange(nfreq, dtype=np.float32) / (2 * nfreq))).astype(np.float64)
    ar, ac = row[:, None] * inv, col[:, None] * inv
    cos = np.concatenate([np.cos(ar), np.cos(ar), np.cos(ac), np.cos(ac)], axis=1)
    sin = np.concatenate([-np.sin(ar), np.sin(ar), -np.sin(ac), np.sin(ac)], axis=1)
    return cos.astype(np.float32), sin.astype(np.float32)


def _rope(x, cos, sin):
    hd = x.shape[1]
    q = hd // 4
    lane = lax.broadcasted_iota(jnp.int32, x.shape, 1)
    first = (lane % (2 * q)) < q
    partner = jnp.where(first, pltpu.roll(x, hd - q, 1), pltpu.roll(x, q, 1))
    return x * cos + partner * sin


def _attn_kernel(*refs, rope, has_ctx, has_prev, emit_kv, groups, bq, scale):
    q_ref, k_ref, v_ref, qg_ref, kg_ref = refs[:5]
    pos = 5
    if rope:
        cos_ref, sin_ref = refs[pos:pos + 2]
        pos += 2
    if has_ctx:
        kc_ref, vc_ref = refs[pos:pos + 2]
        pos += 2
    pos += has_prev
    y_ref = refs[pos]
    pos += 1
    if emit_kv:
        ko_ref, vo_ref = refs[pos:pos + 2]

    seq, hd = k_ref.shape
    k = k_ref[...]
    kn = k * lax.rsqrt(jnp.mean(k * k, axis=-1, keepdims=True) + EPS) * kg_ref[...]
    v = v_ref[...]
    if emit_kv:
        ko_ref[...] = kn
        vo_ref[...] = v
    if rope:
        kn = _rope(kn, cos_ref[...], sin_ref[...])
    kb, vb = kn.astype(BF16), v.astype(BF16)
    if has_ctx:
        kcb, vcb = kc_ref[0, 0].astype(BF16), vc_ref[0, 0].astype(BF16)
    for g in range(groups):
        for qi in range(seq // bq):
            lo, hi = qi * bq, (qi + 1) * bq
            q = q_ref[lo:hi, g * hd:(g + 1) * hd]
            qn = q * lax.rsqrt(jnp.mean(q * q, axis=-1, keepdims=True) + EPS) * qg_ref[...]
            if rope:
                qn = _rope(qn, cos_ref[lo:hi, :], sin_ref[lo:hi, :])
            qb = qn.astype(BF16)
            s1 = _nt_dot(qb, kb) * scale
            mx = jnp.max(s1, axis=-1, keepdims=True)
            if has_ctx:
                s2 = _nt_dot(qb, kcb) * scale
                mx = jnp.maximum(mx, jnp.max(s2, axis=-1, keepdims=True))
            p1 = jnp.exp(s1 - mx)
            den = jnp.sum(p1, axis=-1, keepdims=True)
            o = jnp.dot(p1.astype(BF16), vb, preferred_element_type=F32)
            if has_ctx:
                p2 = jnp.exp(s2 - mx)
                den = den + jnp.sum(p2, axis=-1, keepdims=True)
                o = o + jnp.dot(p2.astype(BF16), vcb, preferred_element_type=F32)
            y_ref[lo:hi, g * hd:(g + 1) * hd] = (o / den).astype(y_ref.dtype)


def _attention(proj, cols, row0, nb, seq, q_g, k_g, ctx, layer, rope, emit_kv, y_prev=None):
    hd = (cols["at_v"] - cols["at_k"]) // ATT_KV_HEADS
    groups = ATT_HEADS // ATT_KV_HEADS
    gw = groups * hd
    rb = row0 // seq
    in_specs = [pl.BlockSpec((seq, gw), lambda b, h: (rb + b, cols["at_q"] // gw + h)),
                pl.BlockSpec((seq, hd), lambda b, h: (rb + b, cols["at_k"] // hd + h)),
                pl.BlockSpec((seq, hd), lambda b, h: (rb + b, cols["at_v"] // hd + h)),
                pl.BlockSpec((1, hd), lambda b, h: (0, 0)),
                pl.BlockSpec((1, hd), lambda b, h: (0, 0))]
    args = [proj, proj, proj, q_g.reshape(1, hd), k_g.reshape(1, hd)]
    if rope:
        cos_np, sin_np = _rope_tables(seq, hd)
        in_specs += [pl.BlockSpec((seq, hd), lambda b, h: (0, 0))] * 2
        args += [jnp.asarray(cos_np), jnp.asarray(sin_np)]
    has_ctx = ctx is not None
    if has_ctx:
        ck, cv = ctx
        past = ck.shape[2]
        in_specs += [pl.BlockSpec((1, 1, past, hd), lambda b, h: (b, layer, 0, h))] * 2
        args += [ck, cv]
    y_shape, aliases = _shared_rows_out(proj.shape[0], ATT_HEADS * hd, y_prev, in_specs, args)
    out_shape = [y_shape]
    out_specs = [pl.BlockSpec((seq, gw), lambda b, h: (rb + b, h))]
    if emit_kv:
        out_shape += [jax.ShapeDtypeStruct((nb * seq, ATT_KV_HEADS * hd), F32)] * 2
        out_specs += [pl.BlockSpec((seq, hd), lambda b, h: (b, h))] * 2
    outs = pl.pallas_call(
        functools.partial(_attn_kernel, rope=rope, has_ctx=has_ctx, has_prev=y_prev is not None,
                          emit_kv=emit_kv, groups=groups, bq=min(seq, 256), scale=float(hd) ** -0.5),
        out_shape=tuple(out_shape),
        grid=(nb, ATT_KV_HEADS),
        in_specs=in_specs,
        out_specs=tuple(out_specs),
        input_output_aliases=aliases,
        compiler_params=_cparams("parallel", "parallel"),
        name="attention",
    )(*args)
    if emit_kv:
        return outs
    return outs[0], None, None


def _branch_kernel(y0_ref, y1_ref, y2_ref, g0_ref, g1_ref, g2_ref, w_ref, o_ref):
    acc = None
    for n, (y_ref, g_ref) in enumerate(((y0_ref, g0_ref), (y1_ref, g1_ref), (y2_ref, g2_ref))):
        p = jnp.dot(y_ref[...], w_ref[n], preferred_element_type=F32)
        term = jax.nn.sigmoid(g_ref[...]) * p
        acc = term if acc is None else acc + term
    o_ref[...] = acc.astype(o_ref.dtype)


def _branch_merge(ys, proj, cols, w_branch, layer, tm, tn):
    t, bw = ys[0].shape
    d = w_branch.shape[3]
    goff = cols["br_g"]
    y_spec = pl.BlockSpec((tm, bw), lambda i, j: (i, 0))

    def g_spec(n):
        return pl.BlockSpec((tm, tn), lambda i, j, n=n: (i, (goff + n * d) // tn + j))

    return pl.pallas_call(
        _branch_kernel,
        out_shape=jax.ShapeDtypeStruct((t, d), BF16),
        grid=(t // tm, d // tn),
        in_specs=[y_spec, y_spec, y_spec, g_spec(0), g_spec(1), g_spec(2),
                  pl.BlockSpec((None, N_BRANCH, bw, tn), lambda i, j: (layer, 0, 0, j))],
        out_specs=pl.BlockSpec((tm, tn), lambda i, j: (i, j)),
        compiler_params=_cparams("parallel", "arbitrary"),
        name="branch_merge",
    )(*ys, proj, proj, proj, w_branch)


def _proj_residual_kernel(a_ref, w_ref, x_ref, mod_ref, o_ref, *, gate_idx):
    y = jnp.dot(a_ref[...], w_ref[...], preferred_element_type=F32)
    o_ref[...] = x_ref[...] + mod_ref[0, gate_idx:gate_idx + 1, :] * y


def _proj_residual(a, w, layer, x, mod, seq_row, *, gate_idx, tm, tn):
    t, k = a.shape
    d = w.shape[2]
    return pl.pallas_call(
        functools.partial(_proj_residual_kernel, gate_idx=gate_idx),
        out_shape=jax.ShapeDtypeStruct((t, d), F32),
        grid=(t // tm, d // tn),
        in_specs=[pl.BlockSpec((tm, k), lambda i, j: (i, 0)),
                  pl.BlockSpec((None, k, tn), lambda i, j: (layer, 0, j)),
                  pl.BlockSpec((tm, tn), lambda i, j: (i, j)),
                  pl.BlockSpec((1, 6, tn), lambda i, j: (seq_row(i), 0, j))],
        out_specs=pl.BlockSpec((tm, tn), lambda i, j: (i, j)),
        compiler_params=_cparams("parallel", "arbitrary"),
        name="proj_residual",
    )(a, w, x, mod)


def _topk_rows(s, k, exact_ties):
    rows = s.shape[0]
    iota = lax.broadcasted_iota(jnp.int32, s.shape, 0).astype(F32)
    rank = jnp.full(s.shape, float(k), F32)
    vals = []
    for r in range(k):
        mx = jnp.max(s, axis=0, keepdims=True)
        sel = s == mx
        if exact_ties:
            sel = iota == jnp.min(jnp.where(sel, iota, float(rows)), axis=0, keepdims=True)
        rank = jnp.where(sel, float(r), rank)
        vals.append(mx)
        s = jnp.where(sel, -jnp.inf, s)
    count = jnp.sum(jnp.where(rank < float(k), 1.0, 0.0), axis=0, keepdims=True)
    return jnp.concatenate(vals, axis=0), rank, count


def _route_head(s1, s2, exact_ties):
    k = PEER_TOPK
    sv1, rk1, c1 = _topk_rows(s1, k, exact_ties)
    sv2, rk2, c2 = _topk_rows(s2, k, exact_ties)
    row8 = lax.broadcasted_iota(jnp.int32, (8, s1.shape[1]), 0)
    groups = [sv1[0:1, :] + sv2, sv1[1:2, :] + sv2[0:8, :]]
    for p in range(2, 8):
        groups.append(jnp.where(row8 < k // (p + 1), sv1[p:p + 1, :] + sv2[0:8, :], -jnp.inf))
    groups.append(sv1[8:16, :] + sv2[0:1, :])
    fv, rkc, c3 = _topk_rows(jnp.concatenate(groups, axis=0), k, exact_ties)
    z = jnp.sum(jnp.exp(fv - fv[0:1, :]), axis=0, keepdims=True)
    sel = jnp.where(rkc < float(k), 1.0, 0.0)
    cnt_i = jnp.zeros_like(s1)
    starts = [0, 16] + [24 + 8 * (p - 2) for p in range(2, 8)]
    sizes = [16, 8] + [8] * 6
    for p in range(k):
        if p < 8:
            cnt_p = jnp.sum(sel[starts[p]:starts[p] + sizes[p], :], axis=0, keepdims=True)
        else:
            cnt_p = sel[72 + p - 8:72 + p - 7, :]
        cnt_i = cnt_i + jnp.where(rk1 == float(p), cnt_p, 0.0)
    u1 = jnp.exp(s1 - sv1[0:1, :]) / z
    u2 = jnp.exp(s2 - sv2[0:1, :])
    return u1, cnt_i, u2, rk2, jnp.max(jnp.maximum(jnp.maximum(c1, c2), c3))


def _peer_route_kernel(q_ref, keys_ref, u1_ref, cnt_ref, u2_ref, rk2_ref):
    dh = keys_ref.shape[3]
    s1 = _nt_dot(keys_ref[0, 0], q_ref[:, 0:dh])
    s2 = _nt_dot(keys_ref[0, 1], q_ref[:, dh:2 * dh])

    def emit(u1, cnt, u2, rk2):
        u1_ref[0] = u1
        cnt_ref[0] = cnt
        u2_ref[0] = u2.astype(u2_ref.dtype)
        rk2_ref[0] = rk2.astype(rk2_ref.dtype)

    *fast, most = _route_head(s1, s2, exact_ties=False)
    emit(*fast)

    @pl.when(most > float(PEER_TOPK))
    def _():
        emit(*_route_head(s1, s2, exact_ties=True)[:4])


def _peer_route(q, keys_bf, tr):
    t = q.shape[0]
    heads, _, nkeys, dh = keys_bf.shape
    sd = jax.ShapeDtypeStruct((heads, nkeys, t), F32)
    sd16 = jax.ShapeDtypeStruct((heads, nkeys, t), BF16)
    spec = pl.BlockSpec((1, nkeys, tr), lambda i, h: (h, 0, i))
    return pl.pallas_call(
        _peer_route_kernel,
        out_shape=(sd, sd, sd16, sd16),
        grid=(t // tr, heads),
        in_specs=[pl.BlockSpec((tr, 2 * dh), lambda i, h: (i, h)),
                  pl.BlockSpec((1, 2, nkeys, dh), lambda i, h: (h, 0, 0, 0))],
        out_specs=(spec, spec, spec, spec),
        compiler_params=_cparams("parallel", "parallel"),
        name="peer_route",
    )(q, keys_bf)


def _peer_dense_kernel(ht_ref, u_ref, vt_ref, u1_ref, cnt_ref, u2_ref, rk2_ref, o_ref, w_ref, *, sub, group):
    e = pl.program_id(1)

    @pl.when(e == 0)
    def _():
        o_ref[...] = jnp.zeros_like(o_ref)

    te = u_ref.shape[0]
    nkeys, tm = u2_ref.shape[1:]
    ktiles = nkeys // BF16_ROWS
    for i in range(te // nkeys):
        w = None
        for h in range(PEER_HEADS):
            u1 = jnp.broadcast_to(u1_ref[h, i:i + 1, :], (BF16_ROWS, tm)).astype(BF16)[None]
            cn = jnp.broadcast_to(cnt_ref[h, i:i + 1, :], (BF16_ROWS, tm)).astype(BF16)[None]
            rk = rk2_ref[h].reshape(ktiles, BF16_ROWS, tm)
            u2 = u2_ref[h].reshape(ktiles, BF16_ROWS, tm)
            term = u1 * jnp.where(rk < cn, u2, jnp.zeros((), BF16))
            w = term if w is None else w + term
        w_ref[i * nkeys:(i + 1) * nkeys, :] = w.reshape(nkeys, tm)

    ht = ht_ref[...]
    nsub = te // sub
    pre = [jnp.dot(u_ref[s * sub:(s + 1) * sub, :], ht, preferred_element_type=F32) for s in range(nsub)]
    pending = []
    for s in range(nsub):
        act = jax.nn.gelu(pre[s].astype(BF16), approximate=True)
        pending.append(w_ref[s * sub:(s + 1) * sub, :] * act)
        if len(pending) == group or s + 1 == nsub:
            lo = (s + 1 - len(pending)) * sub
            wa = pending[0] if len(pending) == 1 else jnp.concatenate(pending, axis=0)
            o_ref[...] += jnp.dot(vt_ref[:, lo:(s + 1) * sub], wa, preferred_element_type=F32)
            pending = []


def _peer_dense(h2t, u_bf, vt_bf, layer, route, *, tm, te, sub, group):
    d, t = h2t.shape
    ne = u_bf.shape[1]
    heads, nkeys, _ = route[0].shape
    once = pl.Buffered(1)
    r_spec = pl.BlockSpec((heads, nkeys, tm), lambda i, e: (0, 0, i), pipeline_mode=once)
    k_spec = pl.BlockSpec((heads, te // nkeys, tm), lambda i, e: (0, e, i))
    return pl.pallas_call(
        functools.partial(_peer_dense_kernel, sub=sub, group=group),
        out_shape=jax.ShapeDtypeStruct((d, t), F32),
        grid=(t // tm, ne // te),
        scratch_shapes=[pltpu.VMEM((te, tm), BF16)],
        in_specs=[pl.BlockSpec((d, tm), lambda i, e: (0, i), pipeline_mode=once),
                  pl.BlockSpec((None, te, d), lambda i, e: (layer, e, 0)),
                  pl.BlockSpec((None, d, te), lambda i, e: (layer, 0, e)),
                  k_spec, k_spec, r_spec, r_spec],
        out_specs=pl.BlockSpec((d, tm), lambda i, e: (0, i)),
        compiler_params=_cparams("parallel", "arbitrary"),
        name="peer_dense",
    )(h2t, u_bf, vt_bf, *route)


def _peer_residual_kernel(x_ref, yt_ref, mod_ref, fg_ref, o_ref, *, final_norm):
    y = x_ref[...] + mod_ref[0, 5:6, :] * yt_ref[...].T
    if final_norm:
        y = y * lax.rsqrt(jnp.mean(y * y, axis=-1, keepdims=True) + EPS) * fg_ref[...]
    o_ref[...] = y


def _peer_residual(x, yt, mod, final_g, seq_row, *, final_norm, tm, first_tile=0, ntiles=None):
    t, d = x.shape
    ntiles = t // tm if ntiles is None else ntiles
    return pl.pallas_call(
        functools.partial(_peer_residual_kernel, final_norm=final_norm),
        out_shape=jax.ShapeDtypeStruct((ntiles * tm, d), F32),
        grid=(ntiles,),
        in_specs=[pl.BlockSpec((tm, d), lambda i: (first_tile + i, 0)),
                  pl.BlockSpec((d, tm), lambda i: (0, first_tile + i)),
                  pl.BlockSpec((1, 6, d), lambda i: (seq_row(first_tile + i), 0, 0)),
                  pl.BlockSpec((1, d), lambda i: (0, 0))],
        out_specs=pl.BlockSpec((tm, d), lambda i: (i, 0)),
        compiler_params=_cparams("parallel"),
        name="peer_residual",
    )(x, yt, mod, final_g.reshape(1, d))


def _in_layout(bw, kvw, d):
    sizes = (("ml_q", bw), ("ml_k", bw), ("ml_v", bw), ("ml_o", bw), ("hy", 3 * bw),
             ("at_q", bw), ("at_k", kvw), ("at_v", kvw), ("br_g", N_BRANCH * d), ("ml_g", GATE_PAD))
    cols, off = {}, 0
    for name, size in sizes:
        cols[name] = off
        off += size
    return cols, off


def _pack_w_in_kernel(a_ref, b_ref, g_ref, o_ref, *, n_plain, n_shift, shift):
    j = pl.program_id(2)

    @pl.when(j < n_plain)
    def _():
        o_ref[...] = a_ref[...].astype(BF16)

    @pl.when((j >= n_plain) & (j < n_plain + n_shift))
    def _():
        o_ref[...] = jnp.concatenate([a_ref[:, shift:], b_ref[:, :shift]], axis=1).astype(BF16)

    @pl.when(j >= n_plain + n_shift)
    def _():
        o_ref[...] = g_ref[...]


def _pack_w_in(w_in, bw, ncols):
    depth, d, _ = w_in.shape
    ngate = 4 * ML_HEADS
    cb = GATE_PAD
    gates = w_in[:, :, 4 * bw:4 * bw + ngate].reshape(depth, d, 4, ML_HEADS)
    gates = jnp.swapaxes(gates, 2, 3)
    gates = jnp.pad(gates, ((0, 0), (0, 0), (0, 0), (0, cb // ML_HEADS - 4))).reshape(depth, d, cb).astype(BF16)
    nblk = ncols // cb
    n_plain = 4 * bw // cb
    last_src = pl.cdiv(w_in.shape[2], cb) - 1
    td = _tile(d, 512)
    return pl.pallas_call(
        functools.partial(_pack_w_in_kernel, n_plain=n_plain, n_shift=nblk - 1 - n_plain, shift=ngate),
        out_shape=jax.ShapeDtypeStruct((depth, d, ncols), BF16),
        grid=(depth, d // td, nblk),
        in_specs=[pl.BlockSpec((None, td, cb), lambda l, i, j: (l, i, jnp.minimum(j, last_src))),
                  pl.BlockSpec((None, td, cb), lambda l, i, j: (l, i, jnp.minimum(j + 1, last_src))),
                  pl.BlockSpec((None, td, cb), lambda l, i, j: (l, i, 0))],
        out_specs=pl.BlockSpec((None, td, cb), lambda l, i, j: (l, i, j)),
        compiler_params=_cparams("parallel", "parallel", "arbitrary"),
        name="pack_w_in",
    )(w_in, w_in, gates)


def kernel(x_prompt, x_sample, c, cache_k, cache_v, state_C, state_n, state_m, c_ctx, w_ada, b_ada, norm1_g, norm2_g, w_in, ml_gate_bias, ml_norm_g, hy_conv_w, hy_w1, hy_b1, hy_w2, hy_b2, hy_w3, hy_decay, hy_skip, q_norm_g, k_norm_g, w_branch, w_out, peer_wq, peer_keys, peer_u, peer_v, final_g):
    nbp, seq_p, d = x_prompt.shape
    nbs, seq_s, _ = x_sample.shape
    depth = w_ada.shape[0]
    bw = d // 2
    kvw = cache_k.shape[3] * cache_k.shape[4]
    tp, ts = nbp * seq_p, nbs * seq_s
    cols, ncols = _in_layout(bw, kvw, d)
    assert w_in.shape[2] == ncols - GATE_PAD + 4 * ML_HEADS

    tm = _tile(seq_s, TOKEN_TILE)
    tn = TOKEN_TILE
    assert tp % tm == 0 and PEER_TOPK == 16
    tiles_p, tiles_per_seq = tp // tm, seq_s // tm

    def seq_row(i):
        return jnp.where(i < tiles_p, 0, 1 + (i - tiles_p) // tiles_per_seq)

    x = jnp.concatenate([x_prompt.reshape(tp, d), x_sample.reshape(ts, d)], axis=0)
    cond = jnp.concatenate([c_ctx[None, :], c], axis=0)
    nrow = cond.shape[0]
    cond = jnp.pad(cond, ((0, (-nrow) % 8), (0, 0)))
    mod_all = _ada(cond, w_ada, b_ada).reshape(depth, cond.shape[0], 6, d)

    ctx_k = cache_k.reshape(nbs, depth, cache_k.shape[2], kvw)
    ctx_v = cache_v.reshape(nbs, depth, cache_v.shape[2], kvw)
    dft = {}
    for seq in {seq_p, seq_s}:
        fwd_np, inv_np = _dft_mats(seq)
        fwd_hi = jnp.asarray(fwd_np).astype(BF16)
        fwd_lo = (jnp.asarray(fwd_np) - fwd_hi.astype(F32)).astype(BF16)
        dft[seq] = (fwd_hi, fwd_lo, jnp.asarray(inv_np).astype(BF16))

    w_in_bf = _pack_w_in(w_in, bw, ncols)
    w_branch_bf, w_out_bf, wq_bf = w_branch.astype(BF16), w_out.astype(BF16), peer_wq.astype(BF16)
    u_bf = peer_u.astype(BF16)
    vt_bf = jnp.swapaxes(peer_v, 1, 2).astype(BF16)
    keys_bf = peer_keys.astype(BF16)

    new_k, new_v, new_c, new_n, new_m = [], [], [], [], []
    for l in range(depth):
        mod = mod_all[l]
        proj = _modnorm_matmul(x, mod, norm1_g[l], w_in_bf, l, seq_row, shift_idx=0, out_dtype=F32,
                               tm=tm, tn=_tile(ncols, tn))[0]
        gb = ml_gate_bias[l].reshape(4, ML_HEADS).T
        gb = jnp.pad(gb, ((0, 0), (0, LANES - 4))).reshape(ML_HEADS, 1, LANES)

        y_ml = y_hy = y_at = None
        for (row0, nb, seq, is_ctx) in ((0, nbp, seq_p, True), (tp, nbs, seq_s, False)):
            state = None if is_ctx else (state_C[:, l], state_n[:, l], state_m[:, l])
            y_ml, st = _mlstm(proj, cols, row0, nb, seq, gb, ml_norm_g[l], state, emit_state=is_ctx, y_prev=y_ml)
            fwd_bf, fwd_lo, inv_bf = dft[seq]
            filt = _hyena_filters(seq, hy_w1[l], hy_b1[l], hy_w2[l], hy_b2[l], hy_w3[l], hy_decay[l], fwd_bf, fwd_lo)
            y_hy = _hyena(proj, cols, row0, nb, seq, hy_conv_w[l], filt, hy_skip[l], fwd_bf, inv_bf, y_prev=y_hy)
            y_at, k_l, v_l = _attention(proj, cols, row0, nb, seq, q_norm_g[l], k_norm_g[l],
                                        None if is_ctx else (ctx_k, ctx_v), l, rope=not is_ctx, emit_kv=is_ctx,
                                        y_prev=y_at)
            if is_ctx:
                new_k.append(k_l.reshape(nbp, seq_p, ATT_KV_HEADS, kvw // ATT_KV_HEADS))
                new_v.append(v_l.reshape(nbp, seq_p, ATT_KV_HEADS, kvw // ATT_KV_HEADS))
                new_c.append(st[0])
                new_n.append(st[1])
                new_m.append(st[2])
        mixed = _branch_merge((y_ml, y_hy, y_at), proj, cols, w_branch_bf, l, tm, tn // 2)
        x = _proj_residual(mixed, w_out_bf, l, x, mod, seq_row, gate_idx=2, tm=tm, tn=tn)

        q, _, h2t = _modnorm_matmul(x, mod, norm2_g[l], wq_bf, l, seq_row, shift_idx=3,
                                    out_dtype=BF16, tm=tm, tn=tn, emit_ht=True)
        route = _peer_route(q, keys_bf[l], _tile(tm, ROUTE_TILE))
        yt = _peer_dense(h2t, u_bf, vt_bf, l, route, tm=tm, te=EXPERT_TILE, sub=EXPERT_SUB,
                         group=EXPERT_GROUP)
        if l < depth - 1:
            x = _peer_residual(x, yt, mod, final_g, seq_row, final_norm=False, tm=tm)

    y_prompt = _peer_residual(x, yt, mod, final_g, seq_row, final_norm=True, tm=tm,
                              first_tile=0, ntiles=tiles_p).reshape(nbp, seq_p, d)
    y_sample = _peer_residual(x, yt, mod, final_g, seq_row, final_norm=True, tm=tm,
                              first_tile=tiles_p, ntiles=ts // tm).reshape(nbs, seq_s, d)
    return (y_prompt, y_sample, jnp.stack(new_k, axis=1), jnp.stack(new_v, axis=1),
            jnp.stack(new_c, axis=1), jnp.stack(new_n, axis=1), jnp.stack(new_m, axis=1))
```

```python
import functools

import numpy as np
import jax
import jax.numpy as jnp
from jax import lax
from jax.experimental import pallas as pl
from jax.experimental.pallas import tpu as pltpu

F32 = jnp.float32
BF16 = jnp.bfloat16
EPS = 1e-6

GRID_W = 64
ML_HEADS = 4
ATT_HEADS = 8
ATT_KV_HEADS = 2
ROPE_BASE = 10000.0
HY_BANDS = 16
HY_ORDER = 2
N_BRANCH = 3
PEER_HEADS = 8
PEER_TOPK = 16

VMEM_LIMIT_BYTES = 60 * 1024 * 1024
LANES = 128
BF16_ROWS = 16
ML_CHUNK = 256
TOKEN_TILE = 1024
ROUTE_TILE = 512
EXPERT_TILE = 1024
EXPERT_SUB = 256
EXPERT_GROUP = 4
GATE_PAD = 512


def _cparams(*sem):
    return pltpu.CompilerParams(dimension_semantics=sem, vmem_limit_bytes=VMEM_LIMIT_BYTES)


def _tile(n, pref):
    t = min(n, pref)
    while n % t:
        t //= 2
    return t


def _nt_dot(a, b):
    return lax.dot_general(a, b, (((1,), (1,)), ((), ())), preferred_element_type=F32)


def _ada_kernel(c_ref, w_ref, b_ref, o_ref):
    c = c_ref[...]
    a = c * jax.nn.sigmoid(c)
    o_ref[0] = jnp.dot(a.astype(BF16), w_ref[0].astype(BF16), preferred_element_type=F32) + b_ref[0]


def _ada(cond, w_ada, b_ada):
    depth, d, n = w_ada.shape
    r = cond.shape[0]
    tn = _tile(n, 1024)
    return pl.pallas_call(
        _ada_kernel,
        out_shape=jax.ShapeDtypeStruct((depth, r, n), F32),
        grid=(depth, n // tn),
        in_specs=[pl.BlockSpec((r, d), lambda l, j: (0, 0)),
                  pl.BlockSpec((1, d, tn), lambda l, j: (l, 0, j)),
                  pl.BlockSpec((1, 1, tn), lambda l, j: (l, 0, j))],
        out_specs=pl.BlockSpec((1, r, tn), lambda l, j: (l, 0, j)),
        compiler_params=_cparams("parallel", "parallel"),
        name="ada",
    )(cond, w_ada, b_ada.reshape(depth, 1, n))


def _modnorm_matmul_kernel(x_ref, mod_ref, g_ref, w_ref, o_ref, h_ref, *ht_ref, shift_idx):
    @pl.when(pl.program_id(1) == 0)
    def _():
        x = x_ref[...]
        y = x * lax.rsqrt(jnp.mean(x * x, axis=-1, keepdims=True) + EPS) * g_ref[...]
        sh = mod_ref[0, shift_idx:shift_idx + 1, :]
        sc = mod_ref[0, shift_idx + 1:shift_idx + 2, :]
        h = y * (1.0 + sc) + sh
        h_ref[...] = h.astype(BF16)
        if ht_ref:
            ht_ref[0][...] = h.T.astype(BF16)

    o_ref[...] = jnp.dot(h_ref[...], w_ref[...], preferred_element_type=F32).astype(o_ref.dtype)


def _modnorm_matmul(x, mod, g, w, layer, seq_row, *, shift_idx, out_dtype, tm, tn, emit_ht=False):
    t, d = x.shape
    n = w.shape[2]
    out_shape = [jax.ShapeDtypeStruct((t, n), out_dtype), jax.ShapeDtypeStruct((t, d), BF16)]
    out_specs = [pl.BlockSpec((tm, tn), lambda i, j: (i, j)), pl.BlockSpec((tm, d), lambda i, j: (i, 0))]
    if emit_ht:
        out_shape.append(jax.ShapeDtypeStruct((d, t), BF16))
        out_specs.append(pl.BlockSpec((d, tm), lambda i, j: (0, i)))
    return pl.pallas_call(
        functools.partial(_modnorm_matmul_kernel, shift_idx=shift_idx),
        out_shape=tuple(out_shape),
        grid=(t // tm, n // tn),
        in_specs=[pl.BlockSpec((tm, d), lambda i, j: (i, 0)),
                  pl.BlockSpec((1, 6, d), lambda i, j: (seq_row(i), 0, 0)),
                  pl.BlockSpec((1, d), lambda i, j: (0, 0)),
                  pl.BlockSpec((None, d, tn), lambda i, j: (layer, 0, j))],
        out_specs=tuple(out_specs),
        compiler_params=_cparams("parallel", "arbitrary"),
        name="modnorm_matmul",
    )(x, mod, g.reshape(1, d), w)


def _log_sigmoid(x):
    return jnp.minimum(x, 0.0) - jnp.log1p(jnp.exp(-jnp.abs(x)))


def _mlstm_chunk(qc, kc, vc, lf_col, ig_col, lf_row, ig_row, C, n, m, fwd, k_scale):
    tc = qc.shape[0]
    r = lax.broadcasted_iota(jnp.int32, (tc, tc), 0)
    c = lax.broadcasted_iota(jnp.int32, (tc, tc), 1)
    mask = (c <= r) if fwd else (c >= r)
    mask_t = (r <= c) if fwd else (r >= c)
    b_col = jnp.sum(jnp.where(mask, lf_row, 0.0), axis=1, keepdims=True)
    b_row = jnp.sum(jnp.where(mask_t, lf_col, 0.0), axis=0, keepdims=True)
    dmat = jnp.where(mask, b_col - b_row + ig_row, -jnp.inf)
    m_t = jnp.maximum(b_col + m, jnp.max(dmat, axis=1, keepdims=True))
    w_in = jnp.exp(dmat - m_t)
    w_prev = jnp.exp(b_col + m - m_t)
    qb, kb, vb = qc.astype(BF16), kc.astype(BF16), vc.astype(BF16)
    s = _nt_dot(qb, kb) * k_scale * w_in
    num = (jnp.dot(s.astype(BF16), vb, preferred_element_type=F32)
           + w_prev * jnp.dot(qb, C.astype(BF16), preferred_element_type=F32))
    den = jnp.sum(s, axis=1, keepdims=True) + w_prev * jnp.sum(qc * n, axis=1, keepdims=True)
    h = num / jnp.maximum(jnp.abs(den), jnp.exp(-m_t))
    b_last = jnp.sum(lf_row, axis=1, keepdims=True)
    lw_col = b_last - b_col + ig_col
    lw_row = b_last - b_row + ig_row
    m_new = jnp.maximum(b_last + m, jnp.max(lw_row, axis=1, keepdims=True))
    decay = jnp.exp(b_last + m - m_new)
    kw = kc * k_scale * jnp.exp(lw_col - m_new)
    C_new = decay * C + jnp.dot(kw.T.astype(BF16), vb, preferred_element_type=F32)
    n_new = decay * n + jnp.sum(kw, axis=0, keepdims=True)
    return h, C_new, n_new, m_new


def _shared_rows_out(total_rows, width, prev, in_specs, args):
    aliases = {}
    if prev is not None:
        in_specs.append(pl.BlockSpec(memory_space=pl.ANY))
        args.append(prev)
        aliases = {len(args) - 1: 0}
    return jax.ShapeDtypeStruct((total_rows, width), BF16), aliases


def _mlstm_kernel(*refs, seq, chunk, has_state, has_prev, emit_state, k_scale):
    q_ref, k_ref, v_ref, o_ref, g_ref, bias_ref, ng_ref = refs[:7]
    pos = 7
    if has_state:
        c0_ref, n0_ref, m0_ref = refs[pos:pos + 3]
        pos += 3
    pos += has_prev
    y_ref = refs[pos]
    pos += 1
    if emit_state:
        c_out, n_out, m_out = refs[pos:pos + 3]
        pos += 3
    hacc = refs[pos]

    g = g_ref[...] + bias_ref[0]
    gt = g.T
    dk = q_ref.shape[1]
    nchunks = seq // chunk
    for d in range(2):
        fwd = d == 0
        ig_col_all = g[:, 2 * d:2 * d + 1]
        lf_col_all = _log_sigmoid(g[:, 2 * d + 1:2 * d + 2])
        ig_row_all = gt[2 * d:2 * d + 1, :]
        lf_row_all = _log_sigmoid(gt[2 * d + 1:2 * d + 2, :])
        if has_state:
            C = c0_ref[0, d, 0]
            n = n0_ref[0, d, 0]
            m = m0_ref[0, d, 0][:, 0:1]
        else:
            C = jnp.zeros((dk, v_ref.shape[1]), F32)
            n = jnp.zeros((1, dk), F32)
            m = jnp.zeros((1, 1), F32)
        order = range(nchunks) if fwd else range(nchunks - 1, -1, -1)
        for ci in order:
            lo, hi = ci * chunk, (ci + 1) * chunk
            h, C, n, m = _mlstm_chunk(
                q_ref[lo:hi, :], k_ref[lo:hi, :], v_ref[lo:hi, :],
                lf_col_all[lo:hi], ig_col_all[lo:hi], lf_row_all[:, lo:hi], ig_row_all[:, lo:hi],
                C, n, m, fwd, k_scale)
            if fwd:
                hacc[lo:hi, :] = h
            else:
                hacc[lo:hi, :] += h
        if emit_state:
            c_out[0, d, 0] = C
            n_out[0, d, 0] = n
            m_out[0, d, 0] = jnp.broadcast_to(m, (1, LANES))
    hh = hacc[...]
    hn = hh * lax.rsqrt(jnp.mean(hh * hh, axis=-1, keepdims=True) + EPS) * ng_ref[0]
    y_ref[...] = (jax.nn.sigmoid(o_ref[...]) * hn).astype(y_ref.dtype)


def _mlstm(proj, cols, row0, nb, seq, gate_bias, norm_g, state, emit_state, y_prev=None):
    heads = ML_HEADS
    dk = (cols["ml_k"] - cols["ml_q"]) // heads
    chunk = min(ML_CHUNK, seq)
    rb = row0 // seq

    def col_spec(off, width):
        return pl.BlockSpec((seq, width), lambda b, h, off=off, width=width: (rb + b, off // width + h))

    in_specs = [col_spec(cols["ml_q"], dk), col_spec(cols["ml_k"], dk), col_spec(cols["ml_v"], dk),
                col_spec(cols["ml_o"], dk), col_spec(cols["ml_g"], LANES),
                pl.BlockSpec((1, 1, LANES), lambda b, h: (h, 0, 0)),
                pl.BlockSpec((1, 1, dk), lambda b, h: (h, 0, 0))]
    args = [proj, proj, proj, proj, proj, gate_bias, norm_g.reshape(heads, 1, dk)]
    has_state = state is not None
    if has_state:
        c0, n0, m0 = state
        in_specs += [pl.BlockSpec((1, 2, 1, dk, dk), lambda b, h: (b, 0, h, 0, 0)),
                     pl.BlockSpec((1, 2, 1, 1, dk), lambda b, h: (b, 0, h, 0, 0)),
                     pl.BlockSpec((1, 2, 1, 1, LANES), lambda b, h: (b, 0, h, 0, 0))]
        args += [c0, n0.reshape(nb, 2, heads, 1, dk),
                 jnp.broadcast_to(m0[..., None, None], (nb, 2, heads, 1, LANES))]
    y_shape, aliases = _shared_rows_out(proj.shape[0], heads * dk, y_prev, in_specs, args)
    n_prev = len(aliases)
    out_shape = [y_shape]
    out_specs = [pl.BlockSpec((seq, dk), lambda b, h: (rb + b, h))]
    if emit_state:
        layer, depth, state_prev = emit_state
        out_shape += [jax.ShapeDtypeStruct((nb, depth, 2, heads, dk, dk), F32),
                      jax.ShapeDtypeStruct((nb, depth, 2, heads, 1, dk), F32),
                      jax.ShapeDtypeStruct((nb, depth, 2, heads, 1, LANES), F32)]
        out_specs += [pl.BlockSpec((1, None, 2, 1, dk, dk), lambda b, h: (b, layer, 0, h, 0, 0)),
                      pl.BlockSpec((1, None, 2, 1, 1, dk), lambda b, h: (b, layer, 0, h, 0, 0)),
                      pl.BlockSpec((1, None, 2, 1, 1, LANES), lambda b, h: (b, layer, 0, h, 0, 0))]
        if state_prev is not None:
            for k, prev in enumerate(state_prev):
                in_specs.append(pl.BlockSpec(memory_space=pl.ANY))
                args.append(prev)
                aliases[len(args) - 1] = 1 + k
            n_prev += len(state_prev)
    outs = pl.pallas_call(
        functools.partial(_mlstm_kernel, seq=seq, chunk=chunk, has_state=has_state,
                          has_prev=n_prev, emit_state=bool(emit_state), k_scale=float(dk) ** -0.5),
        out_shape=tuple(out_shape),
        grid=(nb, heads),
        in_specs=in_specs,
        out_specs=tuple(out_specs),
        input_output_aliases=aliases,
        scratch_shapes=[pltpu.VMEM((seq, dk), F32)],
        compiler_params=_cparams("parallel", "parallel"),
        name="mlstm",
    )(*args)
    if emit_state:
        return outs[0], tuple(outs[1:])
    return outs[0], None


def _dft_mats(seq):
    k = np.arange(seq, dtype=np.int64)
    ang = np.pi * ((k[:, None] * k[None, :]) % (2 * seq)).astype(np.float64) / seq
    cos, sin = np.cos(ang), np.sin(ang)
    alt = np.where(k % 2 == 0, 1.0, -1.0)
    fwd_b = -sin
    fwd_b[0, :] = alt
    fwd = np.concatenate([cos, fwd_b], axis=0)
    inv_a = cos.T / seq
    inv_a[:, 0] = 0.5 / seq
    inv_b = -sin.T / seq
    inv_b[:, 0] = alt * 0.5 / seq
    inv = np.concatenate([inv_a, inv_b], axis=1)
    return fwd.astype(np.float32), inv.astype(np.float32)


def _hy_features(seq):
    pos = np.arange(seq, dtype=np.float64)
    t = pos / (seq - 1)
    bands = np.arange(1, HY_BANDS + 1, dtype=np.float64)
    ang = (2.0 * np.pi / seq) * pos[:, None] * bands[None, :]
    feat = np.concatenate([t[:, None], np.cos(ang), np.sin(ang)], axis=-1)
    feat = np.pad(feat, ((0, 0), (0, LANES - feat.shape[1])))
    return feat.astype(np.float32), t.astype(np.float32)[:, None]


def _dot_split(w_hi, w_lo, g):
    g_hi = g.astype(BF16)
    g_lo = (g - g_hi.astype(F32)).astype(BF16)
    return (jnp.dot(w_hi, g_hi, preferred_element_type=F32)
            + (jnp.dot(w_hi, g_lo, preferred_element_type=F32) + jnp.dot(w_lo, g_hi, preferred_element_type=F32)))


def _hyfilt_kernel(feat_ref, t_ref, w1_ref, b1_ref, w2_ref, b2_ref, w3f_ref, w3b_ref, decf_ref, decb_ref,
                   fwd_ref, fwd_lo_ref, p_ref, fi_ref, s_ref):
    hp = lax.Precision.HIGHEST
    h = jnp.sin(jnp.dot(feat_ref[...], w1_ref[...], precision=hp, preferred_element_type=F32) + b1_ref[...])
    h = jnp.sin(jnp.dot(h, w2_ref[...], precision=hp, preferred_element_type=F32) + b2_ref[...])
    t = t_ref[...]
    seq = t.shape[0]
    row = lax.broadcasted_iota(jnp.int32, (seq, 1), 0)
    hf = jnp.dot(h, w3f_ref[...], precision=hp, preferred_element_type=F32) * jnp.exp(-t * decf_ref[0, 0])
    hb = jnp.dot(h, w3b_ref[...], precision=hp, preferred_element_type=F32) * jnp.exp(-t * decb_ref[0, 0])
    hb = jnp.where(row == 0, 0.0, hb)
    nrm = lax.rsqrt(jnp.sum(hf * hf, axis=0, keepdims=True) + jnp.sum(hb * hb, axis=0, keepdims=True) + EPS)
    gp = (hf + hb) * nrm
    gm = (hf - hb) * nrm
    fa = _dot_split(fwd_ref[0:seq, :], fwd_lo_ref[0:seq, :], gp)
    fb = _dot_split(fwd_ref[seq:2 * seq, :], fwd_lo_ref[seq:2 * seq, :], gm)
    alt = jnp.where(row % 2 == 0, 1.0, -1.0)
    f_nyq = jnp.sum(alt * gp, axis=0, keepdims=True)
    p_ref[0] = fa
    fi_ref[0] = jnp.where(row == 0, 0.0, fb)
    s_ref[0] = jnp.where(row == 0, f_nyq, fa)


def _hyena_filters(seq, w1, b1, w2, b2, w3, decay, fwd_hi, fwd_lo):
    feat_np, t_np = _hy_features(seq)
    nfeat, ffn = w1.shape
    width = decay.shape[-1]
    pf = LANES - ffn
    w1p = jnp.pad(w1, ((0, LANES - nfeat), (0, pf)))
    w2p = jnp.pad(w2, ((0, pf), (0, pf)))
    w3p = jnp.pad(w3, ((0, pf), (0, 0)))
    b1p = jnp.pad(b1, (0, pf)).reshape(1, LANES)
    b2p = jnp.pad(b2, (0, pf)).reshape(1, LANES)
    ct = _tile(width, 256)
    nct = width // ct
    dec = decay.reshape(HY_ORDER * 2, 1, width)
    full = lambda shape: pl.BlockSpec(shape, lambda o, j: (0,) * len(shape))
    out_sd = jax.ShapeDtypeStruct((HY_ORDER, seq, width), F32)
    out_spec = pl.BlockSpec((1, seq, ct), lambda o, j: (o, 0, j))
    return pl.pallas_call(
        _hyfilt_kernel,
        out_shape=(out_sd, out_sd, out_sd),
        grid=(HY_ORDER, nct),
        in_specs=[full(feat_np.shape), full(t_np.shape), full(w1p.shape), full(b1p.shape),
                  full(w2p.shape), full(b2p.shape),
                  pl.BlockSpec((LANES, ct), lambda o, j: (0, o * 2 * nct + j)),
                  pl.BlockSpec((LANES, ct), lambda o, j: (0, (o * 2 + 1) * nct + j)),
                  pl.BlockSpec((1, 1, ct), lambda o, j: (o * 2, 0, j)),
                  pl.BlockSpec((1, 1, ct), lambda o, j: (o * 2 + 1, 0, j)),
                  full(fwd_hi.shape), full(fwd_lo.shape)],
        out_specs=(out_spec, out_spec, out_spec),
        compiler_params=_cparams("parallel", "parallel"),
        name="hyena_filters",
    )(jnp.asarray(feat_np), jnp.asarray(t_np), w1p, b1p, w2p, b2p, w3p, w3p, dec, dec, fwd_hi, fwd_lo)


def _short_conv(u, w):
    seq = u.shape[0]
    row = lax.broadcasted_iota(jnp.int32, (seq, 1), 0)
    prev = jnp.where(row == 0, 0.0, pltpu.roll(u, 1, 0))
    nxt = jnp.where(row == seq - 1, 0.0, pltpu.roll(u, seq - 1, 0))
    return prev * w[0:1, :] + u * w[1:2, :] + nxt * w[2:3, :]


def _hyena_kernel(uv_ref, u1_ref, u2_ref, cwv_ref, cw1_ref, cw2_ref, fwd_ref, inv_ref,
                  p_ref, fi_ref, s_ref, skip_ref, *rest):
    y_ref = rest[-1]
    seq = uv_ref.shape[0]
    z = _short_conv(uv_ref[...], cwv_ref[...])
    gates = (_short_conv(u1_ref[...], cw1_ref[...]), _short_conv(u2_ref[...], cw2_ref[...]))
    for order in range(HY_ORDER):
        zf = jnp.dot(fwd_ref[...], z.astype(BF16), preferred_element_type=F32)
        a, b = zf[:seq], zf[seq:]
        p, fi, s = p_ref[order], fi_ref[order], s_ref[order]
        ya = a * p - b * fi
        yb = a * fi + b * s
        conv = (jnp.dot(inv_ref[:, :seq], ya.astype(BF16), preferred_element_type=F32)
                + jnp.dot(inv_ref[:, seq:], yb.astype(BF16), preferred_element_type=F32))
        z = gates[order] * (conv + skip_ref[order:order + 1, :] * z)
    y_ref[...] = z.astype(y_ref.dtype)


def _hyena_channel_tile(seq, width):
    const = 2 * 2 * (2 * seq * seq * 2)
    per_channel = 20 * seq * 4
    fit = (VMEM_LIMIT_BYTES * 2 // 3 - const) // per_channel
    ct = LANES
    while ct * 2 <= min(fit, width):
        ct *= 2
    return _tile(width, ct)


def _hyena(proj, cols, row0, nb, seq, conv_w, filt, skip, fwd_bf, inv_bf, y_prev=None):
    width = skip.shape[-1]
    ct = _hyena_channel_tile(seq, width)
    nct = width // ct
    rb = row0 // seq
    off = cols["hy"]
    p_arr, fi_arr, s_arr = filt

    def u_spec(part):
        return pl.BlockSpec((seq, ct), lambda j, b, part=part: (rb + b, off // ct + part * nct + j))

    def cw_spec(part):
        return pl.BlockSpec((3, ct), lambda j, b, part=part: (0, part * nct + j))

    full = lambda shape: pl.BlockSpec(shape, lambda j, b: (0,) * len(shape))
    f_spec = pl.BlockSpec((HY_ORDER, seq, ct), lambda j, b: (0, 0, j))
    in_specs = [u_spec(0), u_spec(1), u_spec(2), cw_spec(0), cw_spec(1), cw_spec(2),
                full(fwd_bf.shape), full(inv_bf.shape), f_spec, f_spec, f_spec,
                pl.BlockSpec((HY_ORDER, ct), lambda j, b: (0, j))]
    args = [proj, proj, proj, conv_w, conv_w, conv_w, fwd_bf, inv_bf, p_arr, fi_arr, s_arr, skip]
    y_shape, aliases = _shared_rows_out(proj.shape[0], width, y_prev, in_specs, args)
    return pl.pallas_call(
        _hyena_kernel,
        out_shape=y_shape,
        grid=(nct, nb),
        in_specs=in_specs,
        out_specs=pl.BlockSpec((seq, ct), lambda j, b: (rb + b, j)),
        input_output_aliases=aliases,
        compiler_params=_cparams("parallel", "parallel"),
        name="hyena",
    )(*args)


def _rope_tables(seq, head_dim):
    nfreq = head_dim // 4
    rows = seq // GRID_W
    row = np.repeat(np.arange(rows, dtype=np.float64), GRID_W)
    col = np.tile(np.arange(GRID_W, dtype=np.float64), rows)
    inv = (ROPE_BASE ** (-2.0 * np.arange(nfreq, dtype=np.float32) / (2 * nfreq))).astype(np.float64)
    ar, ac = row[:, None] * inv, col[:, None] * inv
    cos = np.concatenate([np.cos(ar), np.cos(ar), np.cos(ac), np.cos(ac)], axis=1)
    sin = np.concatenate([-np.sin(ar), np.sin(ar), -np.sin(ac), np.sin(ac)], axis=1)
    return cos.astype(np.float32), sin.astype(np.float32)


def _rope(x, cos, sin):
    hd = x.shape[1]
    q = hd // 4
    lane = lax.broadcasted_iota(jnp.int32, x.shape, 1)
    first = (lane % (2 * q)) < q
    partner = jnp.where(first, pltpu.roll(x, hd - q, 1), pltpu.roll(x, q, 1))
    return x * cos + partner * sin


def _attn_kernel(*refs, rope, has_ctx, has_prev, emit_kv, groups, bq, scale):
    q_ref, k_ref, v_ref, qg_ref, kg_ref = refs[:5]
    pos = 5
    if rope:
        cos_ref, sin_ref = refs[pos:pos + 2]
        pos += 2
    if has_ctx:
        kc_ref, vc_ref = refs[pos:pos + 2]
        pos += 2
    pos += has_prev
    y_ref = refs[pos]
    pos += 1
    if emit_kv:
        ko_ref, vo_ref = refs[pos:pos + 2]

    seq, hd = k_ref.shape
    k = k_ref[...]
    kn = k * lax.rsqrt(jnp.mean(k * k, axis=-1, keepdims=True) + EPS) * kg_ref[...]
    v = v_ref[...]
    if emit_kv:
        ko_ref[...] = kn
        vo_ref[...] = v
    if rope:
        kn = _rope(kn, cos_ref[...], sin_ref[...])
    kb, vb = kn.astype(BF16), v.astype(BF16)
    if has_ctx:
        kcb, vcb = kc_ref[0, 0].astype(BF16), vc_ref[0, 0].astype(BF16)
    for g in range(groups):
        for qi in range(seq // bq):
            lo, hi = qi * bq, (qi + 1) * bq
            q = q_ref[lo:hi, g * hd:(g + 1) * hd]
            qn = q * lax.rsqrt(jnp.mean(q * q, axis=-1, keepdims=True) + EPS) * qg_ref[...]
            if rope:
                qn = _rope(qn, cos_ref[lo:hi, :], sin_ref[lo:hi, :])
            qb = qn.astype(BF16)
            s1 = _nt_dot(qb, kb) * scale
            mx = jnp.max(s1, axis=-1, keepdims=True)
            if has_ctx:
                s2 = _nt_dot(qb, kcb) * scale
                mx = jnp.maximum(mx, jnp.max(s2, axis=-1, keepdims=True))
            p1 = jnp.exp(s1 - mx)
            den = jnp.sum(p1, axis=-1, keepdims=True)
            o = jnp.dot(p1.astype(BF16), vb, preferred_element_type=F32)
            if has_ctx:
                p2 = jnp.exp(s2 - mx)
                den = den + jnp.sum(p2, axis=-1, keepdims=True)
                o = o + jnp.dot(p2.astype(BF16), vcb, preferred_element_type=F32)
            y_ref[lo:hi, g * hd:(g + 1) * hd] = (o / den).astype(y_ref.dtype)


def _attention(proj, cols, row0, nb, seq, q_g, k_g, ctx, layer, rope, emit_kv, y_prev=None):
    hd = (cols["at_v"] - cols["at_k"]) // ATT_KV_HEADS
    groups = ATT_HEADS // ATT_KV_HEADS
    gw = groups * hd
    rb = row0 // seq
    in_specs = [pl.BlockSpec((seq, gw), lambda b, h: (rb + b, cols["at_q"] // gw + h)),
                pl.BlockSpec((seq, hd), lambda b, h: (rb + b, cols["at_k"] // hd + h)),
                pl.BlockSpec((seq, hd), lambda b, h: (rb + b, cols["at_v"] // hd + h)),
                pl.BlockSpec((1, hd), lambda b, h: (0, 0)),
                pl.BlockSpec((1, hd), lambda b, h: (0, 0))]
    args = [proj, proj, proj, q_g.reshape(1, hd), k_g.reshape(1, hd)]
    if rope:
        cos_np, sin_np = _rope_tables(seq, hd)
        in_specs += [pl.BlockSpec((seq, hd), lambda b, h: (0, 0))] * 2
        args += [jnp.asarray(cos_np), jnp.asarray(sin_np)]
    has_ctx = ctx is not None
    if has_ctx:
        ck, cv = ctx
        past = ck.shape[2]
        in_specs += [pl.BlockSpec((1, 1, past, hd), lambda b, h: (b, layer, 0, h))] * 2
        args += [ck, cv]
    y_shape, aliases = _shared_rows_out(proj.shape[0], ATT_HEADS * hd, y_prev, in_specs, args)
    n_prev = len(aliases)
    out_shape = [y_shape]
    out_specs = [pl.BlockSpec((seq, gw), lambda b, h: (rb + b, h))]
    if emit_kv:
        depth, kv_prev = emit_kv
        out_shape += [jax.ShapeDtypeStruct((nb, depth, seq, ATT_KV_HEADS * hd), F32)] * 2
        out_specs += [pl.BlockSpec((None, None, seq, hd), lambda b, h: (b, layer, 0, h))] * 2
        if kv_prev is not None:
            for k, prev in enumerate(kv_prev):
                in_specs.append(pl.BlockSpec(memory_space=pl.ANY))
                args.append(prev)
                aliases[len(args) - 1] = 1 + k
            n_prev += len(kv_prev)
    outs = pl.pallas_call(
        functools.partial(_attn_kernel, rope=rope, has_ctx=has_ctx, has_prev=n_prev,
                          emit_kv=bool(emit_kv), groups=groups, bq=min(seq, 256), scale=float(hd) ** -0.5),
        out_shape=tuple(out_shape),
        grid=(nb, ATT_KV_HEADS),
        in_specs=in_specs,
        out_specs=tuple(out_specs),
        input_output_aliases=aliases,
        compiler_params=_cparams("parallel", "parallel"),
        name="attention",
    )(*args)
    if emit_kv:
        return outs
    return outs[0], None, None


def _branch_kernel(y0_ref, y1_ref, y2_ref, g0_ref, g1_ref, g2_ref, w_ref, o_ref):
    acc = None
    for n, (y_ref, g_ref) in enumerate(((y0_ref, g0_ref), (y1_ref, g1_ref), (y2_ref, g2_ref))):
        p = jnp.dot(y_ref[...], w_ref[n], preferred_element_type=F32)
        term = jax.nn.sigmoid(g_ref[...]) * p
        acc = term if acc is None else acc + term
    o_ref[...] = acc.astype(o_ref.dtype)


def _branch_merge(ys, proj, cols, w_branch, layer, tm, tn):
    t, bw = ys[0].shape
    d = w_branch.shape[3]
    goff = cols["br_g"]
    y_spec = pl.BlockSpec((tm, bw), lambda i, j: (i, 0))

    def g_spec(n):
        return pl.BlockSpec((tm, tn), lambda i, j, n=n: (i, (goff + n * d) // tn + j))

    return pl.pallas_call(
        _branch_kernel,
        out_shape=jax.ShapeDtypeStruct((t, d), BF16),
        grid=(t // tm, d // tn),
        in_specs=[y_spec, y_spec, y_spec, g_spec(0), g_spec(1), g_spec(2),
                  pl.BlockSpec((None, N_BRANCH, bw, tn), lambda i, j: (layer, 0, 0, j))],
        out_specs=pl.BlockSpec((tm, tn), lambda i, j: (i, j)),
        compiler_params=_cparams("parallel", "arbitrary"),
        name="branch_merge",
    )(*ys, proj, proj, proj, w_branch)


def _proj_residual_kernel(a_ref, w_ref, x_ref, mod_ref, o_ref, *, gate_idx):
    y = jnp.dot(a_ref[...], w_ref[...], preferred_element_type=F32)
    o_ref[...] = x_ref[...] + mod_ref[0, gate_idx:gate_idx + 1, :] * y


def _proj_residual(a, w, layer, x, mod, seq_row, *, gate_idx, tm, tn):
    t, k = a.shape
    d = w.shape[2]
    return pl.pallas_call(
        functools.partial(_proj_residual_kernel, gate_idx=gate_idx),
        out_shape=jax.ShapeDtypeStruct((t, d), F32),
        grid=(t // tm, d // tn),
        in_specs=[pl.BlockSpec((tm, k), lambda i, j: (i, 0)),
                  pl.BlockSpec((None, k, tn), lambda i, j: (layer, 0, j)),
                  pl.BlockSpec((tm, tn), lambda i, j: (i, j)),
                  pl.BlockSpec((1, 6, tn), lambda i, j: (seq_row(i), 0, j))],
        out_specs=pl.BlockSpec((tm, tn), lambda i, j: (i, j)),
        compiler_params=_cparams("parallel", "arbitrary"),
        name="proj_residual",
    )(a, w, x, mod)


def _topk_rows(s, k, exact_ties):
    rows = s.shape[0]
    iota = lax.broadcasted_iota(jnp.int32, s.shape, 0).astype(F32)
    rank = jnp.full(s.shape, float(k), F32)
    vals = []
    for r in range(k):
        mx = jnp.max(s, axis=0, keepdims=True)
        sel = s == mx
        if exact_ties:
            sel = iota == jnp.min(jnp.where(sel, iota, float(rows)), axis=0, keepdims=True)
        rank = jnp.where(sel, float(r), rank)
        vals.append(mx)
        s = jnp.where(sel, -jnp.inf, s)
    count = jnp.sum(jnp.where(rank < float(k), 1.0, 0.0), axis=0, keepdims=True)
    return jnp.concatenate(vals, axis=0), rank, count


def _route_head(s1, s2, exact_ties):
    k = PEER_TOPK
    sv1, rk1, c1 = _topk_rows(s1, k, exact_ties)
    sv2, rk2, c2 = _topk_rows(s2, k, exact_ties)
    row8 = lax.broadcasted_iota(jnp.int32, (8, s1.shape[1]), 0)
    groups = [sv1[0:1, :] + sv2, sv1[1:2, :] + sv2[0:8, :]]
    for p in range(2, 8):
        groups.append(jnp.where(row8 < k // (p + 1), sv1[p:p + 1, :] + sv2[0:8, :], -jnp.inf))
    groups.append(sv1[8:16, :] + sv2[0:1, :])
    fv, rkc, c3 = _topk_rows(jnp.concatenate(groups, axis=0), k, exact_ties)
    z = jnp.sum(jnp.exp(fv - fv[0:1, :]), axis=0, keepdims=True)
    sel = jnp.where(rkc < float(k), 1.0, 0.0)
    cnt_i = jnp.zeros_like(s1)
    starts = [0, 16] + [24 + 8 * (p - 2) for p in range(2, 8)]
    sizes = [16, 8] + [8] * 6
    for p in range(k):
        if p < 8:
            cnt_p = jnp.sum(sel[starts[p]:starts[p] + sizes[p], :], axis=0, keepdims=True)
        else:
            cnt_p = sel[72 + p - 8:72 + p - 7, :]
        cnt_i = cnt_i + jnp.where(rk1 == float(p), cnt_p, 0.0)
    u1 = jnp.exp(s1 - sv1[0:1, :]) / z
    u2 = jnp.exp(s2 - sv2[0:1, :])
    return u1, cnt_i, u2, rk2, jnp.max(jnp.maximum(jnp.maximum(c1, c2), c3))


def _peer_route_kernel(q_ref, keys_ref, u1_ref, cnt_ref, u2_ref, rk2_ref):
    dh = keys_ref.shape[3]
    s1 = _nt_dot(keys_ref[0, 0], q_ref[:, 0:dh])
    s2 = _nt_dot(keys_ref[0, 1], q_ref[:, dh:2 * dh])

    def emit(u1, cnt, u2, rk2):
        u1_ref[0] = u1
        cnt_ref[0] = cnt
        u2_ref[0] = u2.astype(u2_ref.dtype)
        rk2_ref[0] = rk2.astype(rk2_ref.dtype)

    *fast, most = _route_head(s1, s2, exact_ties=False)
    emit(*fast)

    @pl.when(most > float(PEER_TOPK))
    def _():
        emit(*_route_head(s1, s2, exact_ties=True)[:4])


def _peer_route(q, keys_bf, tr):
    t = q.shape[0]
    heads, _, nkeys, dh = keys_bf.shape
    sd = jax.ShapeDtypeStruct((heads, nkeys, t), F32)
    sd16 = jax.ShapeDtypeStruct((heads, nkeys, t), BF16)
    spec = pl.BlockSpec((1, nkeys, tr), lambda i, h: (h, 0, i))
    return pl.pallas_call(
        _peer_route_kernel,
        out_shape=(sd, sd, sd16, sd16),
        grid=(t // tr, heads),
        in_specs=[pl.BlockSpec((tr, 2 * dh), lambda i, h: (i, h)),
                  pl.BlockSpec((1, 2, nkeys, dh), lambda i, h: (h, 0, 0, 0))],
        out_specs=(spec, spec, spec, spec),
        compiler_params=_cparams("parallel", "parallel"),
        name="peer_route",
    )(q, keys_bf)


def _peer_dense_kernel(ht_ref, u_ref, vt_ref, u1_ref, cnt_ref, u2_ref, rk2_ref, o_ref, w_ref, *, sub, group):
    e = pl.program_id(1)

    @pl.when(e == 0)
    def _():
        o_ref[...] = jnp.zeros_like(o_ref)

    te = u_ref.shape[0]
    nkeys, tm = u2_ref.shape[1:]
    ktiles = nkeys // BF16_ROWS
    for i in range(te // nkeys):
        w = None
        for h in range(PEER_HEADS):
            u1 = jnp.broadcast_to(u1_ref[h, i:i + 1, :], (BF16_ROWS, tm)).astype(BF16)[None]
            cn = jnp.broadcast_to(cnt_ref[h, i:i + 1, :], (BF16_ROWS, tm)).astype(BF16)[None]
            rk = rk2_ref[h].reshape(ktiles, BF16_ROWS, tm)
            u2 = u2_ref[h].reshape(ktiles, BF16_ROWS, tm)
            term = u1 * jnp.where(rk < cn, u2, jnp.zeros((), BF16))
            w = term if w is None else w + term
        w_ref[i * nkeys:(i + 1) * nkeys, :] = w.reshape(nkeys, tm)

    ht = ht_ref[...]
    nsub = te // sub
    pre = [jnp.dot(u_ref[s * sub:(s + 1) * sub, :], ht, preferred_element_type=F32) for s in range(nsub)]
    pending = []
    for s in range(nsub):
        act = jax.nn.gelu(pre[s].astype(BF16), approximate=True)
        pending.append(w_ref[s * sub:(s + 1) * sub, :] * act)
        if len(pending) == group or s + 1 == nsub:
            lo = (s + 1 - len(pending)) * sub
            wa = pending[0] if len(pending) == 1 else jnp.concatenate(pending, axis=0)
            o_ref[...] += jnp.dot(vt_ref[:, lo:(s + 1) * sub], wa, preferred_element_type=F32)
            pending = []


def _peer_dense(h2t, u_bf, vt_bf, layer, route, *, tm, te, sub, group):
    d, t = h2t.shape
    ne = u_bf.shape[1]
    heads, nkeys, _ = route[0].shape
    once = pl.Buffered(1)
    r_spec = pl.BlockSpec((heads, nkeys, tm), lambda i, e: (0, 0, i), pipeline_mode=once)
    k_spec = pl.BlockSpec((heads, te // nkeys, tm), lambda i, e: (0, e, i))
    return pl.pallas_call(
        functools.partial(_peer_dense_kernel, sub=sub, group=group),
        out_shape=jax.ShapeDtypeStruct((d, t), F32),
        grid=(t // tm, ne // te),
        scratch_shapes=[pltpu.VMEM((te, tm), BF16)],
        in_specs=[pl.BlockSpec((d, tm), lambda i, e: (0, i), pipeline_mode=once),
                  pl.BlockSpec((None, te, d), lambda i, e: (layer, e, 0)),
                  pl.BlockSpec((None, d, te), lambda i, e: (layer, 0, e)),
                  k_spec, k_spec, r_spec, r_spec],
        out_specs=pl.BlockSpec((d, tm), lambda i, e: (0, i)),
        compiler_params=_cparams("parallel", "arbitrary"),
        name="peer_dense",
    )(h2t, u_bf, vt_bf, *route)


def _peer_residual_kernel(x_ref, yt_ref, mod_ref, fg_ref, o_ref, *, final_norm):
    y = x_ref[...] + mod_ref[0, 5:6, :] * yt_ref[...].T
    if final_norm:
        y = y * lax.rsqrt(jnp.mean(y * y, axis=-1, keepdims=True) + EPS) * fg_ref[...]
    o_ref[...] = y


def _peer_residual(x, yt, mod, final_g, seq_row, *, final_norm, tm, first_tile=0, ntiles=None):
    t, d = x.shape
    ntiles = t // tm if ntiles is None else ntiles
    return pl.pallas_call(
        functools.partial(_peer_residual_kernel, final_norm=final_norm),
        out_shape=jax.ShapeDtypeStruct((ntiles * tm, d), F32),
        grid=(ntiles,),
        in_specs=[pl.BlockSpec((tm, d), lambda i: (first_tile + i, 0)),
                  pl.BlockSpec((d, tm), lambda i: (0, first_tile + i)),
                  pl.BlockSpec((1, 6, d), lambda i: (seq_row(first_tile + i), 0, 0)),
                  pl.BlockSpec((1, d), lambda i: (0, 0))],
        out_specs=pl.BlockSpec((tm, d), lambda i: (i, 0)),
        compiler_params=_cparams("parallel"),
        name="peer_residual",
    )(x, yt, mod, final_g.reshape(1, d))


def _in_layout(bw, kvw, d):
    sizes = (("ml_q", bw), ("ml_k", bw), ("ml_v", bw), ("ml_o", bw), ("hy", 3 * bw),
             ("at_q", bw), ("at_k", kvw), ("at_v", kvw), ("br_g", N_BRANCH * d), ("ml_g", GATE_PAD))
    cols, off = {}, 0
    for name, size in sizes:
        cols[name] = off
        off += size
    return cols, off


def _pack_w_in(w_in, bw):
    depth, d, _ = w_in.shape
    ngate = 4 * ML_HEADS
    gates = w_in[:, :, 4 * bw:4 * bw + ngate].reshape(depth, d, 4, ML_HEADS)
    gates = jnp.swapaxes(gates, 2, 3)
    gates = jnp.pad(gates, ((0, 0), (0, 0), (0, 0), (0, GATE_PAD // ML_HEADS - 4))).reshape(depth, d, GATE_PAD)
    return jnp.concatenate([w_in[:, :, :4 * bw], w_in[:, :, 4 * bw + ngate:], gates], axis=2).astype(BF16)


def kernel(x_prompt, x_sample, c, cache_k, cache_v, state_C, state_n, state_m, c_ctx, w_ada, b_ada, norm1_g, norm2_g, w_in, ml_gate_bias, ml_norm_g, hy_conv_w, hy_w1, hy_b1, hy_w2, hy_b2, hy_w3, hy_decay, hy_skip, q_norm_g, k_norm_g, w_branch, w_out, peer_wq, peer_keys, peer_u, peer_v, final_g):
    nbp, seq_p, d = x_prompt.shape
    nbs, seq_s, _ = x_sample.shape
    depth = w_ada.shape[0]
    bw = d // 2
    kvw = cache_k.shape[3] * cache_k.shape[4]
    tp, ts = nbp * seq_p, nbs * seq_s
    cols, ncols = _in_layout(bw, kvw, d)
    assert w_in.shape[2] == ncols - GATE_PAD + 4 * ML_HEADS

    tm = _tile(seq_s, TOKEN_TILE)
    tn = TOKEN_TILE
    assert tp % tm == 0 and PEER_TOPK == 16
    tiles_p, tiles_per_seq = tp // tm, seq_s // tm

    def seq_row(i):
        return jnp.where(i < tiles_p, 0, 1 + (i - tiles_p) // tiles_per_seq)

    x = jnp.concatenate([x_prompt.reshape(tp, d), x_sample.reshape(ts, d)], axis=0)
    cond = jnp.concatenate([c_ctx[None, :], c], axis=0)
    nrow = cond.shape[0]
    cond = jnp.pad(cond, ((0, (-nrow) % 8), (0, 0)))
    mod_all = _ada(cond, w_ada, b_ada).reshape(depth, cond.shape[0], 6, d)

    ctx_k = cache_k.reshape(nbs, depth, cache_k.shape[2], kvw)
    ctx_v = cache_v.reshape(nbs, depth, cache_v.shape[2], kvw)
    dft = {}
    for seq in {seq_p, seq_s}:
        fwd_np, inv_np = _dft_mats(seq)
        fwd_hi = jnp.asarray(fwd_np).astype(BF16)
        fwd_lo = (jnp.asarray(fwd_np) - fwd_hi.astype(F32)).astype(BF16)
        dft[seq] = (fwd_hi, fwd_lo, jnp.asarray(inv_np).astype(BF16))

    w_in_bf = _pack_w_in(w_in, bw)
    w_branch_bf, w_out_bf, wq_bf = w_branch.astype(BF16), w_out.astype(BF16), peer_wq.astype(BF16)
    u_bf = peer_u.astype(BF16)
    vt_bf = jnp.swapaxes(peer_v, 1, 2).astype(BF16)
    keys_bf = peer_keys.astype(BF16)

    new_state = new_kv = None
    for l in range(depth):
        mod = mod_all[l]
        proj = _modnorm_matmul(x, mod, norm1_g[l], w_in_bf, l, seq_row, shift_idx=0, out_dtype=F32,
                               tm=tm, tn=_tile(ncols, tn))[0]
        gb = ml_gate_bias[l].reshape(4, ML_HEADS).T
        gb = jnp.pad(gb, ((0, 0), (0, LANES - 4))).reshape(ML_HEADS, 1, LANES)

        y_ml = y_hy = y_at = None
        for (row0, nb, seq, is_ctx) in ((0, nbp, seq_p, True), (tp, nbs, seq_s, False)):
            state = None if is_ctx else (state_C[:, l], state_n[:, l], state_m[:, l])
            y_ml, st = _mlstm(proj, cols, row0, nb, seq, gb, ml_norm_g[l], state,
                              emit_state=(l, depth, new_state) if is_ctx else None, y_prev=y_ml)
            fwd_bf, fwd_lo, inv_bf = dft[seq]
            filt = _hyena_filters(seq, hy_w1[l], hy_b1[l], hy_w2[l], hy_b2[l], hy_w3[l], hy_decay[l], fwd_bf, fwd_lo)
            y_hy = _hyena(proj, cols, row0, nb, seq, hy_conv_w[l], filt, hy_skip[l], fwd_bf, inv_bf, y_prev=y_hy)
            y_at, k_l, v_l = _attention(proj, cols, row0, nb, seq, q_norm_g[l], k_norm_g[l],
                                        None if is_ctx else (ctx_k, ctx_v), l, rope=not is_ctx,
                                        emit_kv=(depth, new_kv) if is_ctx else None, y_prev=y_at)
            if is_ctx:
                new_state, new_kv = st, (k_l, v_l)
        mixed = _branch_merge((y_ml, y_hy, y_at), proj, cols, w_branch_bf, l, tm, tn // 2)
        x = _proj_residual(mixed, w_out_bf, l, x, mod, seq_row, gate_idx=2, tm=tm, tn=tn)

        q, _, h2t = _modnorm_matmul(x, mod, norm2_g[l], wq_bf, l, seq_row, shift_idx=3,
                                    out_dtype=BF16, tm=tm, tn=tn, emit_ht=True)
        route = _peer_route(q, keys_bf[l], _tile(tm, ROUTE_TILE))
        yt = _peer_dense(h2t, u_bf, vt_bf, l, route, tm=tm, te=EXPERT_TILE, sub=EXPERT_SUB,
                         group=EXPERT_GROUP)
        if l < depth - 1:
            x = _peer_residual(x, yt, mod, final_g, seq_row, final_norm=False, tm=tm)

    y_prompt = _peer_residual(x, yt, mod, final_g, seq_row, final_norm=True, tm=tm,
                              first_tile=0, ntiles=tiles_p).reshape(nbp, seq_p, d)
    y_sample = _peer_residual(x, yt, mod, final_g, seq_row, final_norm=True, tm=tm,
                              first_tile=tiles_p, ntiles=ts // tm).reshape(nbs, seq_s, d)
    kv_shape = (nbp, depth, seq_p, ATT_KV_HEADS, kvw // ATT_KV_HEADS)
    new_c, new_n, new_m = new_state
    return (y_prompt, y_sample, new_kv[0].reshape(kv_shape), new_kv[1].reshape(kv_shape),
            new_c, new_n[:, :, :, :, 0, :], new_m[:, :, :, :, 0, 0])
```

```python
import functools

import numpy as np
import jax
import jax.numpy as jnp
from jax import lax
from jax.experimental import pallas as pl
from jax.experimental.pallas import tpu as pltpu

F32 = jnp.float32
BF16 = jnp.bfloat16
EPS = 1e-6
LOG2E = 1.4426950408889634

GRID_W = 64
ML_HEADS = 4
ATT_HEADS = 8
ATT_KV_HEADS = 2
ROPE_BASE = 10000.0
HY_BANDS = 16
HY_ORDER = 2
N_BRANCH = 3
PEER_HEADS = 8
PEER_TOPK = 16

VMEM_LIMIT_BYTES = 60 * 1024 * 1024
LANES = 128
BF16_ROWS = 16
ML_CHUNK = 512
ATT_BLOCK_Q = 256
TOKEN_TILE = 1024
ROUTE_TILE = 512
EXPERT_TILE = 1024
EXPERT_SUB = 256
EXPERT_GROUP = 4
GATE_PAD = 512


def _cparams(*sem):
    return pltpu.CompilerParams(dimension_semantics=sem, vmem_limit_bytes=VMEM_LIMIT_BYTES)


def _tile(n, pref):
    t = min(n, pref)
    while n % t:
        t //= 2
    return t


def _nt_dot(a, b):
    return lax.dot_general(a, b, (((1,), (1,)), ((), ())), preferred_element_type=F32)


def _ada_kernel(c_ref, w_ref, b_ref, o_ref):
    c = c_ref[...]
    a = c * jax.nn.sigmoid(c)
    o_ref[0] = jnp.dot(a.astype(BF16), w_ref[0].astype(BF16), preferred_element_type=F32) + b_ref[0]


def _ada(cond, w_ada, b_ada):
    depth, d, n = w_ada.shape
    r = cond.shape[0]
    tn = _tile(n, 1024)
    return pl.pallas_call(
        _ada_kernel,
        out_shape=jax.ShapeDtypeStruct((depth, r, n), F32),
        grid=(depth, n // tn),
        in_specs=[pl.BlockSpec((r, d), lambda l, j: (0, 0)),
                  pl.BlockSpec((1, d, tn), lambda l, j: (l, 0, j)),
                  pl.BlockSpec((1, 1, tn), lambda l, j: (l, 0, j))],
        out_specs=pl.BlockSpec((1, r, tn), lambda l, j: (l, 0, j)),
        compiler_params=_cparams("parallel", "parallel"),
        name="ada",
    )(cond, w_ada, b_ada.reshape(depth, 1, n))


def _modnorm_matmul_kernel(x_ref, mod_ref, g_ref, w_ref, o_ref, h_ref, *ht_ref, shift_idx):
    @pl.when(pl.program_id(1) == 0)
    def _():
        x = x_ref[...]
        y = x * lax.rsqrt(jnp.mean(x * x, axis=-1, keepdims=True) + EPS) * g_ref[...]
        sh = mod_ref[0, shift_idx:shift_idx + 1, :]
        sc = mod_ref[0, shift_idx + 1:shift_idx + 2, :]
        h = y * (1.0 + sc) + sh
        h_ref[...] = h.astype(BF16)
        if ht_ref:
            ht_ref[0][...] = h.T.astype(BF16)

    o_ref[...] = jnp.dot(h_ref[...], w_ref[...], preferred_element_type=F32).astype(o_ref.dtype)


def _modnorm_matmul(x, mod, g, w, layer, seq_row, *, shift_idx, out_dtype, tm, tn, emit_ht=False):
    t, d = x.shape
    n = w.shape[2]
    out_shape = [jax.ShapeDtypeStruct((t, n), out_dtype), jax.ShapeDtypeStruct((t, d), BF16)]
    out_specs = [pl.BlockSpec((tm, tn), lambda i, j: (i, j)), pl.BlockSpec((tm, d), lambda i, j: (i, 0))]
    if emit_ht:
        out_shape.append(jax.ShapeDtypeStruct((d, t), BF16))
        out_specs.append(pl.BlockSpec((d, tm), lambda i, j: (0, i)))
    return pl.pallas_call(
        functools.partial(_modnorm_matmul_kernel, shift_idx=shift_idx),
        out_shape=tuple(out_shape),
        grid=(t // tm, n // tn),
        in_specs=[pl.BlockSpec((tm, d), lambda i, j: (i, 0)),
                  pl.BlockSpec((1, 6, d), lambda i, j: (seq_row(i), 0, 0)),
                  pl.BlockSpec((1, d), lambda i, j: (0, 0)),
                  pl.BlockSpec((None, d, tn), lambda i, j: (layer, 0, j))],
        out_specs=tuple(out_specs),
        compiler_params=_cparams("parallel", "arbitrary"),
        name="modnorm_matmul",
    )(x, mod, g.reshape(1, d), w)


def _log_sigmoid(x):
    return jnp.minimum(x, 0.0) - jnp.log1p(jnp.exp(-jnp.abs(x)))


def _mlstm_chunk(qc, kc, vc, lf_col, ig_col, lf_row, ig_row, C, n, m, fwd, k_scale):
    tc = qc.shape[0]
    r = lax.broadcasted_iota(jnp.int32, (tc, tc), 0)
    c = lax.broadcasted_iota(jnp.int32, (tc, tc), 1)
    mask = (c <= r) if fwd else (c >= r)
    mask_t = (r <= c) if fwd else (r >= c)
    b_col = jnp.sum(jnp.where(mask, lf_row, 0.0), axis=1, keepdims=True)
    b_row = jnp.sum(jnp.where(mask_t, lf_col, 0.0), axis=0, keepdims=True)
    dmat = jnp.where(mask, b_col - b_row + ig_row, -jnp.inf)
    m_t = jnp.maximum(b_col + m, jnp.max(dmat, axis=1, keepdims=True))
    w_in = jnp.exp(dmat - m_t)
    w_prev = jnp.exp(b_col + m - m_t)
    qb, kb, vb = qc.astype(BF16), kc.astype(BF16), vc.astype(BF16)
    s = _nt_dot(qb, kb) * k_scale * w_in
    num = (jnp.dot(s.astype(BF16), vb, preferred_element_type=F32)
           + w_prev * jnp.dot(qb, C.astype(BF16), preferred_element_type=F32))
    den = jnp.sum(s, axis=1, keepdims=True) + w_prev * jnp.sum(qc * n, axis=1, keepdims=True)
    h = num / jnp.maximum(jnp.abs(den), jnp.exp(-m_t))
    b_last = jnp.sum(lf_row, axis=1, keepdims=True)
    lw_col = b_last - b_col + ig_col
    lw_row = b_last - b_row + ig_row
    m_new = jnp.maximum(b_last + m, jnp.max(lw_row, axis=1, keepdims=True))
    decay = jnp.exp(b_last + m - m_new)
    kw = kc * k_scale * jnp.exp(lw_col - m_new)
    C_new = decay * C + jnp.dot(kw.T.astype(BF16), vb, preferred_element_type=F32)
    n_new = decay * n + jnp.sum(kw, axis=0, keepdims=True)
    return h, C_new, n_new, m_new


def _shared_rows_out(total_rows, width, prev, in_specs, args):
    aliases = {}
    if prev is not None:
        in_specs.append(pl.BlockSpec(memory_space=pl.ANY))
        args.append(prev)
        aliases = {len(args) - 1: 0}
    return jax.ShapeDtypeStruct((total_rows, width), BF16), aliases


def _mlstm_kernel(*refs, seq, chunk, has_state, has_prev, emit_state, k_scale):
    q_ref, k_ref, v_ref, o_ref, g_ref, bias_ref, ng_ref = refs[:7]
    pos = 7
    if has_state:
        c0_ref, n0_ref, m0_ref = refs[pos:pos + 3]
        pos += 3
    pos += has_prev
    y_ref = refs[pos]
    pos += 1
    if emit_state:
        c_out, n_out, m_out = refs[pos:pos + 3]
        pos += 3
    hacc = refs[pos]

    g = g_ref[...] + bias_ref[0]
    gt = g.T
    dk = q_ref.shape[1]
    nchunks = seq // chunk
    for d in range(2):
        fwd = d == 0
        ig_col_all = g[:, 2 * d:2 * d + 1]
        lf_col_all = _log_sigmoid(g[:, 2 * d + 1:2 * d + 2])
        ig_row_all = gt[2 * d:2 * d + 1, :]
        lf_row_all = _log_sigmoid(gt[2 * d + 1:2 * d + 2, :])
        if has_state:
            C = c0_ref[0, d, 0]
            n = n0_ref[0, d, 0]
            m = m0_ref[0, d, 0][:, 0:1]
        else:
            C = jnp.zeros((dk, v_ref.shape[1]), F32)
            n = jnp.zeros((1, dk), F32)
            m = jnp.zeros((1, 1), F32)
        order = range(nchunks) if fwd else range(nchunks - 1, -1, -1)
        for ci in order:
            lo, hi = ci * chunk, (ci + 1) * chunk
            h, C, n, m = _mlstm_chunk(
                q_ref[lo:hi, :], k_ref[lo:hi, :], v_ref[lo:hi, :],
                lf_col_all[lo:hi], ig_col_all[lo:hi], lf_row_all[:, lo:hi], ig_row_all[:, lo:hi],
                C, n, m, fwd, k_scale)
            if fwd:
                hacc[lo:hi, :] = h
            else:
                hacc[lo:hi, :] += h
        if emit_state:
            c_out[0, d, 0] = C
            n_out[0, d, 0] = n
            m_out[0, d, 0] = jnp.broadcast_to(m, (1, LANES))
    hh = hacc[...]
    hn = hh * lax.rsqrt(jnp.mean(hh * hh, axis=-1, keepdims=True) + EPS) * ng_ref[0]
    y_ref[...] = (jax.nn.sigmoid(o_ref[...]) * hn).astype(y_ref.dtype)


def _mlstm(proj, cols, row0, nb, seq, gate_bias, norm_g, state, emit_state, y_prev=None):
    heads = ML_HEADS
    dk = (cols["ml_k"] - cols["ml_q"]) // heads
    chunk = min(ML_CHUNK, seq)
    rb = row0 // seq

    def col_spec(off, width):
        return pl.BlockSpec((seq, width), lambda b, h, off=off, width=width: (rb + b, off // width + h))

    in_specs = [col_spec(cols["ml_q"], dk), col_spec(cols["ml_k"], dk), col_spec(cols["ml_v"], dk),
                col_spec(cols["ml_o"], dk), col_spec(cols["ml_g"], LANES),
                pl.BlockSpec((1, 1, LANES), lambda b, h: (h, 0, 0)),
                pl.BlockSpec((1, 1, dk), lambda b, h: (h, 0, 0))]
    args = [proj, proj, proj, proj, proj, gate_bias, norm_g.reshape(heads, 1, dk)]
    has_state = state is not None
    if has_state:
        c0, n0, m0 = state
        in_specs += [pl.BlockSpec((1, 2, 1, dk, dk), lambda b, h: (b, 0, h, 0, 0)),
                     pl.BlockSpec((1, 2, 1, 1, dk), lambda b, h: (b, 0, h, 0, 0)),
                     pl.BlockSpec((1, 2, 1, 1, LANES), lambda b, h: (b, 0, h, 0, 0))]
        args += [c0, n0.reshape(nb, 2, heads, 1, dk),
                 jnp.broadcast_to(m0[..., None, None], (nb, 2, heads, 1, LANES))]
    y_shape, aliases = _shared_rows_out(proj.shape[0], heads * dk, y_prev, in_specs, args)
    n_prev = len(aliases)
    out_shape = [y_shape]
    out_specs = [pl.BlockSpec((seq, dk), lambda b, h: (rb + b, h))]
    if emit_state:
        layer, depth, state_prev = emit_state
        out_shape += [jax.ShapeDtypeStruct((nb, depth, 2, heads, dk, dk), F32),
                      jax.ShapeDtypeStruct((nb, depth, 2, heads, 1, dk), F32),
                      jax.ShapeDtypeStruct((nb, depth, 2, heads, 1, LANES), F32)]
        out_specs += [pl.BlockSpec((1, None, 2, 1, dk, dk), lambda b, h: (b, layer, 0, h, 0, 0)),
                      pl.BlockSpec((1, None, 2, 1, 1, dk), lambda b, h: (b, layer, 0, h, 0, 0)),
                      pl.BlockSpec((1, None, 2, 1, 1, LANES), lambda b, h: (b, layer, 0, h, 0, 0))]
        if state_prev is not None:
            for k, prev in enumerate(state_prev):
                in_specs.append(pl.BlockSpec(memory_space=pl.ANY))
                args.append(prev)
                aliases[len(args) - 1] = 1 + k
            n_prev += len(state_prev)
    outs = pl.pallas_call(
        functools.partial(_mlstm_kernel, seq=seq, chunk=chunk, has_state=has_state,
                          has_prev=n_prev, emit_state=bool(emit_state), k_scale=float(dk) ** -0.5),
        out_shape=tuple(out_shape),
        grid=(nb, heads),
        in_specs=in_specs,
        out_specs=tuple(out_specs),
        input_output_aliases=aliases,
        scratch_shapes=[pltpu.VMEM((seq, dk), F32)],
        compiler_params=_cparams("parallel", "parallel"),
        name="mlstm",
    )(*args)
    if emit_state:
        return outs[0], tuple(outs[1:])
    return outs[0], None


def _dft_mats(seq):
    k = np.arange(seq, dtype=np.int64)
    ang = np.pi * ((k[:, None] * k[None, :]) % (2 * seq)).astype(np.float64) / seq
    cos, sin = np.cos(ang), np.sin(ang)
    alt = np.where(k % 2 == 0, 1.0, -1.0)
    fwd_b = -sin
    fwd_b[0, :] = alt
    fwd = np.concatenate([cos, fwd_b], axis=0)
    inv_a = cos.T / seq
    inv_a[:, 0] = 0.5 / seq
    inv_b = -sin.T / seq
    inv_b[:, 0] = alt * 0.5 / seq
    inv = np.concatenate([inv_a, inv_b], axis=1)
    return fwd.astype(np.float32), inv.astype(np.float32)


def _hy_features(seq):
    pos = np.arange(seq, dtype=np.float64)
    t = pos / (seq - 1)
    bands = np.arange(1, HY_BANDS + 1, dtype=np.float64)
    ang = (2.0 * np.pi / seq) * pos[:, None] * bands[None, :]
    feat = np.concatenate([t[:, None], np.cos(ang), np.sin(ang)], axis=-1)
    feat = np.pad(feat, ((0, 0), (0, LANES - feat.shape[1])))
    return feat.astype(np.float32), t.astype(np.float32)[:, None]


def _dot_split(w_hi, w_lo, g):
    g_hi = g.astype(BF16)
    g_lo = (g - g_hi.astype(F32)).astype(BF16)
    return (jnp.dot(w_hi, g_hi, preferred_element_type=F32)
            + (jnp.dot(w_hi, g_lo, preferred_element_type=F32) + jnp.dot(w_lo, g_hi, preferred_element_type=F32)))


def _hyfilt_kernel(feat_ref, t_ref, w1_ref, b1_ref, w2_ref, b2_ref, w3f_ref, w3b_ref, decf_ref, decb_ref,
                   fwd_ref, fwd_lo_ref, p_ref, fi_ref, s_ref):
    hp = lax.Precision.HIGHEST
    h = jnp.sin(jnp.dot(feat_ref[...], w1_ref[...], precision=hp, preferred_element_type=F32) + b1_ref[...])
    h = jnp.sin(jnp.dot(h, w2_ref[...], precision=hp, preferred_element_type=F32) + b2_ref[...])
    t = t_ref[...]
    seq = t.shape[0]
    row = lax.broadcasted_iota(jnp.int32, (seq, 1), 0)
    hf = jnp.dot(h, w3f_ref[...], precision=hp, preferred_element_type=F32) * jnp.exp(-t * decf_ref[0, 0])
    hb = jnp.dot(h, w3b_ref[...], precision=hp, preferred_element_type=F32) * jnp.exp(-t * decb_ref[0, 0])
    hb = jnp.where(row == 0, 0.0, hb)
    nrm = lax.rsqrt(jnp.sum(hf * hf, axis=0, keepdims=True) + jnp.sum(hb * hb, axis=0, keepdims=True) + EPS)
    gp = (hf + hb) * nrm
    gm = (hf - hb) * nrm
    fa = _dot_split(fwd_ref[0:seq, :], fwd_lo_ref[0:seq, :], gp)
    fb = _dot_split(fwd_ref[seq:2 * seq, :], fwd_lo_ref[seq:2 * seq, :], gm)
    alt = jnp.where(row % 2 == 0, 1.0, -1.0)
    f_nyq = jnp.sum(alt * gp, axis=0, keepdims=True)
    p_ref[0] = fa
    fi_ref[0] = jnp.where(row == 0, 0.0, fb)
    s_ref[0] = jnp.where(row == 0, f_nyq, fa)


def _hyena_filters(seq, w1, b1, w2, b2, w3, decay, fwd_hi, fwd_lo):
    feat_np, t_np = _hy_features(seq)
    nfeat, ffn = w1.shape
    width = decay.shape[-1]
    pf = LANES - ffn
    w1p = jnp.pad(w1, ((0, LANES - nfeat), (0, pf)))
    w2p = jnp.pad(w2, ((0, pf), (0, pf)))
    w3p = jnp.pad(w3, ((0, pf), (0, 0)))
    b1p = jnp.pad(b1, (0, pf)).reshape(1, LANES)
    b2p = jnp.pad(b2, (0, pf)).reshape(1, LANES)
    ct = _tile(width, 256)
    nct = width // ct
    dec = decay.reshape(HY_ORDER * 2, 1, width)
    full = lambda shape: pl.BlockSpec(shape, lambda o, j: (0,) * len(shape))
    out_sd = jax.ShapeDtypeStruct((HY_ORDER, seq, width), F32)
    out_spec = pl.BlockSpec((1, seq, ct), lambda o, j: (o, 0, j))
    return pl.pallas_call(
        _hyfilt_kernel,
        out_shape=(out_sd, out_sd, out_sd),
        grid=(HY_ORDER, nct),
        in_specs=[full(feat_np.shape), full(t_np.shape), full(w1p.shape), full(b1p.shape),
                  full(w2p.shape), full(b2p.shape),
                  pl.BlockSpec((LANES, ct), lambda o, j: (0, o * 2 * nct + j)),
                  pl.BlockSpec((LANES, ct), lambda o, j: (0, (o * 2 + 1) * nct + j)),
                  pl.BlockSpec((1, 1, ct), lambda o, j: (o * 2, 0, j)),
                  pl.BlockSpec((1, 1, ct), lambda o, j: (o * 2 + 1, 0, j)),
                  full(fwd_hi.shape), full(fwd_lo.shape)],
        out_specs=(out_spec, out_spec, out_spec),
        compiler_params=_cparams("parallel", "parallel"),
        name="hyena_filters",
    )(jnp.asarray(feat_np), jnp.asarray(t_np), w1p, b1p, w2p, b2p, w3p, w3p, dec, dec, fwd_hi, fwd_lo)


def _short_conv(u, w):
    seq = u.shape[0]
    row = lax.broadcasted_iota(jnp.int32, (seq, 1), 0)
    prev = jnp.where(row == 0, 0.0, pltpu.roll(u, 1, 0))
    nxt = jnp.where(row == seq - 1, 0.0, pltpu.roll(u, seq - 1, 0))
    return prev * w[0:1, :] + u * w[1:2, :] + nxt * w[2:3, :]


def _hyena_kernel(uv_ref, u1_ref, u2_ref, cwv_ref, cw1_ref, cw2_ref, fwd_ref, inv_ref,
                  p_ref, fi_ref, s_ref, skip_ref, *rest):
    y_ref = rest[-1]
    seq = uv_ref.shape[0]
    z = _short_conv(uv_ref[...], cwv_ref[...])
    gates = (_short_conv(u1_ref[...], cw1_ref[...]), _short_conv(u2_ref[...], cw2_ref[...]))
    for order in range(HY_ORDER):
        zf = jnp.dot(fwd_ref[...], z.astype(BF16), preferred_element_type=F32)
        a, b = zf[:seq], zf[seq:]
        p, fi, s = p_ref[order], fi_ref[order], s_ref[order]
        ya = a * p - b * fi
        yb = a * fi + b * s
        conv = (jnp.dot(inv_ref[:, :seq], ya.astype(BF16), preferred_element_type=F32)
                + jnp.dot(inv_ref[:, seq:], yb.astype(BF16), preferred_element_type=F32))
        z = gates[order] * (conv + skip_ref[order:order + 1, :] * z)
    y_ref[...] = z.astype(y_ref.dtype)


def _hyena_channel_tile(seq, width):
    const = 2 * 2 * (2 * seq * seq * 2)
    per_channel = 20 * seq * 4
    fit = (VMEM_LIMIT_BYTES * 2 // 3 - const) // per_channel
    ct = LANES
    while ct * 2 <= min(fit, width):
        ct *= 2
    return _tile(width, ct)


def _hyena(proj, cols, row0, nb, seq, conv_w, filt, skip, fwd_bf, inv_bf, y_prev=None):
    width = skip.shape[-1]
    ct = _hyena_channel_tile(seq, width)
    nct = width // ct
    rb = row0 // seq
    off = cols["hy"]
    p_arr, fi_arr, s_arr = filt

    def u_spec(part):
        return pl.BlockSpec((seq, ct), lambda j, b, part=part: (rb + b, off // ct + part * nct + j))

    def cw_spec(part):
        return pl.BlockSpec((3, ct), lambda j, b, part=part: (0, part * nct + j))

    full = lambda shape: pl.BlockSpec(shape, lambda j, b: (0,) * len(shape))
    f_spec = pl.BlockSpec((HY_ORDER, seq, ct), lambda j, b: (0, 0, j))
    in_specs = [u_spec(0), u_spec(1), u_spec(2), cw_spec(0), cw_spec(1), cw_spec(2),
                full(fwd_bf.shape), full(inv_bf.shape), f_spec, f_spec, f_spec,
                pl.BlockSpec((HY_ORDER, ct), lambda j, b: (0, j))]
    args = [proj, proj, proj, conv_w, conv_w, conv_w, fwd_bf, inv_bf, p_arr, fi_arr, s_arr, skip]
    y_shape, aliases = _shared_rows_out(proj.shape[0], width, y_prev, in_specs, args)
    return pl.pallas_call(
        _hyena_kernel,
        out_shape=y_shape,
        grid=(nct, nb),
        in_specs=in_specs,
        out_specs=pl.BlockSpec((seq, ct), lambda j, b: (rb + b, j)),
        input_output_aliases=aliases,
        compiler_params=_cparams("parallel", "parallel"),
        name="hyena",
    )(*args)


def _rope_tables(seq, head_dim):
    nfreq = head_dim // 4
    rows = seq // GRID_W
    row = np.repeat(np.arange(rows, dtype=np.float64), GRID_W)
    col = np.tile(np.arange(GRID_W, dtype=np.float64), rows)
    inv = (ROPE_BASE ** (-2.0 * np.arange(nfreq, dtype=np.float32) / (2 * nfreq))).astype(np.float64)
    ar, ac = row[:, None] * inv, col[:, None] * inv
    cos = np.concatenate([np.cos(ar), np.cos(ar), np.cos(ac), np.cos(ac)], axis=1)
    sin = np.concatenate([-np.sin(ar), np.sin(ar), -np.sin(ac), np.sin(ac)], axis=1)
    return cos.astype(np.float32), sin.astype(np.float32)


def _rope(x, cos, sin):
    hd = x.shape[1]
    q = hd // 4
    lane = lax.broadcasted_iota(jnp.int32, x.shape, 1)
    first = (lane % (2 * q)) < q
    partner = jnp.where(first, pltpu.roll(x, hd - q, 1), pltpu.roll(x, q, 1))
    return x * cos + partner * sin


def _attn_kernel(*refs, rope, has_ctx, has_prev, emit_kv, groups, bq, scale):
    q_ref, k_ref, v_ref, qg_ref, kg_ref = refs[:5]
    pos = 5
    if rope:
        cos_ref, sin_ref = refs[pos:pos + 2]
        pos += 2
    if has_ctx:
        kc_ref, vc_ref = refs[pos:pos + 2]
        pos += 2
    pos += has_prev
    y_ref = refs[pos]
    pos += 1
    if emit_kv:
        ko_ref, vo_ref = refs[pos:pos + 2]

    seq, hd = k_ref.shape
    k = k_ref[...]
    kn = k * lax.rsqrt(jnp.mean(k * k, axis=-1, keepdims=True) + EPS) * kg_ref[...]
    v = v_ref[...]
    if emit_kv:
        ko_ref[...] = kn
        vo_ref[...] = v
    if rope:
        kn = _rope(kn, cos_ref[...], sin_ref[...])
    kb, vb = kn.astype(BF16), v.astype(BF16)
    if has_ctx:
        kcb, vcb = kc_ref[0, 0].astype(BF16), vc_ref[0, 0].astype(BF16)
    for g in range(groups):
        for qi in range(seq // bq):
            lo, hi = qi * bq, (qi + 1) * bq
            q = q_ref[lo:hi, g * hd:(g + 1) * hd]
            qn = q * lax.rsqrt(jnp.mean(q * q, axis=-1, keepdims=True) + EPS) * qg_ref[...]
            if rope:
                qn = _rope(qn, cos_ref[lo:hi, :], sin_ref[lo:hi, :])
            qb = (qn * (scale * LOG2E)).astype(BF16)
            s1 = _nt_dot(qb, kb)
            mx = jnp.max(s1, axis=-1, keepdims=True)
            if has_ctx:
                s2 = _nt_dot(qb, kcb)
                mx = jnp.maximum(mx, jnp.max(s2, axis=-1, keepdims=True))
            p1 = jnp.exp2(s1 - mx)
            den = jnp.sum(p1, axis=-1, keepdims=True)
            o = jnp.dot(p1.astype(BF16), vb, preferred_element_type=F32)
            if has_ctx:
                p2 = jnp.exp2(s2 - mx)
                den = den + jnp.sum(p2, axis=-1, keepdims=True)
                o = o + jnp.dot(p2.astype(BF16), vcb, preferred_element_type=F32)
            y_ref[lo:hi, g * hd:(g + 1) * hd] = (o / den).astype(y_ref.dtype)


def _attention(proj, cols, row0, nb, seq, q_g, k_g, ctx, layer, rope, emit_kv, y_prev=None):
    hd = (cols["at_v"] - cols["at_k"]) // ATT_KV_HEADS
    groups = ATT_HEADS // ATT_KV_HEADS
    gw = groups * hd
    rb = row0 // seq
    in_specs = [pl.BlockSpec((seq, gw), lambda b, h: (rb + b, cols["at_q"] // gw + h)),
                pl.BlockSpec((seq, hd), lambda b, h: (rb + b, cols["at_k"] // hd + h)),
                pl.BlockSpec((seq, hd), lambda b, h: (rb + b, cols["at_v"] // hd + h)),
                pl.BlockSpec((1, hd), lambda b, h: (0, 0)),
                pl.BlockSpec((1, hd), lambda b, h: (0, 0))]
    args = [proj, proj, proj, q_g.reshape(1, hd), k_g.reshape(1, hd)]
    if rope:
        cos_np, sin_np = _rope_tables(seq, hd)
        in_specs += [pl.BlockSpec((seq, hd), lambda b, h: (0, 0))] * 2
        args += [jnp.asarray(cos_np), jnp.asarray(sin_np)]
    has_ctx = ctx is not None
    if has_ctx:
        ck, cv = ctx
        past = ck.shape[2]
        in_specs += [pl.BlockSpec((1, 1, past, hd), lambda b, h: (b, layer, 0, h))] * 2
        args += [ck, cv]
    y_shape, aliases = _shared_rows_out(proj.shape[0], ATT_HEADS * hd, y_prev, in_specs, args)
    n_prev = len(aliases)
    out_shape = [y_shape]
    out_specs = [pl.BlockSpec((seq, gw), lambda b, h: (rb + b, h))]
    if emit_kv:
        depth, kv_prev = emit_kv
        out_shape += [jax.ShapeDtypeStruct((nb, depth, seq, ATT_KV_HEADS * hd), F32)] * 2
        out_specs += [pl.BlockSpec((None, None, seq, hd), lambda b, h: (b, layer, 0, h))] * 2
        if kv_prev is not None:
            for k, prev in enumerate(kv_prev):
                in_specs.append(pl.BlockSpec(memory_space=pl.ANY))
                args.append(prev)
                aliases[len(args) - 1] = 1 + k
            n_prev += len(kv_prev)
    outs = pl.pallas_call(
        functools.partial(_attn_kernel, rope=rope, has_ctx=has_ctx, has_prev=n_prev,
                          emit_kv=bool(emit_kv), groups=groups, bq=min(seq, ATT_BLOCK_Q), scale=float(hd) ** -0.5),
        out_shape=tuple(out_shape),
        grid=(nb, ATT_KV_HEADS),
        in_specs=in_specs,
        out_specs=tuple(out_specs),
        input_output_aliases=aliases,
        compiler_params=_cparams("parallel", "parallel"),
        name="attention",
    )(*args)
    if emit_kv:
        return outs
    return outs[0], None, None


def _branch_kernel(y0_ref, y1_ref, y2_ref, g0_ref, g1_ref, g2_ref, w_ref, o_ref):
    acc = None
    for n, (y_ref, g_ref) in enumerate(((y0_ref, g0_ref), (y1_ref, g1_ref), (y2_ref, g2_ref))):
        p = jnp.dot(y_ref[...], w_ref[n], preferred_element_type=F32)
        term = jax.nn.sigmoid(g_ref[...]) * p
        acc = term if acc is None else acc + term
    o_ref[...] = acc.astype(o_ref.dtype)


def _branch_merge(ys, proj, cols, w_branch, layer, tm, tn):
    t, bw = ys[0].shape
    d = w_branch.shape[3]
    goff = cols["br_g"]
    y_spec = pl.BlockSpec((tm, bw), lambda i, j: (i, 0))

    def g_spec(n):
        return pl.BlockSpec((tm, tn), lambda i, j, n=n: (i, (goff + n * d) // tn + j))

    return pl.pallas_call(
        _branch_kernel,
        out_shape=jax.ShapeDtypeStruct((t, d), BF16),
        grid=(t // tm, d // tn),
        in_specs=[y_spec, y_spec, y_spec, g_spec(0), g_spec(1), g_spec(2),
                  pl.BlockSpec((None, N_BRANCH, bw, tn), lambda i, j: (layer, 0, 0, j))],
        out_specs=pl.BlockSpec((tm, tn), lambda i, j: (i, j)),
        compiler_params=_cparams("parallel", "arbitrary"),
        name="branch_merge",
    )(*ys, proj, proj, proj, w_branch)


def _proj_residual_kernel(a_ref, w_ref, x_ref, mod_ref, o_ref, *, gate_idx):
    y = jnp.dot(a_ref[...], w_ref[...], preferred_element_type=F32)
    o_ref[...] = x_ref[...] + mod_ref[0, gate_idx:gate_idx + 1, :] * y


def _proj_residual(a, w, layer, x, mod, seq_row, *, gate_idx, tm, tn):
    t, k = a.shape
    d = w.shape[2]
    return pl.pallas_call(
        functools.partial(_proj_residual_kernel, gate_idx=gate_idx),
        out_shape=jax.ShapeDtypeStruct((t, d), F32),
        grid=(t // tm, d // tn),
        in_specs=[pl.BlockSpec((tm, k), lambda i, j: (i, 0)),
                  pl.BlockSpec((None, k, tn), lambda i, j: (layer, 0, j)),
                  pl.BlockSpec((tm, tn), lambda i, j: (i, j)),
                  pl.BlockSpec((1, 6, tn), lambda i, j: (seq_row(i), 0, j))],
        out_specs=pl.BlockSpec((tm, tn), lambda i, j: (i, j)),
        compiler_params=_cparams("parallel", "arbitrary"),
        name="proj_residual",
    )(a, w, x, mod)


def _topk_rows(s, k, exact_ties):
    rows = s.shape[0]
    iota = lax.broadcasted_iota(jnp.int32, s.shape, 0).astype(F32)
    rank = jnp.full(s.shape, float(k), F32)
    vals = []
    for r in range(k):
        mx = jnp.max(s, axis=0, keepdims=True)
        sel = s == mx
        if exact_ties:
            sel = iota == jnp.min(jnp.where(sel, iota, float(rows)), axis=0, keepdims=True)
        rank = jnp.where(sel, float(r), rank)
        vals.append(mx)
        s = jnp.where(sel, -jnp.inf, s)
    count = jnp.sum(jnp.where(rank < float(k), 1.0, 0.0), axis=0, keepdims=True)
    return jnp.concatenate(vals, axis=0), rank, count


def _route_head(s1, s2, exact_ties):
    k = PEER_TOPK
    sv1, rk1, c1 = _topk_rows(s1, k, exact_ties)
    sv2, rk2, c2 = _topk_rows(s2, k, exact_ties)
    row8 = lax.broadcasted_iota(jnp.int32, (8, s1.shape[1]), 0)
    groups = [sv1[0:1, :] + sv2, sv1[1:2, :] + sv2[0:8, :]]
    for p in range(2, 8):
        groups.append(jnp.where(row8 < k // (p + 1), sv1[p:p + 1, :] + sv2[0:8, :], -jnp.inf))
    groups.append(sv1[8:16, :] + sv2[0:1, :])
    fv, rkc, c3 = _topk_rows(jnp.concatenate(groups, axis=0), k, exact_ties)
    z = jnp.sum(jnp.exp(fv - fv[0:1, :]), axis=0, keepdims=True)
    sel = jnp.where(rkc < float(k), 1.0, 0.0)
    cnt_i = jnp.zeros_like(s1)
    starts = [0, 16] + [24 + 8 * (p - 2) for p in range(2, 8)]
    sizes = [16, 8] + [8] * 6
    for p in range(k):
        if p < 8:
            cnt_p = jnp.sum(sel[starts[p]:starts[p] + sizes[p], :], axis=0, keepdims=True)
        else:
            cnt_p = sel[72 + p - 8:72 + p - 7, :]
        cnt_i = cnt_i + jnp.where(rk1 == float(p), cnt_p, 0.0)
    u1 = jnp.exp(s1 - sv1[0:1, :]) / z
    u2 = jnp.exp(s2 - sv2[0:1, :])
    return u1, cnt_i, u2, rk2, jnp.max(jnp.maximum(jnp.maximum(c1, c2), c3))


def _peer_route_kernel(q_ref, keys_ref, u1_ref, cnt_ref, u2_ref, rk2_ref):
    dh = keys_ref.shape[3]
    s1 = _nt_dot(keys_ref[0, 0], q_ref[:, 0:dh])
    s2 = _nt_dot(keys_ref[0, 1], q_ref[:, dh:2 * dh])

    def emit(u1, cnt, u2, rk2):
        u1_ref[0] = u1
        cnt_ref[0] = cnt
        u2_ref[0] = u2.astype(u2_ref.dtype)
        rk2_ref[0] = rk2.astype(rk2_ref.dtype)

    *fast, most = _route_head(s1, s2, exact_ties=False)
    emit(*fast)

    @pl.when(most > float(PEER_TOPK))
    def _():
        emit(*_route_head(s1, s2, exact_ties=True)[:4])


def _peer_route(q, keys_bf, tr):
    t = q.shape[0]
    heads, _, nkeys, dh = keys_bf.shape
    sd = jax.ShapeDtypeStruct((heads, nkeys, t), F32)
    sd16 = jax.ShapeDtypeStruct((heads, nkeys, t), BF16)
    spec = pl.BlockSpec((1, nkeys, tr), lambda i, h: (h, 0, i))
    return pl.pallas_call(
        _peer_route_kernel,
        out_shape=(sd, sd, sd16, sd16),
        grid=(t // tr, heads),
        in_specs=[pl.BlockSpec((tr, 2 * dh), lambda i, h: (i, h)),
                  pl.BlockSpec((1, 2, nkeys, dh), lambda i, h: (h, 0, 0, 0))],
        out_specs=(spec, spec, spec, spec),
        compiler_params=_cparams("parallel", "parallel"),
        name="peer_route",
    )(q, keys_bf)


def _peer_dense_kernel(ht_ref, u_ref, vt_ref, u1_ref, cnt_ref, u2_ref, rk2_ref, o_ref, w_ref, *, sub, group):
    e = pl.program_id(1)

    @pl.when(e == 0)
    def _():
        o_ref[...] = jnp.zeros_like(o_ref)

    te = u_ref.shape[0]
    nkeys, tm = u2_ref.shape[1:]
    ktiles = nkeys // BF16_ROWS
    for i in range(te // nkeys):
        w = None
        for h in range(PEER_HEADS):
            u1 = jnp.broadcast_to(u1_ref[h, i:i + 1, :], (BF16_ROWS, tm)).astype(BF16)[None]
            cn = jnp.broadcast_to(cnt_ref[h, i:i + 1, :], (BF16_ROWS, tm)).astype(BF16)[None]
            rk = rk2_ref[h].reshape(ktiles, BF16_ROWS, tm)
            u2 = u2_ref[h].reshape(ktiles, BF16_ROWS, tm)
            term = u1 * jnp.where(rk < cn, u2, jnp.zeros((), BF16))
            w = term if w is None else w + term
        w_ref[i * nkeys:(i + 1) * nkeys, :] = w.reshape(nkeys, tm)

    ht = ht_ref[...]
    nsub = te // sub
    pre = [jnp.dot(u_ref[s * sub:(s + 1) * sub, :], ht, preferred_element_type=F32) for s in range(nsub)]
    pending = []
    for s in range(nsub):
        act = jax.nn.gelu(pre[s].astype(BF16), approximate=True)
        pending.append(w_ref[s * sub:(s + 1) * sub, :] * act)
        if len(pending) == group or s + 1 == nsub:
            lo = (s + 1 - len(pending)) * sub
            wa = pending[0] if len(pending) == 1 else jnp.concatenate(pending, axis=0)
            o_ref[...] += jnp.dot(vt_ref[:, lo:(s + 1) * sub], wa, preferred_element_type=F32)
            pending = []


def _peer_dense(h2t, u_bf, vt_bf, layer, route, *, tm, te, sub, group):
    d, t = h2t.shape
    ne = u_bf.shape[1]
    heads, nkeys, _ = route[0].shape
    once = pl.Buffered(1)
    r_spec = pl.BlockSpec((heads, nkeys, tm), lambda i, e: (0, 0, i), pipeline_mode=once)
    k_spec = pl.BlockSpec((heads, te // nkeys, tm), lambda i, e: (0, e, i))
    return pl.pallas_call(
        functools.partial(_peer_dense_kernel, sub=sub, group=group),
        out_shape=jax.ShapeDtypeStruct((d, t), F32),
        grid=(t // tm, ne // te),
        scratch_shapes=[pltpu.VMEM((te, tm), BF16)],
        in_specs=[pl.BlockSpec((d, tm), lambda i, e: (0, i), pipeline_mode=once),
                  pl.BlockSpec((None, te, d), lambda i, e: (layer, e, 0)),
                  pl.BlockSpec((None, d, te), lambda i, e: (layer, 0, e)),
                  k_spec, k_spec, r_spec, r_spec],
        out_specs=pl.BlockSpec((d, tm), lambda i, e: (0, i)),
        compiler_params=_cparams("parallel", "arbitrary"),
        name="peer_dense",
    )(h2t, u_bf, vt_bf, *route)


def _peer_residual_kernel(x_ref, yt_ref, mod_ref, fg_ref, o_ref, *, final_norm):
    y = x_ref[...] + mod_ref[0, 5:6, :] * yt_ref[...].T
    if final_norm:
        y = y * lax.rsqrt(jnp.mean(y * y, axis=-1, keepdims=True) + EPS) * fg_ref[...]
    o_ref[...] = y


def _peer_residual(x, yt, mod, final_g, seq_row, *, final_norm, tm, first_tile=0, ntiles=None):
    t, d = x.shape
    ntiles = t // tm if ntiles is None else ntiles
    return pl.pallas_call(
        functools.partial(_peer_residual_kernel, final_norm=final_norm),
        out_shape=jax.ShapeDtypeStruct((ntiles * tm, d), F32),
        grid=(ntiles,),
        in_specs=[pl.BlockSpec((tm, d), lambda i: (first_tile + i, 0)),
                  pl.BlockSpec((d, tm), lambda i: (0, first_tile + i)),
                  pl.BlockSpec((1, 6, d), lambda i: (seq_row(first_tile + i), 0, 0)),
                  pl.BlockSpec((1, d), lambda i: (0, 0))],
        out_specs=pl.BlockSpec((tm, d), lambda i: (i, 0)),
        compiler_params=_cparams("parallel"),
        name="peer_residual",
    )(x, yt, mod, final_g.reshape(1, d))


def _in_layout(bw, kvw, d):
    sizes = (("ml_q", bw), ("ml_k", bw), ("ml_v", bw), ("ml_o", bw), ("hy", 3 * bw),
             ("at_q", bw), ("at_k", kvw), ("at_v", kvw), ("br_g", N_BRANCH * d), ("ml_g", GATE_PAD))
    cols, off = {}, 0
    for name, size in sizes:
        cols[name] = off
        off += size
    return cols, off


def _pack_w_in(w_in, bw):
    depth, d, _ = w_in.shape
    ngate = 4 * ML_HEADS
    gates = w_in[:, :, 4 * bw:4 * bw + ngate].reshape(depth, d, 4, ML_HEADS)
    gates = jnp.swapaxes(gates, 2, 3)
    gates = jnp.pad(gates, ((0, 0), (0, 0), (0, 0), (0, GATE_PAD // ML_HEADS - 4))).reshape(depth, d, GATE_PAD)
    return jnp.concatenate([w_in[:, :, :4 * bw], w_in[:, :, 4 * bw + ngate:], gates], axis=2).astype(BF16)


def kernel(x_prompt, x_sample, c, cache_k, cache_v, state_C, state_n, state_m, c_ctx, w_ada, b_ada, norm1_g, norm2_g, w_in, ml_gate_bias, ml_norm_g, hy_conv_w, hy_w1, hy_b1, hy_w2, hy_b2, hy_w3, hy_decay, hy_skip, q_norm_g, k_norm_g, w_branch, w_out, peer_wq, peer_keys, peer_u, peer_v, final_g):
    nbp, seq_p, d = x_prompt.shape
    nbs, seq_s, _ = x_sample.shape
    depth = w_ada.shape[0]
    bw = d // 2
    kvw = cache_k.shape[3] * cache_k.shape[4]
    tp, ts = nbp * seq_p, nbs * seq_s
    cols, ncols = _in_layout(bw, kvw, d)
    assert w_in.shape[2] == ncols - GATE_PAD + 4 * ML_HEADS

    tm = _tile(seq_s, TOKEN_TILE)
    tn = TOKEN_TILE
    assert tp % tm == 0 and PEER_TOPK == 16
    tiles_p, tiles_per_seq = tp // tm, seq_s // tm

    def seq_row(i):
        return jnp.where(i < tiles_p, 0, 1 + (i - tiles_p) // tiles_per_seq)

    x = jnp.concatenate([x_prompt.reshape(tp, d), x_sample.reshape(ts, d)], axis=0)
    cond = jnp.concatenate([c_ctx[None, :], c], axis=0)
    nrow = cond.shape[0]
    cond = jnp.pad(cond, ((0, (-nrow) % 8), (0, 0)))
    mod_all = _ada(cond, w_ada, b_ada).reshape(depth, cond.shape[0], 6, d)

    ctx_k = cache_k.reshape(nbs, depth, cache_k.shape[2], kvw)
    ctx_v = cache_v.reshape(nbs, depth, cache_v.shape[2], kvw)
    dft = {}
    for seq in {seq_p, seq_s}:
        fwd_np, inv_np = _dft_mats(seq)
        fwd_hi = jnp.asarray(fwd_np).astype(BF16)
        fwd_lo = (jnp.asarray(fwd_np) - fwd_hi.astype(F32)).astype(BF16)
        dft[seq] = (fwd_hi, fwd_lo, jnp.asarray(inv_np).astype(BF16))

    w_in_bf = _pack_w_in(w_in, bw)
    w_branch_bf, w_out_bf, wq_bf = w_branch.astype(BF16), w_out.astype(BF16), peer_wq.astype(BF16)
    u_bf = peer_u.astype(BF16)
    vt_bf = jnp.swapaxes(peer_v, 1, 2).astype(BF16)
    keys_bf = peer_keys.astype(BF16)

    new_state = new_kv = None
    for l in range(depth):
        mod = mod_all[l]
        proj = _modnorm_matmul(x, mod, norm1_g[l], w_in_bf, l, seq_row, shift_idx=0, out_dtype=F32,
                               tm=tm, tn=_tile(ncols, tn))[0]
        gb = ml_gate_bias[l].reshape(4, ML_HEADS).T
        gb = jnp.pad(gb, ((0, 0), (0, LANES - 4))).reshape(ML_HEADS, 1, LANES)

        y_ml = y_hy = y_at = None
        for (row0, nb, seq, is_ctx) in ((0, nbp, seq_p, True), (tp, nbs, seq_s, False)):
            state = None if is_ctx else (state_C[:, l], state_n[:, l], state_m[:, l])
            y_ml, st = _mlstm(proj, cols, row0, nb, seq, gb, ml_norm_g[l], state,
                              emit_state=(l, depth, new_state) if is_ctx else None, y_prev=y_ml)
            fwd_bf, fwd_lo, inv_bf = dft[seq]
            filt = _hyena_filters(seq, hy_w1[l], hy_b1[l], hy_w2[l], hy_b2[l], hy_w3[l], hy_decay[l], fwd_bf, fwd_lo)
            y_hy = _hyena(proj, cols, row0, nb, seq, hy_conv_w[l], filt, hy_skip[l], fwd_bf, inv_bf, y_prev=y_hy)
            y_at, k_l, v_l = _attention(proj, cols, row0, nb, seq, q_norm_g[l], k_norm_g[l],
                                        None if is_ctx else (ctx_k, ctx_v), l, rope=not is_ctx,
                                        emit_kv=(depth, new_kv) if is_ctx else None, y_prev=y_at)
            if is_ctx:
                new_state, new_kv = st, (k_l, v_l)
        mixed = _branch_merge((y_ml, y_hy, y_at), proj, cols, w_branch_bf, l, tm, tn // 2)
        x = _proj_residual(mixed, w_out_bf, l, x, mod, seq_row, gate_idx=2, tm=tm, tn=tn)

        q, _, h2t = _modnorm_matmul(x, mod, norm2_g[l], wq_bf, l, seq_row, shift_idx=3,
                                    out_dtype=BF16, tm=tm, tn=tn, emit_ht=True)
        route = _peer_route(q, keys_bf[l], _tile(tm, ROUTE_TILE))
        yt = _peer_dense(h2t, u_bf, vt_bf, l, route, tm=tm, te=EXPERT_TILE, sub=EXPERT_SUB,
                         group=EXPERT_GROUP)
        if l < depth - 1:
            x = _peer_residual(x, yt, mod, final_g, seq_row, final_norm=False, tm=tm)

    y_prompt = _peer_residual(x, yt, mod, final_g, seq_row, final_norm=True, tm=tm,
                              first_tile=0, ntiles=tiles_p).reshape(nbp, seq_p, d)
    y_sample = _peer_residual(x, yt, mod, final_g, seq_row, final_norm=True, tm=tm,
                              first_tile=tiles_p, ntiles=ts // tm).reshape(nbs, seq_s, d)
    kv_shape = (nbp, depth, seq_p, ATT_KV_HEADS, kvw // ATT_KV_HEADS)
    new_c, new_n, new_m = new_state
    return (y_prompt, y_sample, new_kv[0].reshape(kv_shape), new_kv[1].reshape(kv_shape),
            new_c, new_n[:, :, :, :, 0, :], new_m[:, :, :, :, 0, 0])
```

```python
import functools

import numpy as np
import jax
import jax.numpy as jnp
from jax import lax
from jax.experimental import pallas as pl
from jax.experimental.pallas import tpu as pltpu

F32 = jnp.float32
BF16 = jnp.bfloat16
EPS = 1e-6
LOG2E = 1.4426950408889634

GRID_W = 64
ML_HEADS = 4
ATT_HEADS = 8
ATT_KV_HEADS = 2
ROPE_BASE = 10000.0
HY_BANDS = 16
HY_ORDER = 2
N_BRANCH = 3
PEER_HEADS = 8
PEER_TOPK = 16

VMEM_LIMIT_BYTES = 60 * 1024 * 1024
LANES = 128
BF16_ROWS = 16
ML_CHUNK = 512
ATT_BLOCK_Q = 256
TOKEN_TILE = 1024
ROUTE_TILE = 512
EXPERT_TILE = 2048
EXPERT_SUB = 256
EXPERT_GROUP = 8
EXPERT_TOKEN_TILE = 512
GATE_PAD = 512


def _cparams(*sem):
    return pltpu.CompilerParams(dimension_semantics=sem, vmem_limit_bytes=VMEM_LIMIT_BYTES)


def _tile(n, pref):
    t = min(n, pref)
    while n % t:
        t //= 2
    return t


def _nt_dot(a, b):
    return lax.dot_general(a, b, (((1,), (1,)), ((), ())), preferred_element_type=F32)


def _ada_kernel(c_ref, w_ref, b_ref, o_ref):
    c = c_ref[...]
    a = c * jax.nn.sigmoid(c)
    o_ref[0] = jnp.dot(a.astype(BF16), w_ref[0].astype(BF16), preferred_element_type=F32) + b_ref[0]


def _ada(cond, w_ada, b_ada):
    depth, d, n = w_ada.shape
    r = cond.shape[0]
    tn = _tile(n, 1024)
    return pl.pallas_call(
        _ada_kernel,
        out_shape=jax.ShapeDtypeStruct((depth, r, n), F32),
        grid=(depth, n // tn),
        in_specs=[pl.BlockSpec((r, d), lambda l, j: (0, 0)),
                  pl.BlockSpec((1, d, tn), lambda l, j: (l, 0, j)),
                  pl.BlockSpec((1, 1, tn), lambda l, j: (l, 0, j))],
        out_specs=pl.BlockSpec((1, r, tn), lambda l, j: (l, 0, j)),
        compiler_params=_cparams("parallel", "parallel"),
        name="ada",
    )(cond, w_ada, b_ada.reshape(depth, 1, n))


def _modnorm_matmul_kernel(x_ref, mod_ref, g_ref, w_ref, o_ref, h_ref, *ht_ref, shift_idx):
    @pl.when(pl.program_id(1) == 0)
    def _():
        x = x_ref[...]
        y = x * lax.rsqrt(jnp.mean(x * x, axis=-1, keepdims=True) + EPS) * g_ref[...]
        sh = mod_ref[0, shift_idx:shift_idx + 1, :]
        sc = mod_ref[0, shift_idx + 1:shift_idx + 2, :]
        h = y * (1.0 + sc) + sh
        h_ref[...] = h.astype(BF16)
        if ht_ref:
            ht_ref[0][...] = h.T.astype(BF16)

    o_ref[...] = jnp.dot(h_ref[...], w_ref[...], preferred_element_type=F32).astype(o_ref.dtype)


def _modnorm_matmul(x, mod, g, w, layer, seq_row, *, shift_idx, out_dtype, tm, tn, emit_ht=False):
    t, d = x.shape
    n = w.shape[2]
    out_shape = [jax.ShapeDtypeStruct((t, n), out_dtype), jax.ShapeDtypeStruct((t, d), BF16)]
    out_specs = [pl.BlockSpec((tm, tn), lambda i, j: (i, j)), pl.BlockSpec((tm, d), lambda i, j: (i, 0))]
    if emit_ht:
        out_shape.append(jax.ShapeDtypeStruct((d, t), BF16))
        out_specs.append(pl.BlockSpec((d, tm), lambda i, j: (0, i)))
    return pl.pallas_call(
        functools.partial(_modnorm_matmul_kernel, shift_idx=shift_idx),
        out_shape=tuple(out_shape),
        grid=(t // tm, n // tn),
        in_specs=[pl.BlockSpec((tm, d), lambda i, j: (i, 0)),
                  pl.BlockSpec((1, 6, d), lambda i, j: (seq_row(i), 0, 0)),
                  pl.BlockSpec((1, d), lambda i, j: (0, 0)),
                  pl.BlockSpec((None, d, tn), lambda i, j: (layer, 0, j))],
        out_specs=tuple(out_specs),
        compiler_params=_cparams("parallel", "arbitrary"),
        name="modnorm_matmul",
    )(x, mod, g.reshape(1, d), w)


def _log_sigmoid(x):
    return jnp.minimum(x, 0.0) - jnp.log1p(jnp.exp(-jnp.abs(x)))


def _mlstm_chunk(qc, kc, vc, lf_col, ig_col, lf_row, ig_row, C, n, m, fwd, k_scale):
    tc = qc.shape[0]
    r = lax.broadcasted_iota(jnp.int32, (tc, tc), 0)
    c = lax.broadcasted_iota(jnp.int32, (tc, tc), 1)
    mask = (c <= r) if fwd else (c >= r)
    mask_t = (r <= c) if fwd else (r >= c)
    b_col = jnp.sum(jnp.where(mask, lf_row, 0.0), axis=1, keepdims=True)
    b_row = jnp.sum(jnp.where(mask_t, lf_col, 0.0), axis=0, keepdims=True)
    dmat = jnp.where(mask, b_col - b_row + ig_row, -jnp.inf)
    m_t = jnp.maximum(b_col + m, jnp.max(dmat, axis=1, keepdims=True))
    w_in = jnp.exp(dmat - m_t)
    w_prev = jnp.exp(b_col + m - m_t)
    qb, kb, vb = qc.astype(BF16), kc.astype(BF16), vc.astype(BF16)
    s = _nt_dot(qb, kb) * k_scale * w_in
    num = (jnp.dot(s.astype(BF16), vb, preferred_element_type=F32)
           + w_prev * jnp.dot(qb, C.astype(BF16), preferred_element_type=F32))
    den = jnp.sum(s, axis=1, keepdims=True) + w_prev * jnp.sum(qc * n, axis=1, keepdims=True)
    h = num / jnp.maximum(jnp.abs(den), jnp.exp(-m_t))
    b_last = jnp.sum(lf_row, axis=1, keepdims=True)
    lw_col = b_last - b_col + ig_col
    lw_row = b_last - b_row + ig_row
    m_new = jnp.maximum(b_last + m, jnp.max(lw_row, axis=1, keepdims=True))
    decay = jnp.exp(b_last + m - m_new)
    kw = kc * k_scale * jnp.exp(lw_col - m_new)
    C_new = decay * C + jnp.dot(kw.T.astype(BF16), vb, preferred_element_type=F32)
    n_new = decay * n + jnp.sum(kw, axis=0, keepdims=True)
    return h, C_new, n_new, m_new


def _shared_rows_out(total_rows, width, prev, in_specs, args):
    aliases = {}
    if prev is not None:
        in_specs.append(pl.BlockSpec(memory_space=pl.ANY))
        args.append(prev)
        aliases = {len(args) - 1: 0}
    return jax.ShapeDtypeStruct((total_rows, width), BF16), aliases


def _mlstm_kernel(*refs, seq, chunk, has_state, has_prev, emit_state, k_scale):
    q_ref, k_ref, v_ref, o_ref, g_ref, bias_ref, ng_ref = refs[:7]
    pos = 7
    if has_state:
        c0_ref, n0_ref, m0_ref = refs[pos:pos + 3]
        pos += 3
    pos += has_prev
    y_ref = refs[pos]
    pos += 1
    if emit_state:
        c_out, n_out, m_out = refs[pos:pos + 3]
        pos += 3
    hacc = refs[pos]

    g = g_ref[...] + bias_ref[0]
    gt = g.T
    dk = q_ref.shape[1]
    nchunks = seq // chunk
    for d in range(2):
        fwd = d == 0
        ig_col_all = g[:, 2 * d:2 * d + 1]
        lf_col_all = _log_sigmoid(g[:, 2 * d + 1:2 * d + 2])
        ig_row_all = gt[2 * d:2 * d + 1, :]
        lf_row_all = _log_sigmoid(gt[2 * d + 1:2 * d + 2, :])
        if has_state:
            C = c0_ref[0, d, 0]
            n = n0_ref[0, d, 0]
            m = m0_ref[0, d, 0][:, 0:1]
        else:
            C = jnp.zeros((dk, v_ref.shape[1]), F32)
            n = jnp.zeros((1, dk), F32)
            m = jnp.zeros((1, 1), F32)
        order = range(nchunks) if fwd else range(nchunks - 1, -1, -1)
        for ci in order:
            lo, hi = ci * chunk, (ci + 1) * chunk
            h, C, n, m = _mlstm_chunk(
                q_ref[lo:hi, :], k_ref[lo:hi, :], v_ref[lo:hi, :],
                lf_col_all[lo:hi], ig_col_all[lo:hi], lf_row_all[:, lo:hi], ig_row_all[:, lo:hi],
                C, n, m, fwd, k_scale)
            if fwd:
                hacc[lo:hi, :] = h
            else:
                hacc[lo:hi, :] += h
        if emit_state:
            c_out[0, d, 0] = C
            n_out[0, d, 0] = n
            m_out[0, d, 0] = jnp.broadcast_to(m, (1, LANES))
    hh = hacc[...]
    hn = hh * lax.rsqrt(jnp.mean(hh * hh, axis=-1, keepdims=True) + EPS) * ng_ref[0]
    y_ref[...] = (jax.nn.sigmoid(o_ref[...]) * hn).astype(y_ref.dtype)


def _mlstm(proj, cols, row0, nb, seq, gate_bias, norm_g, state, emit_state, y_prev=None):
    heads = ML_HEADS
    dk = (cols["ml_k"] - cols["ml_q"]) // heads
    chunk = min(ML_CHUNK, seq)
    rb = row0 // seq

    def col_spec(off, width):
        return pl.BlockSpec((seq, width), lambda b, h, off=off, width=width: (rb + b, off // width + h))

    in_specs = [col_spec(cols["ml_q"], dk), col_spec(cols["ml_k"], dk), col_spec(cols["ml_v"], dk),
                col_spec(cols["ml_o"], dk), col_spec(cols["ml_g"], LANES),
                pl.BlockSpec((1, 1, LANES), lambda b, h: (h, 0, 0)),
                pl.BlockSpec((1, 1, dk), lambda b, h: (h, 0, 0))]
    args = [proj, proj, proj, proj, proj, gate_bias, norm_g.reshape(heads, 1, dk)]
    has_state = state is not None
    if has_state:
        c0, n0, m0 = state
        in_specs += [pl.BlockSpec((1, 2, 1, dk, dk), lambda b, h: (b, 0, h, 0, 0)),
                     pl.BlockSpec((1, 2, 1, 1, dk), lambda b, h: (b, 0, h, 0, 0)),
                     pl.BlockSpec((1, 2, 1, 1, LANES), lambda b, h: (b, 0, h, 0, 0))]
        args += [c0, n0.reshape(nb, 2, heads, 1, dk),
                 jnp.broadcast_to(m0[..., None, None], (nb, 2, heads, 1, LANES))]
    y_shape, aliases = _shared_rows_out(proj.shape[0], heads * dk, y_prev, in_specs, args)
    n_prev = len(aliases)
    out_shape = [y_shape]
    out_specs = [pl.BlockSpec((seq, dk), lambda b, h: (rb + b, h))]
    if emit_state:
        layer, depth, state_prev = emit_state
        out_shape += [jax.ShapeDtypeStruct((nb, depth, 2, heads, dk, dk), F32),
                      jax.ShapeDtypeStruct((nb, depth, 2, heads, 1, dk), F32),
                      jax.ShapeDtypeStruct((nb, depth, 2, heads, 1, LANES), F32)]
        out_specs += [pl.BlockSpec((1, None, 2, 1, dk, dk), lambda b, h: (b, layer, 0, h, 0, 0)),
                      pl.BlockSpec((1, None, 2, 1, 1, dk), lambda b, h: (b, layer, 0, h, 0, 0)),
                      pl.BlockSpec((1, None, 2, 1, 1, LANES), lambda b, h: (b, layer, 0, h, 0, 0))]
        if state_prev is not None:
            for k, prev in enumerate(state_prev):
                in_specs.append(pl.BlockSpec(memory_space=pl.ANY))
                args.append(prev)
                aliases[len(args) - 1] = 1 + k
            n_prev += len(state_prev)
    outs = pl.pallas_call(
        functools.partial(_mlstm_kernel, seq=seq, chunk=chunk, has_state=has_state,
                          has_prev=n_prev, emit_state=bool(emit_state), k_scale=float(dk) ** -0.5),
        out_shape=tuple(out_shape),
        grid=(nb, heads),
        in_specs=in_specs,
        out_specs=tuple(out_specs),
        input_output_aliases=aliases,
        scratch_shapes=[pltpu.VMEM((seq, dk), F32)],
        compiler_params=_cparams("parallel", "parallel"),
        name="mlstm",
    )(*args)
    if emit_state:
        return outs[0], tuple(outs[1:])
    return outs[0], None


def _dft_mats(seq):
    k = np.arange(seq, dtype=np.int64)
    ang = np.pi * ((k[:, None] * k[None, :]) % (2 * seq)).astype(np.float64) / seq
    cos, sin = np.cos(ang), np.sin(ang)
    alt = np.where(k % 2 == 0, 1.0, -1.0)
    fwd_b = -sin
    fwd_b[0, :] = alt
    fwd = np.concatenate([cos, fwd_b], axis=0)
    inv_a = cos.T / seq
    inv_a[:, 0] = 0.5 / seq
    inv_b = -sin.T / seq
    inv_b[:, 0] = alt * 0.5 / seq
    inv = np.concatenate([inv_a, inv_b], axis=1)
    return fwd.astype(np.float32), inv.astype(np.float32)


def _hy_features(seq):
    pos = np.arange(seq, dtype=np.float64)
    t = pos / (seq - 1)
    bands = np.arange(1, HY_BANDS + 1, dtype=np.float64)
    ang = (2.0 * np.pi / seq) * pos[:, None] * bands[None, :]
    feat = np.concatenate([t[:, None], np.cos(ang), np.sin(ang)], axis=-1)
    feat = np.pad(feat, ((0, 0), (0, LANES - feat.shape[1])))
    return feat.astype(np.float32), t.astype(np.float32)[:, None]


def _dot_split(w_hi, w_lo, g):
    g_hi = g.astype(BF16)
    g_lo = (g - g_hi.astype(F32)).astype(BF16)
    return (jnp.dot(w_hi, g_hi, preferred_element_type=F32)
            + (jnp.dot(w_hi, g_lo, preferred_element_type=F32) + jnp.dot(w_lo, g_hi, preferred_element_type=F32)))


def _hyfilt_kernel(feat_ref, t_ref, w1_ref, b1_ref, w2_ref, b2_ref, w3f_ref, w3b_ref, decf_ref, decb_ref,
                   fwd_ref, fwd_lo_ref, p_ref, fi_ref, s_ref):
    hp = lax.Precision.HIGHEST
    h = jnp.sin(jnp.dot(feat_ref[...], w1_ref[...], precision=hp, preferred_element_type=F32) + b1_ref[...])
    h = jnp.sin(jnp.dot(h, w2_ref[...], precision=hp, preferred_element_type=F32) + b2_ref[...])
    t = t_ref[...]
    seq = t.shape[0]
    row = lax.broadcasted_iota(jnp.int32, (seq, 1), 0)
    hf = jnp.dot(h, w3f_ref[...], precision=hp, preferred_element_type=F32) * jnp.exp(-t * decf_ref[0, 0])
    hb = jnp.dot(h, w3b_ref[...], precision=hp, preferred_element_type=F32) * jnp.exp(-t * decb_ref[0, 0])
    hb = jnp.where(row == 0, 0.0, hb)
    nrm = lax.rsqrt(jnp.sum(hf * hf, axis=0, keepdims=True) + jnp.sum(hb * hb, axis=0, keepdims=True) + EPS)
    gp = (hf + hb) * nrm
    gm = (hf - hb) * nrm
    fa = _dot_split(fwd_ref[0:seq, :], fwd_lo_ref[0:seq, :], gp)
    fb = _dot_split(fwd_ref[seq:2 * seq, :], fwd_lo_ref[seq:2 * seq, :], gm)
    alt = jnp.where(row % 2 == 0, 1.0, -1.0)
    f_nyq = jnp.sum(alt * gp, axis=0, keepdims=True)
    p_ref[0] = fa
    fi_ref[0] = jnp.where(row == 0, 0.0, fb)
    s_ref[0] = jnp.where(row == 0, f_nyq, fa)


def _hyena_filters(seq, w1, b1, w2, b2, w3, decay, fwd_hi, fwd_lo):
    feat_np, t_np = _hy_features(seq)
    nfeat, ffn = w1.shape
    width = decay.shape[-1]
    pf = LANES - ffn
    w1p = jnp.pad(w1, ((0, LANES - nfeat), (0, pf)))
    w2p = jnp.pad(w2, ((0, pf), (0, pf)))
    w3p = jnp.pad(w3, ((0, pf), (0, 0)))
    b1p = jnp.pad(b1, (0, pf)).reshape(1, LANES)
    b2p = jnp.pad(b2, (0, pf)).reshape(1, LANES)
    ct = _tile(width, 256)
    nct = width // ct
    dec = decay.reshape(HY_ORDER * 2, 1, width)
    full = lambda shape: pl.BlockSpec(shape, lambda o, j: (0,) * len(shape))
    out_sd = jax.ShapeDtypeStruct((HY_ORDER, seq, width), F32)
    out_spec = pl.BlockSpec((1, seq, ct), lambda o, j: (o, 0, j))
    return pl.pallas_call(
        _hyfilt_kernel,
        out_shape=(out_sd, out_sd, out_sd),
        grid=(HY_ORDER, nct),
        in_specs=[full(feat_np.shape), full(t_np.shape), full(w1p.shape), full(b1p.shape),
                  full(w2p.shape), full(b2p.shape),
                  pl.BlockSpec((LANES, ct), lambda o, j: (0, o * 2 * nct + j)),
                  pl.BlockSpec((LANES, ct), lambda o, j: (0, (o * 2 + 1) * nct + j)),
                  pl.BlockSpec((1, 1, ct), lambda o, j: (o * 2, 0, j)),
                  pl.BlockSpec((1, 1, ct), lambda o, j: (o * 2 + 1, 0, j)),
                  full(fwd_hi.shape), full(fwd_lo.shape)],
        out_specs=(out_spec, out_spec, out_spec),
        compiler_params=_cparams("parallel", "parallel"),
        name="hyena_filters",
    )(jnp.asarray(feat_np), jnp.asarray(t_np), w1p, b1p, w2p, b2p, w3p, w3p, dec, dec, fwd_hi, fwd_lo)


def _short_conv(u, w):
    seq = u.shape[0]
    row = lax.broadcasted_iota(jnp.int32, (seq, 1), 0)
    prev = jnp.where(row == 0, 0.0, pltpu.roll(u, 1, 0))
    nxt = jnp.where(row == seq - 1, 0.0, pltpu.roll(u, seq - 1, 0))
    return prev * w[0:1, :] + u * w[1:2, :] + nxt * w[2:3, :]


def _hyena_kernel(uv_ref, u1_ref, u2_ref, cwv_ref, cw1_ref, cw2_ref, fwd_ref, inv_ref,
                  p_ref, fi_ref, s_ref, skip_ref, *rest):
    y_ref = rest[-1]
    seq = uv_ref.shape[0]
    z = _short_conv(uv_ref[...], cwv_ref[...])
    gates = (_short_conv(u1_ref[...], cw1_ref[...]), _short_conv(u2_ref[...], cw2_ref[...]))
    for order in range(HY_ORDER):
        zf = jnp.dot(fwd_ref[...], z.astype(BF16), preferred_element_type=F32)
        a, b = zf[:seq], zf[seq:]
        p, fi, s = p_ref[order], fi_ref[order], s_ref[order]
        ya = a * p - b * fi
        yb = a * fi + b * s
        conv = (jnp.dot(inv_ref[:, :seq], ya.astype(BF16), preferred_element_type=F32)
                + jnp.dot(inv_ref[:, seq:], yb.astype(BF16), preferred_element_type=F32))
        z = gates[order] * (conv + skip_ref[order:order + 1, :] * z)
    y_ref[...] = z.astype(y_ref.dtype)


def _hyena_channel_tile(seq, width):
    const = 2 * 2 * (2 * seq * seq * 2)
    per_channel = 20 * seq * 4
    fit = (VMEM_LIMIT_BYTES * 2 // 3 - const) // per_channel
    ct = LANES
    while ct * 2 <= min(fit, width):
        ct *= 2
    return _tile(width, ct)


def _hyena(proj, cols, row0, nb, seq, conv_w, filt, skip, fwd_bf, inv_bf, y_prev=None):
    width = skip.shape[-1]
    ct = _hyena_channel_tile(seq, width)
    nct = width // ct
    rb = row0 // seq
    off = cols["hy"]
    p_arr, fi_arr, s_arr = filt

    def u_spec(part):
        return pl.BlockSpec((seq, ct), lambda j, b, part=part: (rb + b, off // ct + part * nct + j))

    def cw_spec(part):
        return pl.BlockSpec((3, ct), lambda j, b, part=part: (0, part * nct + j))

    full = lambda shape: pl.BlockSpec(shape, lambda j, b: (0,) * len(shape))
    f_spec = pl.BlockSpec((HY_ORDER, seq, ct), lambda j, b: (0, 0, j))
    in_specs = [u_spec(0), u_spec(1), u_spec(2), cw_spec(0), cw_spec(1), cw_spec(2),
                full(fwd_bf.shape), full(inv_bf.shape), f_spec, f_spec, f_spec,
                pl.BlockSpec((HY_ORDER, ct), lambda j, b: (0, j))]
    args = [proj, proj, proj, conv_w, conv_w, conv_w, fwd_bf, inv_bf, p_arr, fi_arr, s_arr, skip]
    y_shape, aliases = _shared_rows_out(proj.shape[0], width, y_prev, in_specs, args)
    return pl.pallas_call(
        _hyena_kernel,
        out_shape=y_shape,
        grid=(nct, nb),
        in_specs=in_specs,
        out_specs=pl.BlockSpec((seq, ct), lambda j, b: (rb + b, j)),
        input_output_aliases=aliases,
        compiler_params=_cparams("parallel", "parallel"),
        name="hyena",
    )(*args)


def _rope_tables(seq, head_dim):
    nfreq = head_dim // 4
    rows = seq // GRID_W
    row = np.repeat(np.arange(rows, dtype=np.float64), GRID_W)
    col = np.tile(np.arange(GRID_W, dtype=np.float64), rows)
    inv = (ROPE_BASE ** (-2.0 * np.arange(nfreq, dtype=np.float32) / (2 * nfreq))).astype(np.float64)
    ar, ac = row[:, None] * inv, col[:, None] * inv
    cos = np.concatenate([np.cos(ar), np.cos(ar), np.cos(ac), np.cos(ac)], axis=1)
    sin = np.concatenate([-np.sin(ar), np.sin(ar), -np.sin(ac), np.sin(ac)], axis=1)
    return cos.astype(np.float32), sin.astype(np.float32)


def _rope(x, cos, sin):
    hd = x.shape[1]
    q = hd // 4
    lane = lax.broadcasted_iota(jnp.int32, x.shape, 1)
    first = (lane % (2 * q)) < q
    partner = jnp.where(first, pltpu.roll(x, hd - q, 1), pltpu.roll(x, q, 1))
    return x * cos + partner * sin


def _attn_kernel(*refs, rope, has_ctx, has_prev, emit_kv, groups, bq, scale):
    q_ref, k_ref, v_ref, qg_ref, kg_ref = refs[:5]
    pos = 5
    if rope:
        cos_ref, sin_ref = refs[pos:pos + 2]
        pos += 2
    if has_ctx:
        kc_ref, vc_ref = refs[pos:pos + 2]
        pos += 2
    pos += has_prev
    y_ref = refs[pos]
    pos += 1
    if emit_kv:
        ko_ref, vo_ref = refs[pos:pos + 2]

    seq, hd = k_ref.shape
    k = k_ref[...]
    kn = k * lax.rsqrt(jnp.mean(k * k, axis=-1, keepdims=True) + EPS) * kg_ref[...]
    v = v_ref[...]
    if emit_kv:
        ko_ref[...] = kn
        vo_ref[...] = v
    if rope:
        kn = _rope(kn, cos_ref[...], sin_ref[...])
    kb, vb = kn.astype(BF16), v.astype(BF16)
    if has_ctx:
        kcb, vcb = kc_ref[0, 0].astype(BF16), vc_ref[0, 0].astype(BF16)
    for g in range(groups):
        for qi in range(seq // bq):
            lo, hi = qi * bq, (qi + 1) * bq
            q = q_ref[lo:hi, g * hd:(g + 1) * hd]
            qn = q * lax.rsqrt(jnp.mean(q * q, axis=-1, keepdims=True) + EPS) * qg_ref[...]
            if rope:
                qn = _rope(qn, cos_ref[lo:hi, :], sin_ref[lo:hi, :])
            qb = (qn * (scale * LOG2E)).astype(BF16)
            s1 = _nt_dot(qb, kb)
            mx = jnp.max(s1, axis=-1, keepdims=True)
            if has_ctx:
                s2 = _nt_dot(qb, kcb)
                mx = jnp.maximum(mx, jnp.max(s2, axis=-1, keepdims=True))
            p1 = jnp.exp2(s1 - mx)
            den = jnp.sum(p1, axis=-1, keepdims=True)
            o = jnp.dot(p1.astype(BF16), vb, preferred_element_type=F32)
            if has_ctx:
                p2 = jnp.exp2(s2 - mx)
                den = den + jnp.sum(p2, axis=-1, keepdims=True)
                o = o + jnp.dot(p2.astype(BF16), vcb, preferred_element_type=F32)
            y_ref[lo:hi, g * hd:(g + 1) * hd] = (o / den).astype(y_ref.dtype)


def _attention(proj, cols, row0, nb, seq, q_g, k_g, ctx, layer, rope, emit_kv, y_prev=None):
    hd = (cols["at_v"] - cols["at_k"]) // ATT_KV_HEADS
    groups = ATT_HEADS // ATT_KV_HEADS
    gw = groups * hd
    rb = row0 // seq
    in_specs = [pl.BlockSpec((seq, gw), lambda b, h: (rb + b, cols["at_q"] // gw + h)),
                pl.BlockSpec((seq, hd), lambda b, h: (rb + b, cols["at_k"] // hd + h)),
                pl.BlockSpec((seq, hd), lambda b, h: (rb + b, cols["at_v"] // hd + h)),
                pl.BlockSpec((1, hd), lambda b, h: (0, 0)),
                pl.BlockSpec((1, hd), lambda b, h: (0, 0))]
    args = [proj, proj, proj, q_g.reshape(1, hd), k_g.reshape(1, hd)]
    if rope:
        cos_np, sin_np = _rope_tables(seq, hd)
        in_specs += [pl.BlockSpec((seq, hd), lambda b, h: (0, 0))] * 2
        args += [jnp.asarray(cos_np), jnp.asarray(sin_np)]
    has_ctx = ctx is not None
    if has_ctx:
        ck, cv = ctx
        past = ck.shape[2]
        in_specs += [pl.BlockSpec((1, 1, past, hd), lambda b, h: (b, layer, 0, h))] * 2
        args += [ck, cv]
    y_shape, aliases = _shared_rows_out(proj.shape[0], ATT_HEADS * hd, y_prev, in_specs, args)
    n_prev = len(aliases)
    out_shape = [y_shape]
    out_specs = [pl.BlockSpec((seq, gw), lambda b, h: (rb + b, h))]
    if emit_kv:
        depth, kv_prev = emit_kv
        out_shape += [jax.ShapeDtypeStruct((nb, depth, seq, ATT_KV_HEADS * hd), F32)] * 2
        out_specs += [pl.BlockSpec((None, None, seq, hd), lambda b, h: (b, layer, 0, h))] * 2
        if kv_prev is not None:
            for k, prev in enumerate(kv_prev):
                in_specs.append(pl.BlockSpec(memory_space=pl.ANY))
                args.append(prev)
                aliases[len(args) - 1] = 1 + k
            n_prev += len(kv_prev)
    outs = pl.pallas_call(
        functools.partial(_attn_kernel, rope=rope, has_ctx=has_ctx, has_prev=n_prev,
                          emit_kv=bool(emit_kv), groups=groups, bq=min(seq, ATT_BLOCK_Q), scale=float(hd) ** -0.5),
        out_shape=tuple(out_shape),
        grid=(nb, ATT_KV_HEADS),
        in_specs=in_specs,
        out_specs=tuple(out_specs),
        input_output_aliases=aliases,
        compiler_params=_cparams("parallel", "parallel"),
        name="attention",
    )(*args)
    if emit_kv:
        return outs
    return outs[0], None, None


def _branch_kernel(y0_ref, y1_ref, y2_ref, g0_ref, g1_ref, g2_ref, w_ref, o_ref):
    acc = None
    for n, (y_ref, g_ref) in enumerate(((y0_ref, g0_ref), (y1_ref, g1_ref), (y2_ref, g2_ref))):
        p = jnp.dot(y_ref[...], w_ref[n], preferred_element_type=F32)
        term = jax.nn.sigmoid(g_ref[...]) * p
        acc = term if acc is None else acc + term
    o_ref[...] = acc.astype(o_ref.dtype)


def _branch_merge(ys, proj, cols, w_branch, layer, tm, tn):
    t, bw = ys[0].shape
    d = w_branch.shape[3]
    goff = cols["br_g"]
    y_spec = pl.BlockSpec((tm, bw), lambda i, j: (i, 0))

    def g_spec(n):
        return pl.BlockSpec((tm, tn), lambda i, j, n=n: (i, (goff + n * d) // tn + j))

    return pl.pallas_call(
        _branch_kernel,
        out_shape=jax.ShapeDtypeStruct((t, d), BF16),
        grid=(t // tm, d // tn),
        in_specs=[y_spec, y_spec, y_spec, g_spec(0), g_spec(1), g_spec(2),
                  pl.BlockSpec((None, N_BRANCH, bw, tn), lambda i, j: (layer, 0, 0, j))],
        out_specs=pl.BlockSpec((tm, tn), lambda i, j: (i, j)),
        compiler_params=_cparams("parallel", "arbitrary"),
        name="branch_merge",
    )(*ys, proj, proj, proj, w_branch)


def _proj_residual_kernel(a_ref, w_ref, x_ref, mod_ref, o_ref, *, gate_idx):
    y = jnp.dot(a_ref[...], w_ref[...], preferred_element_type=F32)
    o_ref[...] = x_ref[...] + mod_ref[0, gate_idx:gate_idx + 1, :] * y


def _proj_residual(a, w, layer, x, mod, seq_row, *, gate_idx, tm, tn):
    t, k = a.shape
    d = w.shape[2]
    return pl.pallas_call(
        functools.partial(_proj_residual_kernel, gate_idx=gate_idx),
        out_shape=jax.ShapeDtypeStruct((t, d), F32),
        grid=(t // tm, d // tn),
        in_specs=[pl.BlockSpec((tm, k), lambda i, j: (i, 0)),
                  pl.BlockSpec((None, k, tn), lambda i, j: (layer, 0, j)),
                  pl.BlockSpec((tm, tn), lambda i, j: (i, j)),
                  pl.BlockSpec((1, 6, tn), lambda i, j: (seq_row(i), 0, j))],
        out_specs=pl.BlockSpec((tm, tn), lambda i, j: (i, j)),
        compiler_params=_cparams("parallel", "arbitrary"),
        name="proj_residual",
    )(a, w, x, mod)


def _topk_rows(s, k, exact_ties):
    rows = s.shape[0]
    iota = lax.broadcasted_iota(jnp.int32, s.shape, 0).astype(F32)
    rank = jnp.full(s.shape, float(k), F32)
    vals = []
    for r in range(k):
        mx = jnp.max(s, axis=0, keepdims=True)
        sel = s == mx
        if exact_ties:
            sel = iota == jnp.min(jnp.where(sel, iota, float(rows)), axis=0, keepdims=True)
        rank = jnp.where(sel, float(r), rank)
        vals.append(mx)
        s = jnp.where(sel, -jnp.inf, s)
    count = jnp.sum(jnp.where(rank < float(k), 1.0, 0.0), axis=0, keepdims=True)
    return jnp.concatenate(vals, axis=0), rank, count


def _route_head(s1, s2, exact_ties):
    k = PEER_TOPK
    sv1, rk1, c1 = _topk_rows(s1, k, exact_ties)
    sv2, rk2, c2 = _topk_rows(s2, k, exact_ties)
    row8 = lax.broadcasted_iota(jnp.int32, (8, s1.shape[1]), 0)
    groups = [sv1[0:1, :] + sv2, sv1[1:2, :] + sv2[0:8, :]]
    for p in range(2, 8):
        groups.append(jnp.where(row8 < k // (p + 1), sv1[p:p + 1, :] + sv2[0:8, :], -jnp.inf))
    groups.append(sv1[8:16, :] + sv2[0:1, :])
    fv, rkc, c3 = _topk_rows(jnp.concatenate(groups, axis=0), k, exact_ties)
    z = jnp.sum(jnp.exp(fv - fv[0:1, :]), axis=0, keepdims=True)
    sel = jnp.where(rkc < float(k), 1.0, 0.0)
    cnt_i = jnp.zeros_like(s1)
    starts = [0, 16] + [24 + 8 * (p - 2) for p in range(2, 8)]
    sizes = [16, 8] + [8] * 6
    for p in range(k):
        if p < 8:
            cnt_p = jnp.sum(sel[starts[p]:starts[p] + sizes[p], :], axis=0, keepdims=True)
        else:
            cnt_p = sel[72 + p - 8:72 + p - 7, :]
        cnt_i = cnt_i + jnp.where(rk1 == float(p), cnt_p, 0.0)
    u1 = jnp.exp(s1 - sv1[0:1, :]) / z
    u2 = jnp.exp(s2 - sv2[0:1, :])
    return u1, cnt_i, u2, rk2, jnp.max(jnp.maximum(jnp.maximum(c1, c2), c3))


def _peer_route_kernel(q_ref, keys_ref, u1_ref, cnt_ref, u2_ref, rk2_ref):
    dh = keys_ref.shape[3]
    s1 = _nt_dot(keys_ref[0, 0], q_ref[:, 0:dh])
    s2 = _nt_dot(keys_ref[0, 1], q_ref[:, dh:2 * dh])

    def emit(u1, cnt, u2, rk2):
        u1_ref[0] = u1
        cnt_ref[0] = cnt
        u2_ref[0] = u2.astype(u2_ref.dtype)
        rk2_ref[0] = rk2.astype(rk2_ref.dtype)

    *fast, most = _route_head(s1, s2, exact_ties=False)
    emit(*fast)

    @pl.when(most > float(PEER_TOPK))
    def _():
        emit(*_route_head(s1, s2, exact_ties=True)[:4])


def _peer_route(q, keys_bf, tr):
    t = q.shape[0]
    heads, _, nkeys, dh = keys_bf.shape
    sd = jax.ShapeDtypeStruct((heads, nkeys, t), F32)
    sd16 = jax.ShapeDtypeStruct((heads, nkeys, t), BF16)
    spec = pl.BlockSpec((1, nkeys, tr), lambda i, h: (h, 0, i))
    return pl.pallas_call(
        _peer_route_kernel,
        out_shape=(sd, sd, sd16, sd16),
        grid=(t // tr, heads),
        in_specs=[pl.BlockSpec((tr, 2 * dh), lambda i, h: (i, h)),
                  pl.BlockSpec((1, 2, nkeys, dh), lambda i, h: (h, 0, 0, 0))],
        out_specs=(spec, spec, spec, spec),
        compiler_params=_cparams("parallel", "parallel"),
        name="peer_route",
    )(q, keys_bf)


def _peer_dense_kernel(ht_ref, u_ref, vt_ref, u1_ref, cnt_ref, u2_ref, rk2_ref, o_ref, w_ref, *, sub, group):
    e = pl.program_id(1)

    @pl.when(e == 0)
    def _():
        o_ref[...] = jnp.zeros_like(o_ref)

    te = u_ref.shape[0]
    nkeys, tm = u2_ref.shape[1:]
    ktiles = nkeys // BF16_ROWS
    for i in range(te // nkeys):
        w = None
        for h in range(PEER_HEADS):
            u1 = jnp.broadcast_to(u1_ref[h, i:i + 1, :], (BF16_ROWS, tm)).astype(BF16)[None]
            cn = jnp.broadcast_to(cnt_ref[h, i:i + 1, :], (BF16_ROWS, tm)).astype(BF16)[None]
            rk = rk2_ref[h].reshape(ktiles, BF16_ROWS, tm)
            u2 = u2_ref[h].reshape(ktiles, BF16_ROWS, tm)
            term = u1 * jnp.where(rk < cn, u2, jnp.zeros((), BF16))
            w = term if w is None else w + term
        w_ref[i * nkeys:(i + 1) * nkeys, :] = w.reshape(nkeys, tm)

    ht = ht_ref[...]
    nsub = te // sub
    pre = [jnp.dot(u_ref[s * sub:(s + 1) * sub, :], ht, preferred_element_type=F32) for s in range(nsub)]
    pending = []
    for s in range(nsub):
        act = jax.nn.gelu(pre[s].astype(BF16), approximate=True)
        pending.append(w_ref[s * sub:(s + 1) * sub, :] * act)
        if len(pending) == group or s + 1 == nsub:
            lo = (s + 1 - len(pending)) * sub
            wa = pending[0] if len(pending) == 1 else jnp.concatenate(pending, axis=0)
            o_ref[...] += jnp.dot(vt_ref[:, lo:(s + 1) * sub], wa, preferred_element_type=F32)
            pending = []


def _peer_dense(h2t, u_bf, vt_bf, layer, route, *, tm, te, sub, group):
    d, t = h2t.shape
    ne = u_bf.shape[1]
    heads, nkeys, _ = route[0].shape
    once = pl.Buffered(1)
    r_spec = pl.BlockSpec((heads, nkeys, tm), lambda i, e: (0, 0, i), pipeline_mode=once)
    k_spec = pl.BlockSpec((heads, te // nkeys, tm), lambda i, e: (0, e, i))
    return pl.pallas_call(
        functools.partial(_peer_dense_kernel, sub=sub, group=group),
        out_shape=jax.ShapeDtypeStruct((d, t), F32),
        grid=(t // tm, ne // te),
        scratch_shapes=[pltpu.VMEM((te, tm), BF16)],
        in_specs=[pl.BlockSpec((d, tm), lambda i, e: (0, i), pipeline_mode=once),
                  pl.BlockSpec((None, te, d), lambda i, e: (layer, e, 0)),
                  pl.BlockSpec((None, d, te), lambda i, e: (layer, 0, e)),
                  k_spec, k_spec, r_spec, r_spec],
        out_specs=pl.BlockSpec((d, tm), lambda i, e: (0, i)),
        compiler_params=_cparams("parallel", "arbitrary"),
        name="peer_dense",
    )(h2t, u_bf, vt_bf, *route)


def _peer_residual_kernel(x_ref, yt_ref, mod_ref, fg_ref, o_ref, *, final_norm):
    y = x_ref[...] + mod_ref[0, 5:6, :] * yt_ref[...].T
    if final_norm:
        y = y * lax.rsqrt(jnp.mean(y * y, axis=-1, keepdims=True) + EPS) * fg_ref[...]
    o_ref[...] = y


def _peer_residual(x, yt, mod, final_g, seq_row, *, final_norm, tm, first_tile=0, ntiles=None):
    t, d = x.shape
    ntiles = t // tm if ntiles is None else ntiles
    return pl.pallas_call(
        functools.partial(_peer_residual_kernel, final_norm=final_norm),
        out_shape=jax.ShapeDtypeStruct((ntiles * tm, d), F32),
        grid=(ntiles,),
        in_specs=[pl.BlockSpec((tm, d), lambda i: (first_tile + i, 0)),
                  pl.BlockSpec((d, tm), lambda i: (0, first_tile + i)),
                  pl.BlockSpec((1, 6, d), lambda i: (seq_row(first_tile + i), 0, 0)),
                  pl.BlockSpec((1, d), lambda i: (0, 0))],
        out_specs=pl.BlockSpec((tm, d), lambda i: (i, 0)),
        compiler_params=_cparams("parallel"),
        name="peer_residual",
    )(x, yt, mod, final_g.reshape(1, d))


def _in_layout(bw, kvw, d):
    sizes = (("ml_q", bw), ("ml_k", bw), ("ml_v", bw), ("ml_o", bw), ("hy", 3 * bw),
             ("at_q", bw), ("at_k", kvw), ("at_v", kvw), ("br_g", N_BRANCH * d), ("ml_g", GATE_PAD))
    cols, off = {}, 0
    for name, size in sizes:
        cols[name] = off
        off += size
    return cols, off


def _pack_w_in(w_in, bw):
    depth, d, _ = w_in.shape
    ngate = 4 * ML_HEADS
    gates = w_in[:, :, 4 * bw:4 * bw + ngate].reshape(depth, d, 4, ML_HEADS)
    gates = jnp.swapaxes(gates, 2, 3)
    gates = jnp.pad(gates, ((0, 0), (0, 0), (0, 0), (0, GATE_PAD // ML_HEADS - 4))).reshape(depth, d, GATE_PAD)
    return jnp.concatenate([w_in[:, :, :4 * bw], w_in[:, :, 4 * bw + ngate:], gates], axis=2).astype(BF16)


def kernel(x_prompt, x_sample, c, cache_k, cache_v, state_C, state_n, state_m, c_ctx, w_ada, b_ada, norm1_g, norm2_g, w_in, ml_gate_bias, ml_norm_g, hy_conv_w, hy_w1, hy_b1, hy_w2, hy_b2, hy_w3, hy_decay, hy_skip, q_norm_g, k_norm_g, w_branch, w_out, peer_wq, peer_keys, peer_u, peer_v, final_g):
    nbp, seq_p, d = x_prompt.shape
    nbs, seq_s, _ = x_sample.shape
    depth = w_ada.shape[0]
    bw = d // 2
    kvw = cache_k.shape[3] * cache_k.shape[4]
    tp, ts = nbp * seq_p, nbs * seq_s
    cols, ncols = _in_layout(bw, kvw, d)
    assert w_in.shape[2] == ncols - GATE_PAD + 4 * ML_HEADS

    tm = _tile(seq_s, TOKEN_TILE)
    tn = TOKEN_TILE
    assert tp % tm == 0 and PEER_TOPK == 16
    tiles_p, tiles_per_seq = tp // tm, seq_s // tm

    def seq_row(i):
        return jnp.where(i < tiles_p, 0, 1 + (i - tiles_p) // tiles_per_seq)

    x = jnp.concatenate([x_prompt.reshape(tp, d), x_sample.reshape(ts, d)], axis=0)
    cond = jnp.concatenate([c_ctx[None, :], c], axis=0)
    nrow = cond.shape[0]
    cond = jnp.pad(cond, ((0, (-nrow) % 8), (0, 0)))
    mod_all = _ada(cond, w_ada, b_ada).reshape(depth, cond.shape[0], 6, d)

    ctx_k = cache_k.reshape(nbs, depth, cache_k.shape[2], kvw)
    ctx_v = cache_v.reshape(nbs, depth, cache_v.shape[2], kvw)
    dft = {}
    for seq in {seq_p, seq_s}:
        fwd_np, inv_np = _dft_mats(seq)
        fwd_hi = jnp.asarray(fwd_np).astype(BF16)
        fwd_lo = (jnp.asarray(fwd_np) - fwd_hi.astype(F32)).astype(BF16)
        dft[seq] = (fwd_hi, fwd_lo, jnp.asarray(inv_np).astype(BF16))

    w_in_bf = _pack_w_in(w_in, bw)
    w_branch_bf, w_out_bf, wq_bf = w_branch.astype(BF16), w_out.astype(BF16), peer_wq.astype(BF16)
    u_bf = peer_u.astype(BF16)
    vt_bf = jnp.swapaxes(peer_v, 1, 2).astype(BF16)
    keys_bf = peer_keys.astype(BF16)

    new_state = new_kv = None
    for l in range(depth):
        mod = mod_all[l]
        proj = _modnorm_matmul(x, mod, norm1_g[l], w_in_bf, l, seq_row, shift_idx=0, out_dtype=F32,
                               tm=tm, tn=_tile(ncols, tn))[0]
        gb = ml_gate_bias[l].reshape(4, ML_HEADS).T
        gb = jnp.pad(gb, ((0, 0), (0, LANES - 4))).reshape(ML_HEADS, 1, LANES)

        y_ml = y_hy = y_at = None
        for (row0, nb, seq, is_ctx) in ((0, nbp, seq_p, True), (tp, nbs, seq_s, False)):
            state = None if is_ctx else (state_C[:, l], state_n[:, l], state_m[:, l])
            y_ml, st = _mlstm(proj, cols, row0, nb, seq, gb, ml_norm_g[l], state,
                              emit_state=(l, depth, new_state) if is_ctx else None, y_prev=y_ml)
            fwd_bf, fwd_lo, inv_bf = dft[seq]
            filt = _hyena_filters(seq, hy_w1[l], hy_b1[l], hy_w2[l], hy_b2[l], hy_w3[l], hy_decay[l], fwd_bf, fwd_lo)
            y_hy = _hyena(proj, cols, row0, nb, seq, hy_conv_w[l], filt, hy_skip[l], fwd_bf, inv_bf, y_prev=y_hy)
            y_at, k_l, v_l = _attention(proj, cols, row0, nb, seq, q_norm_g[l], k_norm_g[l],
                                        None if is_ctx else (ctx_k, ctx_v), l, rope=not is_ctx,
                                        emit_kv=(depth, new_kv) if is_ctx else None, y_prev=y_at)
            if is_ctx:
                new_state, new_kv = st, (k_l, v_l)
        mixed = _branch_merge((y_ml, y_hy, y_at), proj, cols, w_branch_bf, l, tm, tn // 2)
        x = _proj_residual(mixed, w_out_bf, l, x, mod, seq_row, gate_idx=2, tm=tm, tn=tn)

        q, _, h2t = _modnorm_matmul(x, mod, norm2_g[l], wq_bf, l, seq_row, shift_idx=3,
                                    out_dtype=BF16, tm=tm, tn=tn, emit_ht=True)
        route = _peer_route(q, keys_bf[l], _tile(tm, ROUTE_TILE))
        yt = _peer_dense(h2t, u_bf, vt_bf, l, route, tm=_tile(tm, EXPERT_TOKEN_TILE), te=EXPERT_TILE,
                         sub=EXPERT_SUB, group=EXPERT_GROUP)
        if l < depth - 1:
            x = _peer_residual(x, yt, mod, final_g, seq_row, final_norm=False, tm=tm)

    y_prompt = _peer_residual(x, yt, mod, final_g, seq_row, final_norm=True, tm=tm,
                              first_tile=0, ntiles=tiles_p).reshape(nbp, seq_p, d)
    y_sample = _peer_residual(x, yt, mod, final_g, seq_row, final_norm=True, tm=tm,
                              first_tile=tiles_p, ntiles=ts // tm).reshape(nbs, seq_s, d)
    kv_shape = (nbp, depth, seq_p, ATT_KV_HEADS, kvw // ATT_KV_HEADS)
    new_c, new_n, new_m = new_state
    return (y_prompt, y_sample, new_kv[0].reshape(kv_shape), new_kv[1].reshape(kv_shape),
            new_c, new_n[:, :, :, :, 0, :], new_m[:, :, :, :, 0, 0])
```

```python
import functools

import numpy as np
import jax
import jax.numpy as jnp
from jax import lax
from jax.experimental import pallas as pl
from jax.experimental.pallas import tpu as pltpu

F32 = jnp.float32
BF16 = jnp.bfloat16
EPS = 1e-6
LOG2E = 1.4426950408889634

GRID_W = 64
ML_HEADS = 4
ATT_HEADS = 8
ATT_KV_HEADS = 2
ROPE_BASE = 10000.0
HY_BANDS = 16
HY_ORDER = 2
N_BRANCH = 3
PEER_HEADS = 8
PEER_TOPK = 16

VMEM_LIMIT_BYTES = 60 * 1024 * 1024
LANES = 128
BF16_ROWS = 16
ML_CHUNK = 512
ATT_BLOCK_Q = 256
TOKEN_TILE = 1024
ROUTE_TILE = 512
EXPERT_TILE = 2048
EXPERT_SUB = 256
EXPERT_GROUP = 8
EXPERT_TOKEN_TILE = 512
GATE_PAD = 512


def _cparams(*sem):
    return pltpu.CompilerParams(dimension_semantics=sem, vmem_limit_bytes=VMEM_LIMIT_BYTES)


def _tile(n, pref):
    t = min(n, pref)
    while n % t:
        t //= 2
    return t


def _nt_dot(a, b):
    return lax.dot_general(a, b, (((1,), (1,)), ((), ())), preferred_element_type=F32)


def _ada_kernel(c_ref, w_ref, b_ref, o_ref):
    c = c_ref[...]
    a = c * jax.nn.sigmoid(c)
    o_ref[0] = jnp.dot(a.astype(BF16), w_ref[0].astype(BF16), preferred_element_type=F32) + b_ref[0]


def _ada(cond, w_ada, b_ada):
    depth, d, n = w_ada.shape
    r = cond.shape[0]
    tn = _tile(n, 1024)
    return pl.pallas_call(
        _ada_kernel,
        out_shape=jax.ShapeDtypeStruct((depth, r, n), F32),
        grid=(depth, n // tn),
        in_specs=[pl.BlockSpec((r, d), lambda l, j: (0, 0)),
                  pl.BlockSpec((1, d, tn), lambda l, j: (l, 0, j)),
                  pl.BlockSpec((1, 1, tn), lambda l, j: (l, 0, j))],
        out_specs=pl.BlockSpec((1, r, tn), lambda l, j: (l, 0, j)),
        compiler_params=_cparams("parallel", "parallel"),
        name="ada",
    )(cond, w_ada, b_ada.reshape(depth, 1, n))


def _modnorm_matmul_kernel(x_ref, mod_ref, g_ref, w_ref, o_ref, h_ref, *ht_ref, shift_idx):
    @pl.when(pl.program_id(1) == 0)
    def _():
        x = x_ref[...]
        y = x * lax.rsqrt(jnp.mean(x * x, axis=-1, keepdims=True) + EPS) * g_ref[...]
        sh = mod_ref[0, shift_idx:shift_idx + 1, :]
        sc = mod_ref[0, shift_idx + 1:shift_idx + 2, :]
        h = y * (1.0 + sc) + sh
        h_ref[...] = h.astype(BF16)
        if ht_ref:
            ht_ref[0][...] = h.T.astype(BF16)

    o_ref[...] = jnp.dot(h_ref[...], w_ref[...], preferred_element_type=F32).astype(o_ref.dtype)


def _modnorm_matmul(x, mod, g, w, layer, seq_row, *, shift_idx, out_dtype, tm, tn, emit_ht=False):
    t, d = x.shape
    n = w.shape[2]
    out_shape = [jax.ShapeDtypeStruct((t, n), out_dtype), jax.ShapeDtypeStruct((t, d), BF16)]
    out_specs = [pl.BlockSpec((tm, tn), lambda i, j: (i, j)), pl.BlockSpec((tm, d), lambda i, j: (i, 0))]
    if emit_ht:
        out_shape.append(jax.ShapeDtypeStruct((d, t), BF16))
        out_specs.append(pl.BlockSpec((d, tm), lambda i, j: (0, i)))
    return pl.pallas_call(
        functools.partial(_modnorm_matmul_kernel, shift_idx=shift_idx),
        out_shape=tuple(out_shape),
        grid=(t // tm, n // tn),
        in_specs=[pl.BlockSpec((tm, d), lambda i, j: (i, 0)),
                  pl.BlockSpec((1, 6, d), lambda i, j: (seq_row(i), 0, 0)),
                  pl.BlockSpec((1, d), lambda i, j: (0, 0)),
                  pl.BlockSpec((None, d, tn), lambda i, j: (layer, 0, j))],
        out_specs=tuple(out_specs),
        compiler_params=_cparams("parallel", "arbitrary"),
        name="modnorm_matmul",
    )(x, mod, g.reshape(1, d), w)


def _log_sigmoid(x):
    return jnp.minimum(x, 0.0) - jnp.log1p(jnp.exp(-jnp.abs(x)))


def _mlstm_chunk(qc, kc, vc, lf_col, ig_col, lf_row, ig_row, C, n, m, fwd, k_scale):
    tc = qc.shape[0]
    r = lax.broadcasted_iota(jnp.int32, (tc, tc), 0)
    c = lax.broadcasted_iota(jnp.int32, (tc, tc), 1)
    mask = (c <= r) if fwd else (c >= r)
    mask_t = (r <= c) if fwd else (r >= c)
    b_col = jnp.sum(jnp.where(mask, lf_row, 0.0), axis=1, keepdims=True)
    b_row = jnp.sum(jnp.where(mask_t, lf_col, 0.0), axis=0, keepdims=True)
    dmat = jnp.where(mask, b_col - b_row + ig_row, -jnp.inf)
    m_t = jnp.maximum(b_col + m, jnp.max(dmat, axis=1, keepdims=True))
    w_in = jnp.exp(dmat - m_t)
    w_prev = jnp.exp(b_col + m - m_t)
    qb, kb, vb = qc.astype(BF16), kc.astype(BF16), vc.astype(BF16)
    s = _nt_dot(qb, kb) * k_scale * w_in
    num = (jnp.dot(s.astype(BF16), vb, preferred_element_type=F32)
           + w_prev * jnp.dot(qb, C.astype(BF16), preferred_element_type=F32))
    den = jnp.sum(s, axis=1, keepdims=True) + w_prev * jnp.sum(qc * n, axis=1, keepdims=True)
    h = num / jnp.maximum(jnp.abs(den), jnp.exp(-m_t))
    b_last = jnp.sum(lf_row, axis=1, keepdims=True)
    lw_col = b_last - b_col + ig_col
    lw_row = b_last - b_row + ig_row
    m_new = jnp.maximum(b_last + m, jnp.max(lw_row, axis=1, keepdims=True))
    decay = jnp.exp(b_last + m - m_new)
    kw = kc * k_scale * jnp.exp(lw_col - m_new)
    C_new = decay * C + jnp.dot(kw.T.astype(BF16), vb, preferred_element_type=F32)
    n_new = decay * n + jnp.sum(kw, axis=0, keepdims=True)
    return h, C_new, n_new, m_new


def _shared_rows_out(total_rows, width, prev, in_specs, args):
    aliases = {}
    if prev is not None:
        in_specs.append(pl.BlockSpec(memory_space=pl.ANY))
        args.append(prev)
        aliases = {len(args) - 1: 0}
    return jax.ShapeDtypeStruct((total_rows, width), BF16), aliases


def _mlstm_kernel(*refs, seq, chunk, has_state, has_prev, emit_state, k_scale):
    q_ref, k_ref, v_ref, o_ref, g_ref, bias_ref, ng_ref = refs[:7]
    pos = 7
    if has_state:
        c0_ref, n0_ref, m0_ref = refs[pos:pos + 3]
        pos += 3
    pos += has_prev
    y_ref = refs[pos]
    pos += 1
    if emit_state:
        c_out, n_out, m_out = refs[pos:pos + 3]
        pos += 3
    hacc = refs[pos]

    g = g_ref[...] + bias_ref[0]
    gt = g.T
    dk = q_ref.shape[1]
    nchunks = seq // chunk
    for d in range(2):
        fwd = d == 0
        ig_col_all = g[:, 2 * d:2 * d + 1]
        lf_col_all = _log_sigmoid(g[:, 2 * d + 1:2 * d + 2])
        ig_row_all = gt[2 * d:2 * d + 1, :]
        lf_row_all = _log_sigmoid(gt[2 * d + 1:2 * d + 2, :])
        if has_state:
            C = c0_ref[0, d, 0]
            n = n0_ref[0, d, 0]
            m = m0_ref[0, d, 0][:, 0:1]
        else:
            C = jnp.zeros((dk, v_ref.shape[1]), F32)
            n = jnp.zeros((1, dk), F32)
            m = jnp.zeros((1, 1), F32)
        order = range(nchunks) if fwd else range(nchunks - 1, -1, -1)
        for ci in order:
            lo, hi = ci * chunk, (ci + 1) * chunk
            h, C, n, m = _mlstm_chunk(
                q_ref[lo:hi, :], k_ref[lo:hi, :], v_ref[lo:hi, :],
                lf_col_all[lo:hi], ig_col_all[lo:hi], lf_row_all[:, lo:hi], ig_row_all[:, lo:hi],
                C, n, m, fwd, k_scale)
            if fwd:
                hacc[lo:hi, :] = h
            else:
                hacc[lo:hi, :] += h
        if emit_state:
            c_out[0, d, 0] = C
            n_out[0, d, 0] = n
            m_out[0, d, 0] = jnp.broadcast_to(m, (1, LANES))
    hh = hacc[...]
    hn = hh * lax.rsqrt(jnp.mean(hh * hh, axis=-1, keepdims=True) + EPS) * ng_ref[0]
    y_ref[...] = (jax.nn.sigmoid(o_ref[...]) * hn).astype(y_ref.dtype)


def _mlstm(proj, cols, row0, nb, seq, gate_bias, norm_g, state, emit_state, y_prev=None):
    heads = ML_HEADS
    dk = (cols["ml_k"] - cols["ml_q"]) // heads
    chunk = min(ML_CHUNK, seq)
    rb = row0 // seq

    def col_spec(off, width):
        return pl.BlockSpec((seq, width), lambda b, h, off=off, width=width: (rb + b, off // width + h))

    in_specs = [col_spec(cols["ml_q"], dk), col_spec(cols["ml_k"], dk), col_spec(cols["ml_v"], dk),
                col_spec(cols["ml_o"], dk), col_spec(cols["ml_g"], LANES),
                pl.BlockSpec((1, 1, LANES), lambda b, h: (h, 0, 0)),
                pl.BlockSpec((1, 1, dk), lambda b, h: (h, 0, 0))]
    args = [proj, proj, proj, proj, proj, gate_bias, norm_g.reshape(heads, 1, dk)]
    has_state = state is not None
    if has_state:
        c0, n0, m0 = state
        in_specs += [pl.BlockSpec((1, 2, 1, dk, dk), lambda b, h: (b, 0, h, 0, 0)),
                     pl.BlockSpec((1, 2, 1, 1, dk), lambda b, h: (b, 0, h, 0, 0)),
                     pl.BlockSpec((1, 2, 1, 1, LANES), lambda b, h: (b, 0, h, 0, 0))]
        args += [c0, n0.reshape(nb, 2, heads, 1, dk),
                 jnp.broadcast_to(m0[..., None, None], (nb, 2, heads, 1, LANES))]
    y_shape, aliases = _shared_rows_out(proj.shape[0], heads * dk, y_prev, in_specs, args)
    n_prev = len(aliases)
    out_shape = [y_shape]
    out_specs = [pl.BlockSpec((seq, dk), lambda b, h: (rb + b, h))]
    if emit_state:
        layer, depth, state_prev = emit_state
        out_shape += [jax.ShapeDtypeStruct((nb, depth, 2, heads, dk, dk), F32),
                      jax.ShapeDtypeStruct((nb, depth, 2, heads, 1, dk), F32),
                      jax.ShapeDtypeStruct((nb, depth, 2, heads, 1, LANES), F32)]
        out_specs += [pl.BlockSpec((1, None, 2, 1, dk, dk), lambda b, h: (b, layer, 0, h, 0, 0)),
                      pl.BlockSpec((1, None, 2, 1, 1, dk), lambda b, h: (b, layer, 0, h, 0, 0)),
                      pl.BlockSpec((1, None, 2, 1, 1, LANES), lambda b, h: (b, layer, 0, h, 0, 0))]
        if state_prev is not None:
            for k, prev in enumerate(state_prev):
                in_specs.append(pl.BlockSpec(memory_space=pl.ANY))
                args.append(prev)
                aliases[len(args) - 1] = 1 + k
            n_prev += len(state_prev)
    outs = pl.pallas_call(
        functools.partial(_mlstm_kernel, seq=seq, chunk=chunk, has_state=has_state,
                          has_prev=n_prev, emit_state=bool(emit_state), k_scale=float(dk) ** -0.5),
        out_shape=tuple(out_shape),
        grid=(nb, heads),
        in_specs=in_specs,
        out_specs=tuple(out_specs),
        input_output_aliases=aliases,
        scratch_shapes=[pltpu.VMEM((seq, dk), F32)],
        compiler_params=_cparams("parallel", "parallel"),
        name="mlstm",
    )(*args)
    if emit_state:
        return outs[0], tuple(outs[1:])
    return outs[0], None


def _dft_mats(seq):
    k = np.arange(seq, dtype=np.int64)
    ang = np.pi * ((k[:, None] * k[None, :]) % (2 * seq)).astype(np.float64) / seq
    cos, sin = np.cos(ang), np.sin(ang)
    alt = np.where(k % 2 == 0, 1.0, -1.0)
    fwd_b = -sin
    fwd_b[0, :] = alt
    fwd = np.concatenate([cos, fwd_b], axis=0)
    inv_a = cos.T / seq
    inv_a[:, 0] = 0.5 / seq
    inv_b = -sin.T / seq
    inv_b[:, 0] = alt * 0.5 / seq
    inv = np.concatenate([inv_a, inv_b], axis=1)
    return fwd.astype(np.float32), inv.astype(np.float32)


def _hy_features(seq):
    pos = np.arange(seq, dtype=np.float64)
    t = pos / (seq - 1)
    bands = np.arange(1, HY_BANDS + 1, dtype=np.float64)
    ang = (2.0 * np.pi / seq) * pos[:, None] * bands[None, :]
    feat = np.concatenate([t[:, None], np.cos(ang), np.sin(ang)], axis=-1)
    feat = np.pad(feat, ((0, 0), (0, LANES - feat.shape[1])))
    return feat.astype(np.float32), t.astype(np.float32)[:, None]


def _dot_split(w_hi, w_lo, g):
    g_hi = g.astype(BF16)
    g_lo = (g - g_hi.astype(F32)).astype(BF16)
    return (jnp.dot(w_hi, g_hi, preferred_element_type=F32)
            + (jnp.dot(w_hi, g_lo, preferred_element_type=F32) + jnp.dot(w_lo, g_hi, preferred_element_type=F32)))


def _hyfilt_kernel(feat_ref, t_ref, w1_ref, b1_ref, w2_ref, b2_ref, w3f_ref, w3b_ref, decf_ref, decb_ref,
                   fwd_ref, fwd_lo_ref, p_ref, fi_ref, s_ref):
    hp = lax.Precision.HIGHEST
    h = jnp.sin(jnp.dot(feat_ref[...], w1_ref[...], precision=hp, preferred_element_type=F32) + b1_ref[...])
    h = jnp.sin(jnp.dot(h, w2_ref[...], precision=hp, preferred_element_type=F32) + b2_ref[...])
    t = t_ref[...]
    seq = t.shape[0]
    row = lax.broadcasted_iota(jnp.int32, (seq, 1), 0)
    hf = jnp.dot(h, w3f_ref[...], precision=hp, preferred_element_type=F32) * jnp.exp(-t * decf_ref[0, 0])
    hb = jnp.dot(h, w3b_ref[...], precision=hp, preferred_element_type=F32) * jnp.exp(-t * decb_ref[0, 0])
    hb = jnp.where(row == 0, 0.0, hb)
    nrm = lax.rsqrt(jnp.sum(hf * hf, axis=0, keepdims=True) + jnp.sum(hb * hb, axis=0, keepdims=True) + EPS)
    gp = (hf + hb) * nrm
    gm = (hf - hb) * nrm
    fa = _dot_split(fwd_ref[0:seq, :], fwd_lo_ref[0:seq, :], gp)
    fb = _dot_split(fwd_ref[seq:2 * seq, :], fwd_lo_ref[seq:2 * seq, :], gm)
    alt = jnp.where(row % 2 == 0, 1.0, -1.0)
    f_nyq = jnp.sum(alt * gp, axis=0, keepdims=True)
    p_ref[0] = fa
    fi_ref[0] = jnp.where(row == 0, 0.0, fb)
    s_ref[0] = jnp.where(row == 0, f_nyq, fa)


def _hyena_filters(seq, w1, b1, w2, b2, w3, decay, fwd_hi, fwd_lo):
    feat_np, t_np = _hy_features(seq)
    nfeat, ffn = w1.shape
    width = decay.shape[-1]
    pf = LANES - ffn
    w1p = jnp.pad(w1, ((0, LANES - nfeat), (0, pf)))
    w2p = jnp.pad(w2, ((0, pf), (0, pf)))
    w3p = jnp.pad(w3, ((0, pf), (0, 0)))
    b1p = jnp.pad(b1, (0, pf)).reshape(1, LANES)
    b2p = jnp.pad(b2, (0, pf)).reshape(1, LANES)
    ct = _tile(width, 256)
    nct = width // ct
    dec = decay.reshape(HY_ORDER * 2, 1, width)
    full = lambda shape: pl.BlockSpec(shape, lambda o, j: (0,) * len(shape))
    out_sd = jax.ShapeDtypeStruct((HY_ORDER, seq, width), F32)
    out_spec = pl.BlockSpec((1, seq, ct), lambda o, j: (o, 0, j))
    return pl.pallas_call(
        _hyfilt_kernel,
        out_shape=(out_sd, out_sd, out_sd),
        grid=(HY_ORDER, nct),
        in_specs=[full(feat_np.shape), full(t_np.shape), full(w1p.shape), full(b1p.shape),
                  full(w2p.shape), full(b2p.shape),
                  pl.BlockSpec((LANES, ct), lambda o, j: (0, o * 2 * nct + j)),
                  pl.BlockSpec((LANES, ct), lambda o, j: (0, (o * 2 + 1) * nct + j)),
                  pl.BlockSpec((1, 1, ct), lambda o, j: (o * 2, 0, j)),
                  pl.BlockSpec((1, 1, ct), lambda o, j: (o * 2 + 1, 0, j)),
                  full(fwd_hi.shape), full(fwd_lo.shape)],
        out_specs=(out_spec, out_spec, out_spec),
        compiler_params=_cparams("parallel", "parallel"),
        name="hyena_filters",
    )(jnp.asarray(feat_np), jnp.asarray(t_np), w1p, b1p, w2p, b2p, w3p, w3p, dec, dec, fwd_hi, fwd_lo)


def _short_conv(u, w):
    seq = u.shape[0]
    row = lax.broadcasted_iota(jnp.int32, (seq, 1), 0)
    prev = jnp.where(row == 0, 0.0, pltpu.roll(u, 1, 0))
    nxt = jnp.where(row == seq - 1, 0.0, pltpu.roll(u, seq - 1, 0))
    return prev * w[0:1, :] + u * w[1:2, :] + nxt * w[2:3, :]


def _hyena_kernel(uv_ref, u1_ref, u2_ref, cwv_ref, cw1_ref, cw2_ref, fwd_ref, inv_ref,
                  p_ref, fi_ref, s_ref, skip_ref, *rest):
    y_ref = rest[-1]
    seq = uv_ref.shape[0]
    z = _short_conv(uv_ref[...], cwv_ref[...])
    gates = (_short_conv(u1_ref[...], cw1_ref[...]), _short_conv(u2_ref[...], cw2_ref[...]))
    for order in range(HY_ORDER):
        zf = jnp.dot(fwd_ref[...], z.astype(BF16), preferred_element_type=F32)
        a, b = zf[:seq], zf[seq:]
        p, fi, s = p_ref[order], fi_ref[order], s_ref[order]
        ya = a * p - b * fi
        yb = a * fi + b * s
        conv = (jnp.dot(inv_ref[:, :seq], ya.astype(BF16), preferred_element_type=F32)
                + jnp.dot(inv_ref[:, seq:], yb.astype(BF16), preferred_element_type=F32))
        z = gates[order] * (conv + skip_ref[order:order + 1, :] * z)
    y_ref[...] = z.astype(y_ref.dtype)


def _hyena_channel_tile(seq, width):
    const = 2 * (2 * seq * seq * 2)
    per_channel = 20 * seq * 4
    fit = (VMEM_LIMIT_BYTES * 5 // 6 - const) // per_channel
    ct = LANES
    while ct * 2 <= min(fit, width):
        ct *= 2
    return _tile(width, ct)


def _hyena(proj, cols, row0, nb, seq, conv_w, filt, skip, fwd_bf, inv_bf, y_prev=None):
    width = skip.shape[-1]
    ct = _hyena_channel_tile(seq, width)
    nct = width // ct
    rb = row0 // seq
    off = cols["hy"]
    p_arr, fi_arr, s_arr = filt

    def u_spec(part):
        return pl.BlockSpec((seq, ct), lambda j, b, part=part: (rb + b, off // ct + part * nct + j))

    def cw_spec(part):
        return pl.BlockSpec((3, ct), lambda j, b, part=part: (0, part * nct + j))

    once = pl.Buffered(1)
    full = lambda shape: pl.BlockSpec(shape, lambda j, b: (0,) * len(shape), pipeline_mode=once)
    f_spec = pl.BlockSpec((HY_ORDER, seq, ct), lambda j, b: (0, 0, j), pipeline_mode=once)
    in_specs = [u_spec(0), u_spec(1), u_spec(2), cw_spec(0), cw_spec(1), cw_spec(2),
                full(fwd_bf.shape), full(inv_bf.shape), f_spec, f_spec, f_spec,
                pl.BlockSpec((HY_ORDER, ct), lambda j, b: (0, j))]
    args = [proj, proj, proj, conv_w, conv_w, conv_w, fwd_bf, inv_bf, p_arr, fi_arr, s_arr, skip]
    y_shape, aliases = _shared_rows_out(proj.shape[0], width, y_prev, in_specs, args)
    return pl.pallas_call(
        _hyena_kernel,
        out_shape=y_shape,
        grid=(nct, nb),
        in_specs=in_specs,
        out_specs=pl.BlockSpec((seq, ct), lambda j, b: (rb + b, j)),
        input_output_aliases=aliases,
        compiler_params=_cparams("parallel", "parallel"),
        name="hyena",
    )(*args)


def _rope_tables(seq, head_dim):
    nfreq = head_dim // 4
    rows = seq // GRID_W
    row = np.repeat(np.arange(rows, dtype=np.float64), GRID_W)
    col = np.tile(np.arange(GRID_W, dtype=np.float64), rows)
    inv = (ROPE_BASE ** (-2.0 * np.arange(nfreq, dtype=np.float32) / (2 * nfreq))).astype(np.float64)
    ar, ac = row[:, None] * inv, col[:, None] * inv
    cos = np.concatenate([np.cos(ar), np.cos(ar), np.cos(ac), np.cos(ac)], axis=1)
    sin = np.concatenate([-np.sin(ar), np.sin(ar), -np.sin(ac), np.sin(ac)], axis=1)
    return cos.astype(np.float32), sin.astype(np.float32)


def _rope(x, cos, sin):
    hd = x.shape[1]
    q = hd // 4
    lane = lax.broadcasted_iota(jnp.int32, x.shape, 1)
    first = (lane % (2 * q)) < q
    partner = jnp.where(first, pltpu.roll(x, hd - q, 1), pltpu.roll(x, q, 1))
    return x * cos + partner * sin


def _attn_kernel(*refs, rope, has_ctx, has_prev, emit_kv, groups, bq, scale):
    q_ref, k_ref, v_ref, qg_ref, kg_ref = refs[:5]
    pos = 5
    if rope:
        cos_ref, sin_ref = refs[pos:pos + 2]
        pos += 2
    if has_ctx:
        kc_ref, vc_ref = refs[pos:pos + 2]
        pos += 2
    pos += has_prev
    y_ref = refs[pos]
    pos += 1
    if emit_kv:
        ko_ref, vo_ref = refs[pos:pos + 2]

    seq, hd = k_ref.shape
    k = k_ref[...]
    kn = k * lax.rsqrt(jnp.mean(k * k, axis=-1, keepdims=True) + EPS) * kg_ref[...]
    v = v_ref[...]
    if emit_kv:
        ko_ref[...] = kn
        vo_ref[...] = v
    if rope:
        kn = _rope(kn, cos_ref[...], sin_ref[...])
    kb, vb = kn.astype(BF16), v.astype(BF16)
    if has_ctx:
        kcb, vcb = kc_ref[0, 0].astype(BF16), vc_ref[0, 0].astype(BF16)
    for g in range(groups):
        for qi in range(seq // bq):
            lo, hi = qi * bq, (qi + 1) * bq
            q = q_ref[lo:hi, g * hd:(g + 1) * hd]
            qn = q * lax.rsqrt(jnp.mean(q * q, axis=-1, keepdims=True) + EPS) * qg_ref[...]
            if rope:
                qn = _rope(qn, cos_ref[lo:hi, :], sin_ref[lo:hi, :])
            qb = (qn * (scale * LOG2E)).astype(BF16)
            s1 = _nt_dot(qb, kb)
            mx = jnp.max(s1, axis=-1, keepdims=True)
            if has_ctx:
                s2 = _nt_dot(qb, kcb)
                mx = jnp.maximum(mx, jnp.max(s2, axis=-1, keepdims=True))
            p1 = jnp.exp2(s1 - mx)
            den = jnp.sum(p1, axis=-1, keepdims=True)
            o = jnp.dot(p1.astype(BF16), vb, preferred_element_type=F32)
            if has_ctx:
                p2 = jnp.exp2(s2 - mx)
                den = den + jnp.sum(p2, axis=-1, keepdims=True)
                o = o + jnp.dot(p2.astype(BF16), vcb, preferred_element_type=F32)
            y_ref[lo:hi, g * hd:(g + 1) * hd] = (o / den).astype(y_ref.dtype)


def _attention(proj, cols, row0, nb, seq, q_g, k_g, ctx, layer, rope, emit_kv, y_prev=None):
    hd = (cols["at_v"] - cols["at_k"]) // ATT_KV_HEADS
    groups = ATT_HEADS // ATT_KV_HEADS
    gw = groups * hd
    rb = row0 // seq
    in_specs = [pl.BlockSpec((seq, gw), lambda b, h: (rb + b, cols["at_q"] // gw + h)),
                pl.BlockSpec((seq, hd), lambda b, h: (rb + b, cols["at_k"] // hd + h)),
                pl.BlockSpec((seq, hd), lambda b, h: (rb + b, cols["at_v"] // hd + h)),
                pl.BlockSpec((1, hd), lambda b, h: (0, 0)),
                pl.BlockSpec((1, hd), lambda b, h: (0, 0))]
    args = [proj, proj, proj, q_g.reshape(1, hd), k_g.reshape(1, hd)]
    if rope:
        cos_np, sin_np = _rope_tables(seq, hd)
        in_specs += [pl.BlockSpec((seq, hd), lambda b, h: (0, 0))] * 2
        args += [jnp.asarray(cos_np), jnp.asarray(sin_np)]
    has_ctx = ctx is not None
    if has_ctx:
        ck, cv = ctx
        past = ck.shape[2]
        in_specs += [pl.BlockSpec((1, 1, past, hd), lambda b, h: (b, layer, 0, h))] * 2
        args += [ck, cv]
    y_shape, aliases = _shared_rows_out(proj.shape[0], ATT_HEADS * hd, y_prev, in_specs, args)
    n_prev = len(aliases)
    out_shape = [y_shape]
    out_specs = [pl.BlockSpec((seq, gw), lambda b, h: (rb + b, h))]
    if emit_kv:
        depth, kv_prev = emit_kv
        out_shape += [jax.ShapeDtypeStruct((nb, depth, seq, ATT_KV_HEADS * hd), F32)] * 2
        out_specs += [pl.BlockSpec((None, None, seq, hd), lambda b, h: (b, layer, 0, h))] * 2
        if kv_prev is not None:
            for k, prev in enumerate(kv_prev):
                in_specs.append(pl.BlockSpec(memory_space=pl.ANY))
                args.append(prev)
                aliases[len(args) - 1] = 1 + k
            n_prev += len(kv_prev)
    outs = pl.pallas_call(
        functools.partial(_attn_kernel, rope=rope, has_ctx=has_ctx, has_prev=n_prev,
                          emit_kv=bool(emit_kv), groups=groups, bq=min(seq, ATT_BLOCK_Q), scale=float(hd) ** -0.5),
        out_shape=tuple(out_shape),
        grid=(nb, ATT_KV_HEADS),
        in_specs=in_specs,
        out_specs=tuple(out_specs),
        input_output_aliases=aliases,
        compiler_params=_cparams("parallel", "parallel"),
        name="attention",
    )(*args)
    if emit_kv:
        return outs
    return outs[0], None, None


def _branch_kernel(y0_ref, y1_ref, y2_ref, g0_ref, g1_ref, g2_ref, w_ref, o_ref):
    acc = None
    for n, (y_ref, g_ref) in enumerate(((y0_ref, g0_ref), (y1_ref, g1_ref), (y2_ref, g2_ref))):
        p = jnp.dot(y_ref[...], w_ref[n], preferred_element_type=F32)
        term = jax.nn.sigmoid(g_ref[...]) * p
        acc = term if acc is None else acc + term
    o_ref[...] = acc.astype(o_ref.dtype)


def _branch_merge(ys, proj, cols, w_branch, layer, tm, tn):
    t, bw = ys[0].shape
    d = w_branch.shape[3]
    goff = cols["br_g"]
    y_spec = pl.BlockSpec((tm, bw), lambda i, j: (i, 0))

    def g_spec(n):
        return pl.BlockSpec((tm, tn), lambda i, j, n=n: (i, (goff + n * d) // tn + j))

    return pl.pallas_call(
        _branch_kernel,
        out_shape=jax.ShapeDtypeStruct((t, d), BF16),
        grid=(t // tm, d // tn),
        in_specs=[y_spec, y_spec, y_spec, g_spec(0), g_spec(1), g_spec(2),
                  pl.BlockSpec((None, N_BRANCH, bw, tn), lambda i, j: (layer, 0, 0, j))],
        out_specs=pl.BlockSpec((tm, tn), lambda i, j: (i, j)),
        compiler_params=_cparams("parallel", "arbitrary"),
        name="branch_merge",
    )(*ys, proj, proj, proj, w_branch)


def _proj_residual_kernel(a_ref, w_ref, x_ref, mod_ref, o_ref, *, gate_idx):
    y = jnp.dot(a_ref[...], w_ref[...], preferred_element_type=F32)
    o_ref[...] = x_ref[...] + mod_ref[0, gate_idx:gate_idx + 1, :] * y


def _proj_residual(a, w, layer, x, mod, seq_row, *, gate_idx, tm, tn):
    t, k = a.shape
    d = w.shape[2]
    return pl.pallas_call(
        functools.partial(_proj_residual_kernel, gate_idx=gate_idx),
        out_shape=jax.ShapeDtypeStruct((t, d), F32),
        grid=(t // tm, d // tn),
        in_specs=[pl.BlockSpec((tm, k), lambda i, j: (i, 0)),
                  pl.BlockSpec((None, k, tn), lambda i, j: (layer, 0, j)),
                  pl.BlockSpec((tm, tn), lambda i, j: (i, j)),
                  pl.BlockSpec((1, 6, tn), lambda i, j: (seq_row(i), 0, j))],
        out_specs=pl.BlockSpec((tm, tn), lambda i, j: (i, j)),
        compiler_params=_cparams("parallel", "arbitrary"),
        name="proj_residual",
    )(a, w, x, mod)


def _topk_rows(s, k, exact_ties):
    rows = s.shape[0]
    iota = lax.broadcasted_iota(jnp.int32, s.shape, 0).astype(F32)
    rank = jnp.full(s.shape, float(k), F32)
    vals = []
    for r in range(k):
        mx = jnp.max(s, axis=0, keepdims=True)
        sel = s == mx
        if exact_ties:
            sel = iota == jnp.min(jnp.where(sel, iota, float(rows)), axis=0, keepdims=True)
        rank = jnp.where(sel, float(r), rank)
        vals.append(mx)
        s = jnp.where(sel, -jnp.inf, s)
    count = jnp.sum(jnp.where(rank < float(k), 1.0, 0.0), axis=0, keepdims=True)
    return jnp.concatenate(vals, axis=0), rank, count


def _route_head(s1, s2, exact_ties):
    k = PEER_TOPK
    sv1, rk1, c1 = _topk_rows(s1, k, exact_ties)
    sv2, rk2, c2 = _topk_rows(s2, k, exact_ties)
    row8 = lax.broadcasted_iota(jnp.int32, (8, s1.shape[1]), 0)
    groups = [sv1[0:1, :] + sv2, sv1[1:2, :] + sv2[0:8, :]]
    for p in range(2, 8):
        groups.append(jnp.where(row8 < k // (p + 1), sv1[p:p + 1, :] + sv2[0:8, :], -jnp.inf))
    groups.append(sv1[8:16, :] + sv2[0:1, :])
    fv, rkc, c3 = _topk_rows(jnp.concatenate(groups, axis=0), k, exact_ties)
    z = jnp.sum(jnp.exp(fv - fv[0:1, :]), axis=0, keepdims=True)
    sel = jnp.where(rkc < float(k), 1.0, 0.0)
    cnt_i = jnp.zeros_like(s1)
    starts = [0, 16] + [24 + 8 * (p - 2) for p in range(2, 8)]
    sizes = [16, 8] + [8] * 6
    for p in range(k):
        if p < 8:
            cnt_p = jnp.sum(sel[starts[p]:starts[p] + sizes[p], :], axis=0, keepdims=True)
        else:
            cnt_p = sel[72 + p - 8:72 + p - 7, :]
        cnt_i = cnt_i + jnp.where(rk1 == float(p), cnt_p, 0.0)
    u1 = jnp.exp(s1 - sv1[0:1, :]) / z
    u2 = jnp.exp(s2 - sv2[0:1, :])
    return u1, cnt_i, u2, rk2, jnp.max(jnp.maximum(jnp.maximum(c1, c2), c3))


def _peer_route_kernel(q_ref, keys_ref, u1_ref, cnt_ref, u2_ref, rk2_ref):
    dh = keys_ref.shape[3]
    s1 = _nt_dot(keys_ref[0, 0], q_ref[:, 0:dh])
    s2 = _nt_dot(keys_ref[0, 1], q_ref[:, dh:2 * dh])

    def emit(u1, cnt, u2, rk2):
        u1_ref[0] = u1
        cnt_ref[0] = cnt
        u2_ref[0] = u2.astype(u2_ref.dtype)
        rk2_ref[0] = rk2.astype(rk2_ref.dtype)

    *fast, most = _route_head(s1, s2, exact_ties=False)
    emit(*fast)

    @pl.when(most > float(PEER_TOPK))
    def _():
        emit(*_route_head(s1, s2, exact_ties=True)[:4])


def _peer_route(q, keys_bf, tr):
    t = q.shape[0]
    heads, _, nkeys, dh = keys_bf.shape
    sd = jax.ShapeDtypeStruct((heads, nkeys, t), F32)
    sd16 = jax.ShapeDtypeStruct((heads, nkeys, t), BF16)
    spec = pl.BlockSpec((1, nkeys, tr), lambda i, h: (h, 0, i))
    return pl.pallas_call(
        _peer_route_kernel,
        out_shape=(sd, sd, sd16, sd16),
        grid=(t // tr, heads),
        in_specs=[pl.BlockSpec((tr, 2 * dh), lambda i, h: (i, h)),
                  pl.BlockSpec((1, 2, nkeys, dh), lambda i, h: (h, 0, 0, 0))],
        out_specs=(spec, spec, spec, spec),
        compiler_params=_cparams("parallel", "parallel"),
        name="peer_route",
    )(q, keys_bf)


def _peer_dense_kernel(ht_ref, u_ref, vt_ref, u1_ref, cnt_ref, u2_ref, rk2_ref, o_ref, w_ref, *, sub, group):
    e = pl.program_id(1)

    @pl.when(e == 0)
    def _():
        o_ref[...] = jnp.zeros_like(o_ref)

    te = u_ref.shape[0]
    nkeys, tm = u2_ref.shape[1:]
    ktiles = nkeys // BF16_ROWS
    for i in range(te // nkeys):
        w = None
        for h in range(PEER_HEADS):
            u1 = jnp.broadcast_to(u1_ref[h, i:i + 1, :], (BF16_ROWS, tm)).astype(BF16)[None]
            cn = jnp.broadcast_to(cnt_ref[h, i:i + 1, :], (BF16_ROWS, tm)).astype(BF16)[None]
            rk = rk2_ref[h].reshape(ktiles, BF16_ROWS, tm)
            u2 = u2_ref[h].reshape(ktiles, BF16_ROWS, tm)
            term = u1 * jnp.where(rk < cn, u2, jnp.zeros((), BF16))
            w = term if w is None else w + term
        w_ref[i * nkeys:(i + 1) * nkeys, :] = w.reshape(nkeys, tm)

    ht = ht_ref[...]
    nsub = te // sub
    pre = [jnp.dot(u_ref[s * sub:(s + 1) * sub, :], ht, preferred_element_type=F32) for s in range(nsub)]
    pending = []
    for s in range(nsub):
        act = jax.nn.gelu(pre[s].astype(BF16), approximate=True)
        pending.append(w_ref[s * sub:(s + 1) * sub, :] * act)
        if len(pending) == group or s + 1 == nsub:
            lo = (s + 1 - len(pending)) * sub
            wa = pending[0] if len(pending) == 1 else jnp.concatenate(pending, axis=0)
            o_ref[...] += jnp.dot(vt_ref[:, lo:(s + 1) * sub], wa, preferred_element_type=F32)
            pending = []


def _peer_dense(h2t, u_bf, vt_bf, layer, route, *, tm, te, sub, group):
    d, t = h2t.shape
    ne = u_bf.shape[1]
    heads, nkeys, _ = route[0].shape
    once = pl.Buffered(1)
    r_spec = pl.BlockSpec((heads, nkeys, tm), lambda i, e: (0, 0, i), pipeline_mode=once)
    k_spec = pl.BlockSpec((heads, te // nkeys, tm), lambda i, e: (0, e, i))
    return pl.pallas_call(
        functools.partial(_peer_dense_kernel, sub=sub, group=group),
        out_shape=jax.ShapeDtypeStruct((d, t), F32),
        grid=(t // tm, ne // te),
        scratch_shapes=[pltpu.VMEM((te, tm), BF16)],
        in_specs=[pl.BlockSpec((d, tm), lambda i, e: (0, i), pipeline_mode=once),
                  pl.BlockSpec((None, te, d), lambda i, e: (layer, e, 0)),
                  pl.BlockSpec((None, d, te), lambda i, e: (layer, 0, e)),
                  k_spec, k_spec, r_spec, r_spec],
        out_specs=pl.BlockSpec((d, tm), lambda i, e: (0, i)),
        compiler_params=_cparams("parallel", "arbitrary"),
        name="peer_dense",
    )(h2t, u_bf, vt_bf, *route)


def _peer_residual_kernel(x_ref, yt_ref, mod_ref, fg_ref, o_ref, *, final_norm):
    y = x_ref[...] + mod_ref[0, 5:6, :] * yt_ref[...].T
    if final_norm:
        y = y * lax.rsqrt(jnp.mean(y * y, axis=-1, keepdims=True) + EPS) * fg_ref[...]
    o_ref[...] = y


def _peer_residual(x, yt, mod, final_g, seq_row, *, final_norm, tm, first_tile=0, ntiles=None):
    t, d = x.shape
    ntiles = t // tm if ntiles is None else ntiles
    return pl.pallas_call(
        functools.partial(_peer_residual_kernel, final_norm=final_norm),
        out_shape=jax.ShapeDtypeStruct((ntiles * tm, d), F32),
        grid=(ntiles,),
        in_specs=[pl.BlockSpec((tm, d), lambda i: (first_tile + i, 0)),
                  pl.BlockSpec((d, tm), lambda i: (0, first_tile + i)),
                  pl.BlockSpec((1, 6, d), lambda i: (seq_row(first_tile + i), 0, 0)),
                  pl.BlockSpec((1, d), lambda i: (0, 0))],
        out_specs=pl.BlockSpec((tm, d), lambda i: (i, 0)),
        compiler_params=_cparams("parallel"),
        name="peer_residual",
    )(x, yt, mod, final_g.reshape(1, d))


def _in_layout(bw, kvw, d):
    sizes = (("ml_q", bw), ("ml_k", bw), ("ml_v", bw), ("ml_o", bw), ("hy", 3 * bw),
             ("at_q", bw), ("at_k", kvw), ("at_v", kvw), ("br_g", N_BRANCH * d), ("ml_g", GATE_PAD))
    cols, off = {}, 0
    for name, size in sizes:
        cols[name] = off
        off += size
    return cols, off


def _pack_w_in(w_in, bw):
    depth, d, _ = w_in.shape
    ngate = 4 * ML_HEADS
    gates = w_in[:, :, 4 * bw:4 * bw + ngate].reshape(depth, d, 4, ML_HEADS)
    gates = jnp.swapaxes(gates, 2, 3)
    gates = jnp.pad(gates, ((0, 0), (0, 0), (0, 0), (0, GATE_PAD // ML_HEADS - 4))).reshape(depth, d, GATE_PAD)
    return jnp.concatenate([w_in[:, :, :4 * bw], w_in[:, :, 4 * bw + ngate:], gates], axis=2).astype(BF16)


def kernel(x_prompt, x_sample, c, cache_k, cache_v, state_C, state_n, state_m, c_ctx, w_ada, b_ada, norm1_g, norm2_g, w_in, ml_gate_bias, ml_norm_g, hy_conv_w, hy_w1, hy_b1, hy_w2, hy_b2, hy_w3, hy_decay, hy_skip, q_norm_g, k_norm_g, w_branch, w_out, peer_wq, peer_keys, peer_u, peer_v, final_g):
    nbp, seq_p, d = x_prompt.shape
    nbs, seq_s, _ = x_sample.shape
    depth = w_ada.shape[0]
    bw = d // 2
    kvw = cache_k.shape[3] * cache_k.shape[4]
    tp, ts = nbp * seq_p, nbs * seq_s
    cols, ncols = _in_layout(bw, kvw, d)
    assert w_in.shape[2] == ncols - GATE_PAD + 4 * ML_HEADS

    tm = _tile(seq_s, TOKEN_TILE)
    tn = TOKEN_TILE
    assert tp % tm == 0 and PEER_TOPK == 16
    tiles_p, tiles_per_seq = tp // tm, seq_s // tm

    def seq_row(i):
        return jnp.where(i < tiles_p, 0, 1 + (i - tiles_p) // tiles_per_seq)

    x = jnp.concatenate([x_prompt.reshape(tp, d), x_sample.reshape(ts, d)], axis=0)
    cond = jnp.concatenate([c_ctx[None, :], c], axis=0)
    nrow = cond.shape[0]
    cond = jnp.pad(cond, ((0, (-nrow) % 8), (0, 0)))
    mod_all = _ada(cond, w_ada, b_ada).reshape(depth, cond.shape[0], 6, d)

    ctx_k = cache_k.reshape(nbs, depth, cache_k.shape[2], kvw)
    ctx_v = cache_v.reshape(nbs, depth, cache_v.shape[2], kvw)
    dft = {}
    for seq in {seq_p, seq_s}:
        fwd_np, inv_np = _dft_mats(seq)
        fwd_hi = jnp.asarray(fwd_np).astype(BF16)
        fwd_lo = (jnp.asarray(fwd_np) - fwd_hi.astype(F32)).astype(BF16)
        dft[seq] = (fwd_hi, fwd_lo, jnp.asarray(inv_np).astype(BF16))

    w_in_bf = _pack_w_in(w_in, bw)
    w_branch_bf, w_out_bf, wq_bf = w_branch.astype(BF16), w_out.astype(BF16), peer_wq.astype(BF16)
    u_bf = peer_u.astype(BF16)
    vt_bf = jnp.swapaxes(peer_v, 1, 2).astype(BF16)
    keys_bf = peer_keys.astype(BF16)

    new_state = new_kv = None
    for l in range(depth):
        mod = mod_all[l]
        proj = _modnorm_matmul(x, mod, norm1_g[l], w_in_bf, l, seq_row, shift_idx=0, out_dtype=F32,
                               tm=tm, tn=_tile(ncols, tn))[0]
        gb = ml_gate_bias[l].reshape(4, ML_HEADS).T
        gb = jnp.pad(gb, ((0, 0), (0, LANES - 4))).reshape(ML_HEADS, 1, LANES)

        y_ml = y_hy = y_at = None
        for (row0, nb, seq, is_ctx) in ((0, nbp, seq_p, True), (tp, nbs, seq_s, False)):
            state = None if is_ctx else (state_C[:, l], state_n[:, l], state_m[:, l])
            y_ml, st = _mlstm(proj, cols, row0, nb, seq, gb, ml_norm_g[l], state,
                              emit_state=(l, depth, new_state) if is_ctx else None, y_prev=y_ml)
            fwd_bf, fwd_lo, inv_bf = dft[seq]
            filt = _hyena_filters(seq, hy_w1[l], hy_b1[l], hy_w2[l], hy_b2[l], hy_w3[l], hy_decay[l], fwd_bf, fwd_lo)
            y_hy = _hyena(proj, cols, row0, nb, seq, hy_conv_w[l], filt, hy_skip[l], fwd_bf, inv_bf, y_prev=y_hy)
            y_at, k_l, v_l = _attention(proj, cols, row0, nb, seq, q_norm_g[l], k_norm_g[l],
                                        None if is_ctx else (ctx_k, ctx_v), l, rope=not is_ctx,
                                        emit_kv=(depth, new_kv) if is_ctx else None, y_prev=y_at)
            if is_ctx:
                new_state, new_kv = st, (k_l, v_l)
        mixed = _branch_merge((y_ml, y_hy, y_at), proj, cols, w_branch_bf, l, tm, tn // 2)
        x = _proj_residual(mixed, w_out_bf, l, x, mod, seq_row, gate_idx=2, tm=tm, tn=tn)

        q, _, h2t = _modnorm_matmul(x, mod, norm2_g[l], wq_bf, l, seq_row, shift_idx=3,
                                    out_dtype=BF16, tm=tm, tn=tn, emit_ht=True)
        route = _peer_route(q, keys_bf[l], _tile(tm, ROUTE_TILE))
        yt = _peer_dense(h2t, u_bf, vt_bf, l, route, tm=_tile(tm, EXPERT_TOKEN_TILE), te=EXPERT_TILE,
                         sub=EXPERT_SUB, group=EXPERT_GROUP)
        if l < depth - 1:
            x = _peer_residual(x, yt, mod, final_g, seq_row, final_norm=False, tm=tm)

    y_prompt = _peer_residual(x, yt, mod, final_g, seq_row, final_norm=True, tm=tm,
                              first_tile=0, ntiles=tiles_p).reshape(nbp, seq_p, d)
    y_sample = _peer_residual(x, yt, mod, final_g, seq_row, final_norm=True, tm=tm,
                              first_tile=tiles_p, ntiles=ts // tm).reshape(nbs, seq_s, d)
    kv_shape = (nbp, depth, seq_p, ATT_KV_HEADS, kvw // ATT_KV_HEADS)
    new_c, new_n, new_m = new_state
    return (y_prompt, y_sample, new_kv[0].reshape(kv_shape), new_kv[1].reshape(kv_shape),
            new_c, new_n[:, :, :, :, 0, :], new_m[:, :, :, :, 0, 0])
```

```python
import functools

import numpy as np
import jax
import jax.numpy as jnp
from jax import lax
from jax.experimental import pallas as pl
from jax.experimental.pallas import tpu as pltpu

F32 = jnp.float32
BF16 = jnp.bfloat16
EPS = 1e-6
LOG2E = 1.4426950408889634

GRID_W = 64
ML_HEADS = 4
ATT_HEADS = 8
ATT_KV_HEADS = 2
ROPE_BASE = 10000.0
HY_BANDS = 16
HY_ORDER = 2
N_BRANCH = 3
PEER_HEADS = 8
PEER_TOPK = 16

VMEM_LIMIT_BYTES = 60 * 1024 * 1024
LANES = 128
BF16_ROWS = 16
ML_CHUNK = 512
ATT_BLOCK_Q = 256
FILTER_TILE = 512
TOKEN_TILE = 1024
ROUTE_TILE = 512
EXPERT_TILE = 2048
EXPERT_SUB = 256
EXPERT_GROUP = 8
EXPERT_TOKEN_TILE = 512
GATE_PAD = 512


def _cparams(*sem):
    return pltpu.CompilerParams(dimension_semantics=sem, vmem_limit_bytes=VMEM_LIMIT_BYTES)


def _tile(n, pref):
    t = min(n, pref)
    while n % t:
        t //= 2
    return t


def _nt_dot(a, b):
    return lax.dot_general(a, b, (((1,), (1,)), ((), ())), preferred_element_type=F32)


def _ada_kernel(c_ref, w_ref, b_ref, o_ref):
    c = c_ref[...]
    a = c * jax.nn.sigmoid(c)
    o_ref[0] = jnp.dot(a.astype(BF16), w_ref[0].astype(BF16), preferred_element_type=F32) + b_ref[0]


def _ada(cond, w_ada, b_ada):
    depth, d, n = w_ada.shape
    r = cond.shape[0]
    tn = _tile(n, 1024)
    return pl.pallas_call(
        _ada_kernel,
        out_shape=jax.ShapeDtypeStruct((depth, r, n), F32),
        grid=(depth, n // tn),
        in_specs=[pl.BlockSpec((r, d), lambda l, j: (0, 0)),
                  pl.BlockSpec((1, d, tn), lambda l, j: (l, 0, j)),
                  pl.BlockSpec((1, 1, tn), lambda l, j: (l, 0, j))],
        out_specs=pl.BlockSpec((1, r, tn), lambda l, j: (l, 0, j)),
        compiler_params=_cparams("parallel", "parallel"),
        name="ada",
    )(cond, w_ada, b_ada.reshape(depth, 1, n))


def _modnorm_matmul_kernel(x_ref, mod_ref, g_ref, w_ref, o_ref, h_ref, *ht_ref, shift_idx):
    @pl.when(pl.program_id(1) == 0)
    def _():
        x = x_ref[...]
        y = x * lax.rsqrt(jnp.mean(x * x, axis=-1, keepdims=True) + EPS) * g_ref[...]
        sh = mod_ref[0, shift_idx:shift_idx + 1, :]
        sc = mod_ref[0, shift_idx + 1:shift_idx + 2, :]
        h = y * (1.0 + sc) + sh
        h_ref[...] = h.astype(BF16)
        if ht_ref:
            ht_ref[0][...] = h.T.astype(BF16)

    o_ref[...] = jnp.dot(h_ref[...], w_ref[...], preferred_element_type=F32).astype(o_ref.dtype)


def _modnorm_matmul(x, mod, g, w, layer, seq_row, *, shift_idx, out_dtype, tm, tn, emit_ht=False):
    t, d = x.shape
    n = w.shape[2]
    out_shape = [jax.ShapeDtypeStruct((t, n), out_dtype), jax.ShapeDtypeStruct((t, d), BF16)]
    out_specs = [pl.BlockSpec((tm, tn), lambda i, j: (i, j)), pl.BlockSpec((tm, d), lambda i, j: (i, 0))]
    if emit_ht:
        out_shape.append(jax.ShapeDtypeStruct((d, t), BF16))
        out_specs.append(pl.BlockSpec((d, tm), lambda i, j: (0, i)))
    return pl.pallas_call(
        functools.partial(_modnorm_matmul_kernel, shift_idx=shift_idx),
        out_shape=tuple(out_shape),
        grid=(t // tm, n // tn),
        in_specs=[pl.BlockSpec((tm, d), lambda i, j: (i, 0)),
                  pl.BlockSpec((1, 6, d), lambda i, j: (seq_row(i), 0, 0)),
                  pl.BlockSpec((1, d), lambda i, j: (0, 0)),
                  pl.BlockSpec((None, d, tn), lambda i, j: (layer, 0, j))],
        out_specs=tuple(out_specs),
        compiler_params=_cparams("parallel", "arbitrary"),
        name="modnorm_matmul",
    )(x, mod, g.reshape(1, d), w)


def _log_sigmoid(x):
    return jnp.minimum(x, 0.0) - jnp.log1p(jnp.exp(-jnp.abs(x)))


def _mlstm_chunk(qc, kc, vc, lf_col, ig_col, lf_row, ig_row, C, n, m, fwd, k_scale):
    tc = qc.shape[0]
    r = lax.broadcasted_iota(jnp.int32, (tc, tc), 0)
    c = lax.broadcasted_iota(jnp.int32, (tc, tc), 1)
    mask = (c <= r) if fwd else (c >= r)
    mask_t = (r <= c) if fwd else (r >= c)
    b_col = jnp.sum(jnp.where(mask, lf_row, 0.0), axis=1, keepdims=True)
    b_row = jnp.sum(jnp.where(mask_t, lf_col, 0.0), axis=0, keepdims=True)
    dmat = jnp.where(mask, b_col - b_row + ig_row, -jnp.inf)
    m_t = jnp.maximum(b_col + m, jnp.max(dmat, axis=1, keepdims=True))
    w_in = jnp.exp(dmat - m_t)
    w_prev = jnp.exp(b_col + m - m_t)
    qb, kb, vb = qc.astype(BF16), kc.astype(BF16), vc.astype(BF16)
    s = _nt_dot(qb, kb) * k_scale * w_in
    num = (jnp.dot(s.astype(BF16), vb, preferred_element_type=F32)
           + w_prev * jnp.dot(qb, C.astype(BF16), preferred_element_type=F32))
    den = jnp.sum(s, axis=1, keepdims=True) + w_prev * jnp.sum(qc * n, axis=1, keepdims=True)
    h = num / jnp.maximum(jnp.abs(den), jnp.exp(-m_t))
    b_last = jnp.sum(lf_row, axis=1, keepdims=True)
    lw_col = b_last - b_col + ig_col
    lw_row = b_last - b_row + ig_row
    m_new = jnp.maximum(b_last + m, jnp.max(lw_row, axis=1, keepdims=True))
    decay = jnp.exp(b_last + m - m_new)
    kw = kc * k_scale * jnp.exp(lw_col - m_new)
    C_new = decay * C + jnp.dot(kw.T.astype(BF16), vb, preferred_element_type=F32)
    n_new = decay * n + jnp.sum(kw, axis=0, keepdims=True)
    return h, C_new, n_new, m_new


def _shared_rows_out(total_rows, width, prev, in_specs, args):
    aliases = {}
    if prev is not None:
        in_specs.append(pl.BlockSpec(memory_space=pl.ANY))
        args.append(prev)
        aliases = {len(args) - 1: 0}
    return jax.ShapeDtypeStruct((total_rows, width), BF16), aliases


def _mlstm_kernel(*refs, seq, chunk, has_state, has_prev, emit_state, k_scale):
    q_ref, k_ref, v_ref, o_ref, g_ref, bias_ref, ng_ref = refs[:7]
    pos = 7
    if has_state:
        c0_ref, n0_ref, m0_ref = refs[pos:pos + 3]
        pos += 3
    pos += has_prev
    y_ref = refs[pos]
    pos += 1
    if emit_state:
        c_out, n_out, m_out = refs[pos:pos + 3]
        pos += 3
    hacc = refs[pos]

    g = g_ref[...] + bias_ref[0]
    gt = g.T
    dk = q_ref.shape[1]
    nchunks = seq // chunk
    for d in range(2):
        fwd = d == 0
        ig_col_all = g[:, 2 * d:2 * d + 1]
        lf_col_all = _log_sigmoid(g[:, 2 * d + 1:2 * d + 2])
        ig_row_all = gt[2 * d:2 * d + 1, :]
        lf_row_all = _log_sigmoid(gt[2 * d + 1:2 * d + 2, :])
        if has_state:
            C = c0_ref[0, d, 0]
            n = n0_ref[0, d, 0]
            m = m0_ref[0, d, 0][:, 0:1]
        else:
            C = jnp.zeros((dk, v_ref.shape[1]), F32)
            n = jnp.zeros((1, dk), F32)
            m = jnp.zeros((1, 1), F32)
        order = range(nchunks) if fwd else range(nchunks - 1, -1, -1)
        for ci in order:
            lo, hi = ci * chunk, (ci + 1) * chunk
            h, C, n, m = _mlstm_chunk(
                q_ref[lo:hi, :], k_ref[lo:hi, :], v_ref[lo:hi, :],
                lf_col_all[lo:hi], ig_col_all[lo:hi], lf_row_all[:, lo:hi], ig_row_all[:, lo:hi],
                C, n, m, fwd, k_scale)
            if fwd:
                hacc[lo:hi, :] = h
            else:
                hacc[lo:hi, :] += h
        if emit_state:
            c_out[0, d, 0] = C
            n_out[0, d, 0] = n
            m_out[0, d, 0] = jnp.broadcast_to(m, (1, LANES))
    hh = hacc[...]
    hn = hh * lax.rsqrt(jnp.mean(hh * hh, axis=-1, keepdims=True) + EPS) * ng_ref[0]
    y_ref[...] = (jax.nn.sigmoid(o_ref[...]) * hn).astype(y_ref.dtype)


def _mlstm(proj, cols, row0, nb, seq, gate_bias, norm_g, state, emit_state, y_prev=None):
    heads = ML_HEADS
    dk = (cols["ml_k"] - cols["ml_q"]) // heads
    chunk = min(ML_CHUNK, seq)
    rb = row0 // seq

    def col_spec(off, width):
        return pl.BlockSpec((seq, width), lambda b, h, off=off, width=width: (rb + b, off // width + h))

    in_specs = [col_spec(cols["ml_q"], dk), col_spec(cols["ml_k"], dk), col_spec(cols["ml_v"], dk),
                col_spec(cols["ml_o"], dk), col_spec(cols["ml_g"], LANES),
                pl.BlockSpec((1, 1, LANES), lambda b, h: (h, 0, 0)),
                pl.BlockSpec((1, 1, dk), lambda b, h: (h, 0, 0))]
    args = [proj, proj, proj, proj, proj, gate_bias, norm_g.reshape(heads, 1, dk)]
    has_state = state is not None
    if has_state:
        c0, n0, m0 = state
        in_specs += [pl.BlockSpec((1, 2, 1, dk, dk), lambda b, h: (b, 0, h, 0, 0)),
                     pl.BlockSpec((1, 2, 1, 1, dk), lambda b, h: (b, 0, h, 0, 0)),
                     pl.BlockSpec((1, 2, 1, 1, LANES), lambda b, h: (b, 0, h, 0, 0))]
        args += [c0, n0.reshape(nb, 2, heads, 1, dk),
                 jnp.broadcast_to(m0[..., None, None], (nb, 2, heads, 1, LANES))]
    y_shape, aliases = _shared_rows_out(proj.shape[0], heads * dk, y_prev, in_specs, args)
    n_prev = len(aliases)
    out_shape = [y_shape]
    out_specs = [pl.BlockSpec((seq, dk), lambda b, h: (rb + b, h))]
    if emit_state:
        layer, depth, state_prev = emit_state
        out_shape += [jax.ShapeDtypeStruct((nb, depth, 2, heads, dk, dk), F32),
                      jax.ShapeDtypeStruct((nb, depth, 2, heads, 1, dk), F32),
                      jax.ShapeDtypeStruct((nb, depth, 2, heads, 1, LANES), F32)]
        out_specs += [pl.BlockSpec((1, None, 2, 1, dk, dk), lambda b, h: (b, layer, 0, h, 0, 0)),
                      pl.BlockSpec((1, None, 2, 1, 1, dk), lambda b, h: (b, layer, 0, h, 0, 0)),
                      pl.BlockSpec((1, None, 2, 1, 1, LANES), lambda b, h: (b, layer, 0, h, 0, 0))]
        if state_prev is not None:
            for k, prev in enumerate(state_prev):
                in_specs.append(pl.BlockSpec(memory_space=pl.ANY))
                args.append(prev)
                aliases[len(args) - 1] = 1 + k
            n_prev += len(state_prev)
    outs = pl.pallas_call(
        functools.partial(_mlstm_kernel, seq=seq, chunk=chunk, has_state=has_state,
                          has_prev=n_prev, emit_state=bool(emit_state), k_scale=float(dk) ** -0.5),
        out_shape=tuple(out_shape),
        grid=(nb, heads),
        in_specs=in_specs,
        out_specs=tuple(out_specs),
        input_output_aliases=aliases,
        scratch_shapes=[pltpu.VMEM((seq, dk), F32)],
        compiler_params=_cparams("parallel", "parallel"),
        name="mlstm",
    )(*args)
    if emit_state:
        return outs[0], tuple(outs[1:])
    return outs[0], None


def _dft_mats(seq):
    k = np.arange(seq, dtype=np.int64)
    ang = np.pi * ((k[:, None] * k[None, :]) % (2 * seq)).astype(np.float64) / seq
    cos, sin = np.cos(ang), np.sin(ang)
    alt = np.where(k % 2 == 0, 1.0, -1.0)
    fwd_b = -sin
    fwd_b[0, :] = alt
    fwd = np.concatenate([cos, fwd_b], axis=0)
    inv_a = cos.T / seq
    inv_a[:, 0] = 0.5 / seq
    inv_b = -sin.T / seq
    inv_b[:, 0] = alt * 0.5 / seq
    inv = np.concatenate([inv_a, inv_b], axis=1)
    return fwd.astype(np.float32), inv.astype(np.float32)


def _hy_features(seq):
    pos = np.arange(seq, dtype=np.float64)
    t = pos / (seq - 1)
    bands = np.arange(1, HY_BANDS + 1, dtype=np.float64)
    ang = (2.0 * np.pi / seq) * pos[:, None] * bands[None, :]
    feat = np.concatenate([t[:, None], np.cos(ang), np.sin(ang)], axis=-1)
    feat = np.pad(feat, ((0, 0), (0, LANES - feat.shape[1])))
    return feat.astype(np.float32), t.astype(np.float32)[:, None]


def _dot_split(w_hi, w_lo, g):
    g_hi = g.astype(BF16)
    g_lo = (g - g_hi.astype(F32)).astype(BF16)
    return (jnp.dot(w_hi, g_hi, preferred_element_type=F32)
            + (jnp.dot(w_hi, g_lo, preferred_element_type=F32) + jnp.dot(w_lo, g_hi, preferred_element_type=F32)))


def _hyfilt_kernel(feat_ref, t_ref, w1_ref, b1_ref, w2_ref, b2_ref, w3f_ref, w3b_ref, decf_ref, decb_ref,
                   fwd_ref, fwd_lo_ref, p_ref, fi_ref, s_ref):
    hp = lax.Precision.HIGHEST
    h = jnp.sin(jnp.dot(feat_ref[...], w1_ref[...], precision=hp, preferred_element_type=F32) + b1_ref[...])
    h = jnp.sin(jnp.dot(h, w2_ref[...], precision=hp, preferred_element_type=F32) + b2_ref[...])
    t = t_ref[...]
    seq = t.shape[0]
    row = lax.broadcasted_iota(jnp.int32, (seq, 1), 0)
    hf = jnp.dot(h, w3f_ref[...], precision=hp, preferred_element_type=F32) * jnp.exp(-t * decf_ref[0, 0])
    hb = jnp.dot(h, w3b_ref[...], precision=hp, preferred_element_type=F32) * jnp.exp(-t * decb_ref[0, 0])
    hb = jnp.where(row == 0, 0.0, hb)
    nrm = lax.rsqrt(jnp.sum(hf * hf, axis=0, keepdims=True) + jnp.sum(hb * hb, axis=0, keepdims=True) + EPS)
    gp = (hf + hb) * nrm
    gm = (hf - hb) * nrm
    fa = _dot_split(fwd_ref[0:seq, :], fwd_lo_ref[0:seq, :], gp)
    fb = _dot_split(fwd_ref[seq:2 * seq, :], fwd_lo_ref[seq:2 * seq, :], gm)
    alt = jnp.where(row % 2 == 0, 1.0, -1.0)
    f_nyq = jnp.sum(alt * gp, axis=0, keepdims=True)
    p_ref[0] = fa
    fi_ref[0] = jnp.where(row == 0, 0.0, fb)
    s_ref[0] = jnp.where(row == 0, f_nyq, fa)


def _hyena_filters(seq, w1, b1, w2, b2, w3, decay, fwd_hi, fwd_lo):
    feat_np, t_np = _hy_features(seq)
    nfeat, ffn = w1.shape
    width = decay.shape[-1]
    pf = LANES - ffn
    w1p = jnp.pad(w1, ((0, LANES - nfeat), (0, pf)))
    w2p = jnp.pad(w2, ((0, pf), (0, pf)))
    w3p = jnp.pad(w3, ((0, pf), (0, 0)))
    b1p = jnp.pad(b1, (0, pf)).reshape(1, LANES)
    b2p = jnp.pad(b2, (0, pf)).reshape(1, LANES)
    ct = _tile(width, FILTER_TILE)
    nct = width // ct
    dec = decay.reshape(HY_ORDER * 2, 1, width)
    full = lambda shape: pl.BlockSpec(shape, lambda o, j: (0,) * len(shape), pipeline_mode=pl.Buffered(1))
    out_sd = jax.ShapeDtypeStruct((HY_ORDER, seq, width), F32)
    out_spec = pl.BlockSpec((1, seq, ct), lambda o, j: (o, 0, j))
    return pl.pallas_call(
        _hyfilt_kernel,
        out_shape=(out_sd, out_sd, out_sd),
        grid=(HY_ORDER, nct),
        in_specs=[full(feat_np.shape), full(t_np.shape), full(w1p.shape), full(b1p.shape),
                  full(w2p.shape), full(b2p.shape),
                  pl.BlockSpec((LANES, ct), lambda o, j: (0, o * 2 * nct + j)),
                  pl.BlockSpec((LANES, ct), lambda o, j: (0, (o * 2 + 1) * nct + j)),
                  pl.BlockSpec((1, 1, ct), lambda o, j: (o * 2, 0, j)),
                  pl.BlockSpec((1, 1, ct), lambda o, j: (o * 2 + 1, 0, j)),
                  full(fwd_hi.shape), full(fwd_lo.shape)],
        out_specs=(out_spec, out_spec, out_spec),
        compiler_params=_cparams("parallel", "parallel"),
        name="hyena_filters",
    )(jnp.asarray(feat_np), jnp.asarray(t_np), w1p, b1p, w2p, b2p, w3p, w3p, dec, dec, fwd_hi, fwd_lo)


def _short_conv(u, w):
    seq = u.shape[0]
    row = lax.broadcasted_iota(jnp.int32, (seq, 1), 0)
    prev = jnp.where(row == 0, 0.0, pltpu.roll(u, 1, 0))
    nxt = jnp.where(row == seq - 1, 0.0, pltpu.roll(u, seq - 1, 0))
    return prev * w[0:1, :] + u * w[1:2, :] + nxt * w[2:3, :]


def _hyena_kernel(uv_ref, u1_ref, u2_ref, cwv_ref, cw1_ref, cw2_ref, fwd_ref, inv_ref,
                  p_ref, fi_ref, s_ref, skip_ref, *rest):
    y_ref = rest[-1]
    seq = uv_ref.shape[0]
    z = _short_conv(uv_ref[...], cwv_ref[...])
    gates = (_short_conv(u1_ref[...], cw1_ref[...]), _short_conv(u2_ref[...], cw2_ref[...]))
    for order in range(HY_ORDER):
        zf = jnp.dot(fwd_ref[...], z.astype(BF16), preferred_element_type=F32)
        a, b = zf[:seq], zf[seq:]
        p, fi, s = p_ref[order], fi_ref[order], s_ref[order]
        ya = a * p - b * fi
        yb = a * fi + b * s
        conv = (jnp.dot(inv_ref[:, :seq], ya.astype(BF16), preferred_element_type=F32)
                + jnp.dot(inv_ref[:, seq:], yb.astype(BF16), preferred_element_type=F32))
        z = gates[order] * (conv + skip_ref[order:order + 1, :] * z)
    y_ref[...] = z.astype(y_ref.dtype)


def _hyena_channel_tile(seq, width):
    const = 2 * (2 * seq * seq * 2)
    per_channel = 20 * seq * 4
    fit = (VMEM_LIMIT_BYTES * 5 // 6 - const) // per_channel
    ct = LANES
    while ct * 2 <= min(fit, width):
        ct *= 2
    return _tile(width, ct)


def _hyena(proj, cols, row0, nb, seq, conv_w, filt, skip, fwd_bf, inv_bf, y_prev=None):
    width = skip.shape[-1]
    ct = _hyena_channel_tile(seq, width)
    nct = width // ct
    rb = row0 // seq
    off = cols["hy"]
    p_arr, fi_arr, s_arr = filt

    def u_spec(part):
        return pl.BlockSpec((seq, ct), lambda j, b, part=part: (rb + b, off // ct + part * nct + j))

    def cw_spec(part):
        return pl.BlockSpec((3, ct), lambda j, b, part=part: (0, part * nct + j))

    once = pl.Buffered(1)
    full = lambda shape: pl.BlockSpec(shape, lambda j, b: (0,) * len(shape), pipeline_mode=once)
    f_spec = pl.BlockSpec((HY_ORDER, seq, ct), lambda j, b: (0, 0, j), pipeline_mode=once)
    in_specs = [u_spec(0), u_spec(1), u_spec(2), cw_spec(0), cw_spec(1), cw_spec(2),
                full(fwd_bf.shape), full(inv_bf.shape), f_spec, f_spec, f_spec,
                pl.BlockSpec((HY_ORDER, ct), lambda j, b: (0, j))]
    args = [proj, proj, proj, conv_w, conv_w, conv_w, fwd_bf, inv_bf, p_arr, fi_arr, s_arr, skip]
    y_shape, aliases = _shared_rows_out(proj.shape[0], width, y_prev, in_specs, args)
    return pl.pallas_call(
        _hyena_kernel,
        out_shape=y_shape,
        grid=(nct, nb),
        in_specs=in_specs,
        out_specs=pl.BlockSpec((seq, ct), lambda j, b: (rb + b, j)),
        input_output_aliases=aliases,
        compiler_params=_cparams("parallel", "parallel"),
        name="hyena",
    )(*args)


def _rope_tables(seq, head_dim):
    nfreq = head_dim // 4
    rows = seq // GRID_W
    row = np.repeat(np.arange(rows, dtype=np.float64), GRID_W)
    col = np.tile(np.arange(GRID_W, dtype=np.float64), rows)
    inv = (ROPE_BASE ** (-2.0 * np.arange(nfreq, dtype=np.float32) / (2 * nfreq))).astype(np.float64)
    ar, ac = row[:, None] * inv, col[:, None] * inv
    cos = np.concatenate([np.cos(ar), np.cos(ar), np.cos(ac), np.cos(ac)], axis=1)
    sin = np.concatenate([-np.sin(ar), np.sin(ar), -np.sin(ac), np.sin(ac)], axis=1)
    return cos.astype(np.float32), sin.astype(np.float32)


def _rope(x, cos, sin):
    hd = x.shape[1]
    q = hd // 4
    lane = lax.broadcasted_iota(jnp.int32, x.shape, 1)
    first = (lane % (2 * q)) < q
    partner = jnp.where(first, pltpu.roll(x, hd - q, 1), pltpu.roll(x, q, 1))
    return x * cos + partner * sin


def _attn_kernel(*refs, rope, has_ctx, has_prev, emit_kv, groups, bq, scale):
    q_ref, k_ref, v_ref, qg_ref, kg_ref = refs[:5]
    pos = 5
    if rope:
        cos_ref, sin_ref = refs[pos:pos + 2]
        pos += 2
    if has_ctx:
        kc_ref, vc_ref = refs[pos:pos + 2]
        pos += 2
    pos += has_prev
    y_ref = refs[pos]
    pos += 1
    if emit_kv:
        ko_ref, vo_ref = refs[pos:pos + 2]

    seq, hd = k_ref.shape
    k = k_ref[...]
    kn = k * lax.rsqrt(jnp.mean(k * k, axis=-1, keepdims=True) + EPS) * kg_ref[...]
    v = v_ref[...]
    if emit_kv:
        ko_ref[...] = kn
        vo_ref[...] = v
    if rope:
        kn = _rope(kn, cos_ref[...], sin_ref[...])
    kb, vb = kn.astype(BF16), v.astype(BF16)
    if has_ctx:
        kcb, vcb = kc_ref[0, 0].astype(BF16), vc_ref[0, 0].astype(BF16)
    for g in range(groups):
        for qi in range(seq // bq):
            lo, hi = qi * bq, (qi + 1) * bq
            q = q_ref[lo:hi, g * hd:(g + 1) * hd]
            qn = q * lax.rsqrt(jnp.mean(q * q, axis=-1, keepdims=True) + EPS) * qg_ref[...]
            if rope:
                qn = _rope(qn, cos_ref[lo:hi, :], sin_ref[lo:hi, :])
            qb = (qn * (scale * LOG2E)).astype(BF16)
            s1 = _nt_dot(qb, kb)
            mx = jnp.max(s1, axis=-1, keepdims=True)
            if has_ctx:
                s2 = _nt_dot(qb, kcb)
                mx = jnp.maximum(mx, jnp.max(s2, axis=-1, keepdims=True))
            p1 = jnp.exp2(s1 - mx)
            den = jnp.sum(p1, axis=-1, keepdims=True)
            o = jnp.dot(p1.astype(BF16), vb, preferred_element_type=F32)
            if has_ctx:
                p2 = jnp.exp2(s2 - mx)
                den = den + jnp.sum(p2, axis=-1, keepdims=True)
                o = o + jnp.dot(p2.astype(BF16), vcb, preferred_element_type=F32)
            y_ref[lo:hi, g * hd:(g + 1) * hd] = (o / den).astype(y_ref.dtype)


def _attention(proj, cols, row0, nb, seq, q_g, k_g, ctx, layer, rope, emit_kv, y_prev=None):
    hd = (cols["at_v"] - cols["at_k"]) // ATT_KV_HEADS
    groups = ATT_HEADS // ATT_KV_HEADS
    gw = groups * hd
    rb = row0 // seq
    in_specs = [pl.BlockSpec((seq, gw), lambda b, h: (rb + b, cols["at_q"] // gw + h)),
                pl.BlockSpec((seq, hd), lambda b, h: (rb + b, cols["at_k"] // hd + h)),
                pl.BlockSpec((seq, hd), lambda b, h: (rb + b, cols["at_v"] // hd + h)),
                pl.BlockSpec((1, hd), lambda b, h: (0, 0)),
                pl.BlockSpec((1, hd), lambda b, h: (0, 0))]
    args = [proj, proj, proj, q_g.reshape(1, hd), k_g.reshape(1, hd)]
    if rope:
        cos_np, sin_np = _rope_tables(seq, hd)
        in_specs += [pl.BlockSpec((seq, hd), lambda b, h: (0, 0))] * 2
        args += [jnp.asarray(cos_np), jnp.asarray(sin_np)]
    has_ctx = ctx is not None
    if has_ctx:
        ck, cv = ctx
        past = ck.shape[2]
        in_specs += [pl.BlockSpec((1, 1, past, hd), lambda b, h: (b, layer, 0, h))] * 2
        args += [ck, cv]
    y_shape, aliases = _shared_rows_out(proj.shape[0], ATT_HEADS * hd, y_prev, in_specs, args)
    n_prev = len(aliases)
    out_shape = [y_shape]
    out_specs = [pl.BlockSpec((seq, gw), lambda b, h: (rb + b, h))]
    if emit_kv:
        depth, kv_prev = emit_kv
        out_shape += [jax.ShapeDtypeStruct((nb, depth, seq, ATT_KV_HEADS * hd), F32)] * 2
        out_specs += [pl.BlockSpec((None, None, seq, hd), lambda b, h: (b, layer, 0, h))] * 2
        if kv_prev is not None:
            for k, prev in enumerate(kv_prev):
                in_specs.append(pl.BlockSpec(memory_space=pl.ANY))
                args.append(prev)
                aliases[len(args) - 1] = 1 + k
            n_prev += len(kv_prev)
    outs = pl.pallas_call(
        functools.partial(_attn_kernel, rope=rope, has_ctx=has_ctx, has_prev=n_prev,
                          emit_kv=bool(emit_kv), groups=groups, bq=min(seq, ATT_BLOCK_Q), scale=float(hd) ** -0.5),
        out_shape=tuple(out_shape),
        grid=(nb, ATT_KV_HEADS),
        in_specs=in_specs,
        out_specs=tuple(out_specs),
        input_output_aliases=aliases,
        compiler_params=_cparams("parallel", "parallel"),
        name="attention",
    )(*args)
    if emit_kv:
        return outs
    return outs[0], None, None


def _branch_kernel(y0_ref, y1_ref, y2_ref, g0_ref, g1_ref, g2_ref, w_ref, o_ref):
    acc = None
    for n, (y_ref, g_ref) in enumerate(((y0_ref, g0_ref), (y1_ref, g1_ref), (y2_ref, g2_ref))):
        p = jnp.dot(y_ref[...], w_ref[n], preferred_element_type=F32)
        term = jax.nn.sigmoid(g_ref[...]) * p
        acc = term if acc is None else acc + term
    o_ref[...] = acc.astype(o_ref.dtype)


def _branch_merge(ys, proj, cols, w_branch, layer, tm, tn):
    t, bw = ys[0].shape
    d = w_branch.shape[3]
    goff = cols["br_g"]
    y_spec = pl.BlockSpec((tm, bw), lambda i, j: (i, 0))

    def g_spec(n):
        return pl.BlockSpec((tm, tn), lambda i, j, n=n: (i, (goff + n * d) // tn + j))

    return pl.pallas_call(
        _branch_kernel,
        out_shape=jax.ShapeDtypeStruct((t, d), BF16),
        grid=(t // tm, d // tn),
        in_specs=[y_spec, y_spec, y_spec, g_spec(0), g_spec(1), g_spec(2),
                  pl.BlockSpec((None, N_BRANCH, bw, tn), lambda i, j: (layer, 0, 0, j))],
        out_specs=pl.BlockSpec((tm, tn), lambda i, j: (i, j)),
        compiler_params=_cparams("parallel", "arbitrary"),
        name="branch_merge",
    )(*ys, proj, proj, proj, w_branch)


def _proj_residual_kernel(a_ref, w_ref, x_ref, mod_ref, o_ref, *, gate_idx):
    y = jnp.dot(a_ref[...], w_ref[...], preferred_element_type=F32)
    o_ref[...] = x_ref[...] + mod_ref[0, gate_idx:gate_idx + 1, :] * y


def _proj_residual(a, w, layer, x, mod, seq_row, *, gate_idx, tm, tn):
    t, k = a.shape
    d = w.shape[2]
    return pl.pallas_call(
        functools.partial(_proj_residual_kernel, gate_idx=gate_idx),
        out_shape=jax.ShapeDtypeStruct((t, d), F32),
        grid=(t // tm, d // tn),
        in_specs=[pl.BlockSpec((tm, k), lambda i, j: (i, 0)),
                  pl.BlockSpec((None, k, tn), lambda i, j: (layer, 0, j)),
                  pl.BlockSpec((tm, tn), lambda i, j: (i, j)),
                  pl.BlockSpec((1, 6, tn), lambda i, j: (seq_row(i), 0, j))],
        out_specs=pl.BlockSpec((tm, tn), lambda i, j: (i, j)),
        compiler_params=_cparams("parallel", "arbitrary"),
        name="proj_residual",
    )(a, w, x, mod)


def _topk_rows(s, k, exact_ties):
    rows = s.shape[0]
    iota = lax.broadcasted_iota(jnp.int32, s.shape, 0).astype(F32)
    rank = jnp.full(s.shape, float(k), F32)
    vals = []
    for r in range(k):
        mx = jnp.max(s, axis=0, keepdims=True)
        sel = s == mx
        if exact_ties:
            sel = iota == jnp.min(jnp.where(sel, iota, float(rows)), axis=0, keepdims=True)
        rank = jnp.where(sel, float(r), rank)
        vals.append(mx)
        s = jnp.where(sel, -jnp.inf, s)
    count = jnp.sum(jnp.where(rank < float(k), 1.0, 0.0), axis=0, keepdims=True)
    return jnp.concatenate(vals, axis=0), rank, count


def _route_head(s1, s2, exact_ties):
    k = PEER_TOPK
    sv1, rk1, c1 = _topk_rows(s1, k, exact_ties)
    sv2, rk2, c2 = _topk_rows(s2, k, exact_ties)
    row8 = lax.broadcasted_iota(jnp.int32, (8, s1.shape[1]), 0)
    groups = [sv1[0:1, :] + sv2, sv1[1:2, :] + sv2[0:8, :]]
    for p in range(2, 8):
        groups.append(jnp.where(row8 < k // (p + 1), sv1[p:p + 1, :] + sv2[0:8, :], -jnp.inf))
    groups.append(sv1[8:16, :] + sv2[0:1, :])
    fv, rkc, c3 = _topk_rows(jnp.concatenate(groups, axis=0), k, exact_ties)
    z = jnp.sum(jnp.exp(fv - fv[0:1, :]), axis=0, keepdims=True)
    sel = jnp.where(rkc < float(k), 1.0, 0.0)
    cnt_i = jnp.zeros_like(s1)
    starts = [0, 16] + [24 + 8 * (p - 2) for p in range(2, 8)]
    sizes = [16, 8] + [8] * 6
    for p in range(k):
        if p < 8:
            cnt_p = jnp.sum(sel[starts[p]:starts[p] + sizes[p], :], axis=0, keepdims=True)
        else:
            cnt_p = sel[72 + p - 8:72 + p - 7, :]
        cnt_i = cnt_i + jnp.where(rk1 == float(p), cnt_p, 0.0)
    u1 = jnp.exp(s1 - sv1[0:1, :]) / z
    u2 = jnp.exp(s2 - sv2[0:1, :])
    return u1, cnt_i, u2, rk2, jnp.max(jnp.maximum(jnp.maximum(c1, c2), c3))


def _peer_route_kernel(q_ref, keys_ref, u1_ref, cnt_ref, u2_ref, rk2_ref):
    dh = keys_ref.shape[3]
    s1 = _nt_dot(keys_ref[0, 0], q_ref[:, 0:dh])
    s2 = _nt_dot(keys_ref[0, 1], q_ref[:, dh:2 * dh])

    def emit(u1, cnt, u2, rk2):
        u1_ref[0] = u1
        cnt_ref[0] = cnt
        u2_ref[0] = u2.astype(u2_ref.dtype)
        rk2_ref[0] = rk2.astype(rk2_ref.dtype)

    *fast, most = _route_head(s1, s2, exact_ties=False)
    emit(*fast)

    @pl.when(most > float(PEER_TOPK))
    def _():
        emit(*_route_head(s1, s2, exact_ties=True)[:4])


def _peer_route(q, keys_bf, tr):
    t = q.shape[0]
    heads, _, nkeys, dh = keys_bf.shape
    sd = jax.ShapeDtypeStruct((heads, nkeys, t), F32)
    sd16 = jax.ShapeDtypeStruct((heads, nkeys, t), BF16)
    spec = pl.BlockSpec((1, nkeys, tr), lambda i, h: (h, 0, i))
    return pl.pallas_call(
        _peer_route_kernel,
        out_shape=(sd, sd, sd16, sd16),
        grid=(t // tr, heads),
        in_specs=[pl.BlockSpec((tr, 2 * dh), lambda i, h: (i, h)),
                  pl.BlockSpec((1, 2, nkeys, dh), lambda i, h: (h, 0, 0, 0))],
        out_specs=(spec, spec, spec, spec),
        compiler_params=_cparams("parallel", "parallel"),
        name="peer_route",
    )(q, keys_bf)


def _peer_dense_kernel(ht_ref, u_ref, vt_ref, u1_ref, cnt_ref, u2_ref, rk2_ref, o_ref, w_ref, *, sub, group):
    e = pl.program_id(1)

    @pl.when(e == 0)
    def _():
        o_ref[...] = jnp.zeros_like(o_ref)

    te = u_ref.shape[0]
    nkeys, tm = u2_ref.shape[1:]
    ktiles = nkeys // BF16_ROWS
    for i in range(te // nkeys):
        w = None
        for h in range(PEER_HEADS):
            u1 = jnp.broadcast_to(u1_ref[h, i:i + 1, :], (BF16_ROWS, tm)).astype(BF16)[None]
            cn = jnp.broadcast_to(cnt_ref[h, i:i + 1, :], (BF16_ROWS, tm)).astype(BF16)[None]
            rk = rk2_ref[h].reshape(ktiles, BF16_ROWS, tm)
            u2 = u2_ref[h].reshape(ktiles, BF16_ROWS, tm)
            term = u1 * jnp.where(rk < cn, u2, jnp.zeros((), BF16))
            w = term if w is None else w + term
        w_ref[i * nkeys:(i + 1) * nkeys, :] = w.reshape(nkeys, tm)

    ht = ht_ref[...]
    nsub = te // sub
    pre = [jnp.dot(u_ref[s * sub:(s + 1) * sub, :], ht, preferred_element_type=F32) for s in range(nsub)]
    pending = []
    for s in range(nsub):
        act = jax.nn.gelu(pre[s].astype(BF16), approximate=True)
        pending.append(w_ref[s * sub:(s + 1) * sub, :] * act)
        if len(pending) == group or s + 1 == nsub:
            lo = (s + 1 - len(pending)) * sub
            wa = pending[0] if len(pending) == 1 else jnp.concatenate(pending, axis=0)
            o_ref[...] += jnp.dot(vt_ref[:, lo:(s + 1) * sub], wa, preferred_element_type=F32)
            pending = []


def _peer_dense(h2t, u_bf, vt_bf, layer, route, *, tm, te, sub, group):
    d, t = h2t.shape
    ne = u_bf.shape[1]
    heads, nkeys, _ = route[0].shape
    once = pl.Buffered(1)
    r_spec = pl.BlockSpec((heads, nkeys, tm), lambda i, e: (0, 0, i), pipeline_mode=once)
    k_spec = pl.BlockSpec((heads, te // nkeys, tm), lambda i, e: (0, e, i))
    return pl.pallas_call(
        functools.partial(_peer_dense_kernel, sub=sub, group=group),
        out_shape=jax.ShapeDtypeStruct((d, t), F32),
        grid=(t // tm, ne // te),
        scratch_shapes=[pltpu.VMEM((te, tm), BF16)],
        in_specs=[pl.BlockSpec((d, tm), lambda i, e: (0, i), pipeline_mode=once),
                  pl.BlockSpec((None, te, d), lambda i, e: (layer, e, 0)),
                  pl.BlockSpec((None, d, te), lambda i, e: (layer, 0, e)),
                  k_spec, k_spec, r_spec, r_spec],
        out_specs=pl.BlockSpec((d, tm), lambda i, e: (0, i)),
        compiler_params=_cparams("parallel", "arbitrary"),
        name="peer_dense",
    )(h2t, u_bf, vt_bf, *route)


def _peer_residual_kernel(x_ref, yt_ref, mod_ref, fg_ref, o_ref, *, final_norm):
    y = x_ref[...] + mod_ref[0, 5:6, :] * yt_ref[...].T
    if final_norm:
        y = y * lax.rsqrt(jnp.mean(y * y, axis=-1, keepdims=True) + EPS) * fg_ref[...]
    o_ref[...] = y


def _peer_residual(x, yt, mod, final_g, seq_row, *, final_norm, tm, first_tile=0, ntiles=None):
    t, d = x.shape
    ntiles = t // tm if ntiles is None else ntiles
    return pl.pallas_call(
        functools.partial(_peer_residual_kernel, final_norm=final_norm),
        out_shape=jax.ShapeDtypeStruct((ntiles * tm, d), F32),
        grid=(ntiles,),
        in_specs=[pl.BlockSpec((tm, d), lambda i: (first_tile + i, 0)),
                  pl.BlockSpec((d, tm), lambda i: (0, first_tile + i)),
                  pl.BlockSpec((1, 6, d), lambda i: (seq_row(first_tile + i), 0, 0)),
                  pl.BlockSpec((1, d), lambda i: (0, 0))],
        out_specs=pl.BlockSpec((tm, d), lambda i: (i, 0)),
        compiler_params=_cparams("parallel"),
        name="peer_residual",
    )(x, yt, mod, final_g.reshape(1, d))


def _in_layout(bw, kvw, d):
    sizes = (("ml_q", bw), ("ml_k", bw), ("ml_v", bw), ("ml_o", bw), ("hy", 3 * bw),
             ("at_q", bw), ("at_k", kvw), ("at_v", kvw), ("br_g", N_BRANCH * d), ("ml_g", GATE_PAD))
    cols, off = {}, 0
    for name, size in sizes:
        cols[name] = off
        off += size
    return cols, off


def _pack_w_in(w_in, bw):
    depth, d, _ = w_in.shape
    ngate = 4 * ML_HEADS
    gates = w_in[:, :, 4 * bw:4 * bw + ngate].reshape(depth, d, 4, ML_HEADS)
    gates = jnp.swapaxes(gates, 2, 3)
    gates = jnp.pad(gates, ((0, 0), (0, 0), (0, 0), (0, GATE_PAD // ML_HEADS - 4))).reshape(depth, d, GATE_PAD)
    return jnp.concatenate([w_in[:, :, :4 * bw], w_in[:, :, 4 * bw + ngate:], gates], axis=2).astype(BF16)


def kernel(x_prompt, x_sample, c, cache_k, cache_v, state_C, state_n, state_m, c_ctx, w_ada, b_ada, norm1_g, norm2_g, w_in, ml_gate_bias, ml_norm_g, hy_conv_w, hy_w1, hy_b1, hy_w2, hy_b2, hy_w3, hy_decay, hy_skip, q_norm_g, k_norm_g, w_branch, w_out, peer_wq, peer_keys, peer_u, peer_v, final_g):
    nbp, seq_p, d = x_prompt.shape
    nbs, seq_s, _ = x_sample.shape
    depth = w_ada.shape[0]
    bw = d // 2
    kvw = cache_k.shape[3] * cache_k.shape[4]
    tp, ts = nbp * seq_p, nbs * seq_s
    cols, ncols = _in_layout(bw, kvw, d)
    assert w_in.shape[2] == ncols - GATE_PAD + 4 * ML_HEADS

    tm = _tile(seq_s, TOKEN_TILE)
    tn = TOKEN_TILE
    assert tp % tm == 0 and PEER_TOPK == 16
    tiles_p, tiles_per_seq = tp // tm, seq_s // tm

    def seq_row(i):
        return jnp.where(i < tiles_p, 0, 1 + (i - tiles_p) // tiles_per_seq)

    x = jnp.concatenate([x_prompt.reshape(tp, d), x_sample.reshape(ts, d)], axis=0)
    cond = jnp.concatenate([c_ctx[None, :], c], axis=0)
    nrow = cond.shape[0]
    cond = jnp.pad(cond, ((0, (-nrow) % 8), (0, 0)))
    mod_all = _ada(cond, w_ada, b_ada).reshape(depth, cond.shape[0], 6, d)

    ctx_k = cache_k.reshape(nbs, depth, cache_k.shape[2], kvw)
    ctx_v = cache_v.reshape(nbs, depth, cache_v.shape[2], kvw)
    dft = {}
    for seq in {seq_p, seq_s}:
        fwd_np, inv_np = _dft_mats(seq)
        fwd_hi = jnp.asarray(fwd_np).astype(BF16)
        fwd_lo = (jnp.asarray(fwd_np) - fwd_hi.astype(F32)).astype(BF16)
        dft[seq] = (fwd_hi, fwd_lo, jnp.asarray(inv_np).astype(BF16))

    w_in_bf = _pack_w_in(w_in, bw)
    w_branch_bf, w_out_bf, wq_bf = w_branch.astype(BF16), w_out.astype(BF16), peer_wq.astype(BF16)
    u_bf = peer_u.astype(BF16)
    vt_bf = jnp.swapaxes(peer_v, 1, 2).astype(BF16)
    keys_bf = peer_keys.astype(BF16)

    new_state = new_kv = None
    for l in range(depth):
        mod = mod_all[l]
        proj = _modnorm_matmul(x, mod, norm1_g[l], w_in_bf, l, seq_row, shift_idx=0, out_dtype=F32,
                               tm=tm, tn=_tile(ncols, tn))[0]
        gb = ml_gate_bias[l].reshape(4, ML_HEADS).T
        gb = jnp.pad(gb, ((0, 0), (0, LANES - 4))).reshape(ML_HEADS, 1, LANES)

        y_ml = y_hy = y_at = None
        for (row0, nb, seq, is_ctx) in ((0, nbp, seq_p, True), (tp, nbs, seq_s, False)):
            state = None if is_ctx else (state_C[:, l], state_n[:, l], state_m[:, l])
            y_ml, st = _mlstm(proj, cols, row0, nb, seq, gb, ml_norm_g[l], state,
                              emit_state=(l, depth, new_state) if is_ctx else None, y_prev=y_ml)
            fwd_bf, fwd_lo, inv_bf = dft[seq]
            filt = _hyena_filters(seq, hy_w1[l], hy_b1[l], hy_w2[l], hy_b2[l], hy_w3[l], hy_decay[l], fwd_bf, fwd_lo)
            y_hy = _hyena(proj, cols, row0, nb, seq, hy_conv_w[l], filt, hy_skip[l], fwd_bf, inv_bf, y_prev=y_hy)
            y_at, k_l, v_l = _attention(proj, cols, row0, nb, seq, q_norm_g[l], k_norm_g[l],
                                        None if is_ctx else (ctx_k, ctx_v), l, rope=not is_ctx,
                                        emit_kv=(depth, new_kv) if is_ctx else None, y_prev=y_at)
            if is_ctx:
                new_state, new_kv = st, (k_l, v_l)
        mixed = _branch_merge((y_ml, y_hy, y_at), proj, cols, w_branch_bf, l, tm, tn // 2)
        x = _proj_residual(mixed, w_out_bf, l, x, mod, seq_row, gate_idx=2, tm=tm, tn=tn)

        q, _, h2t = _modnorm_matmul(x, mod, norm2_g[l], wq_bf, l, seq_row, shift_idx=3,
                                    out_dtype=BF16, tm=tm, tn=tn, emit_ht=True)
        route = _peer_route(q, keys_bf[l], _tile(tm, ROUTE_TILE))
        yt = _peer_dense(h2t, u_bf, vt_bf, l, route, tm=_tile(tm, EXPERT_TOKEN_TILE), te=EXPERT_TILE,
                         sub=EXPERT_SUB, group=EXPERT_GROUP)
        if l < depth - 1:
            x = _peer_residual(x, yt, mod, final_g, seq_row, final_norm=False, tm=tm)

    y_prompt = _peer_residual(x, yt, mod, final_g, seq_row, final_norm=True, tm=tm,
                              first_tile=0, ntiles=tiles_p).reshape(nbp, seq_p, d)
    y_sample = _peer_residual(x, yt, mod, final_g, seq_row, final_norm=True, tm=tm,
                              first_tile=tiles_p, ntiles=ts // tm).reshape(nbs, seq_s, d)
    kv_shape = (nbp, depth, seq_p, ATT_KV_HEADS, kvw // ATT_KV_HEADS)
    new_c, new_n, new_m = new_state
    return (y_prompt, y_sample, new_kv[0].reshape(kv_shape), new_kv[1].reshape(kv_shape),
            new_c, new_n[:, :, :, :, 0, :], new_m[:, :, :, :, 0, 0])
```

```python
import functools

import numpy as np
import jax
import jax.numpy as jnp
from jax import lax
from jax.experimental import pallas as pl
from jax.experimental.pallas import tpu as pltpu

F32 = jnp.float32
BF16 = jnp.bfloat16
EPS = 1e-6
LOG2E = 1.4426950408889634

GRID_W = 64
ML_HEADS = 4
ATT_HEADS = 8
ATT_KV_HEADS = 2
ROPE_BASE = 10000.0
HY_BANDS = 16
HY_ORDER = 2
N_BRANCH = 3
PEER_HEADS = 8
PEER_TOPK = 16

VMEM_LIMIT_BYTES = 60 * 1024 * 1024
LANES = 128
BF16_ROWS = 16
ML_CHUNK = 512
ATT_BLOCK_Q = 256
FILTER_TILE = 512
TOKEN_TILE = 1024
IN_PROJ_TILE = 1536
ROUTE_TILE = 512
EXPERT_TILE = 2048
EXPERT_SUB = 256
EXPERT_GROUP = 8
EXPERT_TOKEN_TILE = 512
GATE_PAD = 512


def _cparams(*sem):
    return pltpu.CompilerParams(dimension_semantics=sem, vmem_limit_bytes=VMEM_LIMIT_BYTES)


def _tile(n, pref):
    t = min(n, pref)
    while n % t:
        t //= 2
    return t


def _nt_dot(a, b):
    return lax.dot_general(a, b, (((1,), (1,)), ((), ())), preferred_element_type=F32)


def _ada_kernel(c_ref, w_ref, b_ref, o_ref):
    c = c_ref[...]
    a = c * jax.nn.sigmoid(c)
    o_ref[0] = jnp.dot(a.astype(BF16), w_ref[0].astype(BF16), preferred_element_type=F32) + b_ref[0]


def _ada(cond, w_ada, b_ada):
    depth, d, n = w_ada.shape
    r = cond.shape[0]
    tn = _tile(n, 1024)
    return pl.pallas_call(
        _ada_kernel,
        out_shape=jax.ShapeDtypeStruct((depth, r, n), F32),
        grid=(depth, n // tn),
        in_specs=[pl.BlockSpec((r, d), lambda l, j: (0, 0)),
                  pl.BlockSpec((1, d, tn), lambda l, j: (l, 0, j)),
                  pl.BlockSpec((1, 1, tn), lambda l, j: (l, 0, j))],
        out_specs=pl.BlockSpec((1, r, tn), lambda l, j: (l, 0, j)),
        compiler_params=_cparams("parallel", "parallel"),
        name="ada",
    )(cond, w_ada, b_ada.reshape(depth, 1, n))


def _modnorm_matmul_kernel(x_ref, mod_ref, g_ref, w_ref, o_ref, h_ref, *ht_ref, shift_idx):
    @pl.when(pl.program_id(1) == 0)
    def _():
        x = x_ref[...]
        y = x * lax.rsqrt(jnp.mean(x * x, axis=-1, keepdims=True) + EPS) * g_ref[...]
        sh = mod_ref[0, shift_idx:shift_idx + 1, :]
        sc = mod_ref[0, shift_idx + 1:shift_idx + 2, :]
        h = y * (1.0 + sc) + sh
        h_ref[...] = h.astype(BF16)
        if ht_ref:
            ht_ref[0][...] = h.T.astype(BF16)

    o_ref[...] = jnp.dot(h_ref[...], w_ref[...], preferred_element_type=F32).astype(o_ref.dtype)


def _modnorm_matmul(x, mod, g, w, layer, seq_row, *, shift_idx, out_dtype, tm, tn, emit_ht=False):
    t, d = x.shape
    n = w.shape[2]
    out_shape = [jax.ShapeDtypeStruct((t, n), out_dtype), jax.ShapeDtypeStruct((t, d), BF16)]
    out_specs = [pl.BlockSpec((tm, tn), lambda i, j: (i, j)), pl.BlockSpec((tm, d), lambda i, j: (i, 0))]
    if emit_ht:
        out_shape.append(jax.ShapeDtypeStruct((d, t), BF16))
        out_specs.append(pl.BlockSpec((d, tm), lambda i, j: (0, i)))
    return pl.pallas_call(
        functools.partial(_modnorm_matmul_kernel, shift_idx=shift_idx),
        out_shape=tuple(out_shape),
        grid=(t // tm, n // tn),
        in_specs=[pl.BlockSpec((tm, d), lambda i, j: (i, 0)),
                  pl.BlockSpec((1, 6, d), lambda i, j: (seq_row(i), 0, 0)),
                  pl.BlockSpec((1, d), lambda i, j: (0, 0)),
                  pl.BlockSpec((None, d, tn), lambda i, j: (layer, 0, j))],
        out_specs=tuple(out_specs),
        compiler_params=_cparams("parallel", "arbitrary"),
        name="modnorm_matmul",
    )(x, mod, g.reshape(1, d), w)


def _log_sigmoid(x):
    return jnp.minimum(x, 0.0) - jnp.log1p(jnp.exp(-jnp.abs(x)))


def _mlstm_chunk(qc, kc, vc, lf_col, ig_col, lf_row, ig_row, C, n, m, fwd, k_scale):
    tc = qc.shape[0]
    r = lax.broadcasted_iota(jnp.int32, (tc, tc), 0)
    c = lax.broadcasted_iota(jnp.int32, (tc, tc), 1)
    mask = (c <= r) if fwd else (c >= r)
    mask_t = (r <= c) if fwd else (r >= c)
    b_col = jnp.sum(jnp.where(mask, lf_row, 0.0), axis=1, keepdims=True)
    b_row = jnp.sum(jnp.where(mask_t, lf_col, 0.0), axis=0, keepdims=True)
    dmat = jnp.where(mask, b_col - b_row + ig_row, -jnp.inf)
    m_t = jnp.maximum(b_col + m, jnp.max(dmat, axis=1, keepdims=True))
    w_in = jnp.exp(dmat - m_t)
    w_prev = jnp.exp(b_col + m - m_t)
    qb, kb, vb = qc.astype(BF16), kc.astype(BF16), vc.astype(BF16)
    s = _nt_dot(qb, kb) * k_scale * w_in
    num = (jnp.dot(s.astype(BF16), vb, preferred_element_type=F32)
           + w_prev * jnp.dot(qb, C.astype(BF16), preferred_element_type=F32))
    den = jnp.sum(s, axis=1, keepdims=True) + w_prev * jnp.sum(qc * n, axis=1, keepdims=True)
    h = num / jnp.maximum(jnp.abs(den), jnp.exp(-m_t))
    b_last = jnp.sum(lf_row, axis=1, keepdims=True)
    lw_col = b_last - b_col + ig_col
    lw_row = b_last - b_row + ig_row
    m_new = jnp.maximum(b_last + m, jnp.max(lw_row, axis=1, keepdims=True))
    decay = jnp.exp(b_last + m - m_new)
    kw = kc * k_scale * jnp.exp(lw_col - m_new)
    C_new = decay * C + jnp.dot(kw.T.astype(BF16), vb, preferred_element_type=F32)
    n_new = decay * n + jnp.sum(kw, axis=0, keepdims=True)
    return h, C_new, n_new, m_new


def _shared_rows_out(total_rows, width, prev, in_specs, args):
    aliases = {}
    if prev is not None:
        in_specs.append(pl.BlockSpec(memory_space=pl.ANY))
        args.append(prev)
        aliases = {len(args) - 1: 0}
    return jax.ShapeDtypeStruct((total_rows, width), BF16), aliases


def _mlstm_kernel(*refs, seq, chunk, has_state, has_prev, emit_state, k_scale):
    q_ref, k_ref, v_ref, o_ref, g_ref, bias_ref, ng_ref = refs[:7]
    pos = 7
    if has_state:
        c0_ref, n0_ref, m0_ref = refs[pos:pos + 3]
        pos += 3
    pos += has_prev
    y_ref = refs[pos]
    pos += 1
    if emit_state:
        c_out, n_out, m_out = refs[pos:pos + 3]
        pos += 3
    hacc = refs[pos]

    g = g_ref[...] + bias_ref[0]
    gt = g.T
    dk = q_ref.shape[1]
    nchunks = seq // chunk
    for d in range(2):
        fwd = d == 0
        ig_col_all = g[:, 2 * d:2 * d + 1]
        lf_col_all = _log_sigmoid(g[:, 2 * d + 1:2 * d + 2])
        ig_row_all = gt[2 * d:2 * d + 1, :]
        lf_row_all = _log_sigmoid(gt[2 * d + 1:2 * d + 2, :])
        if has_state:
            C = c0_ref[0, d, 0]
            n = n0_ref[0, d, 0]
            m = m0_ref[0, d, 0][:, 0:1]
        else:
            C = jnp.zeros((dk, v_ref.shape[1]), F32)
            n = jnp.zeros((1, dk), F32)
            m = jnp.zeros((1, 1), F32)
        order = range(nchunks) if fwd else range(nchunks - 1, -1, -1)
        for ci in order:
            lo, hi = ci * chunk, (ci + 1) * chunk
            h, C, n, m = _mlstm_chunk(
                q_ref[lo:hi, :], k_ref[lo:hi, :], v_ref[lo:hi, :],
                lf_col_all[lo:hi], ig_col_all[lo:hi], lf_row_all[:, lo:hi], ig_row_all[:, lo:hi],
                C, n, m, fwd, k_scale)
            if fwd:
                hacc[lo:hi, :] = h
            else:
                hacc[lo:hi, :] += h
        if emit_state:
            c_out[0, d, 0] = C
            n_out[0, d, 0] = n
            m_out[0, d, 0] = jnp.broadcast_to(m, (1, LANES))
    hh = hacc[...]
    hn = hh * lax.rsqrt(jnp.mean(hh * hh, axis=-1, keepdims=True) + EPS) * ng_ref[0]
    y_ref[...] = (jax.nn.sigmoid(o_ref[...]) * hn).astype(y_ref.dtype)


def _mlstm(proj, cols, row0, nb, seq, gate_bias, norm_g, state, emit_state, y_prev=None):
    heads = ML_HEADS
    dk = (cols["ml_k"] - cols["ml_q"]) // heads
    chunk = min(ML_CHUNK, seq)
    rb = row0 // seq

    def col_spec(off, width):
        return pl.BlockSpec((seq, width), lambda b, h, off=off, width=width: (rb + b, off // width + h))

    in_specs = [col_spec(cols["ml_q"], dk), col_spec(cols["ml_k"], dk), col_spec(cols["ml_v"], dk),
                col_spec(cols["ml_o"], dk), col_spec(cols["ml_g"], LANES),
                pl.BlockSpec((1, 1, LANES), lambda b, h: (h, 0, 0)),
                pl.BlockSpec((1, 1, dk), lambda b, h: (h, 0, 0))]
    args = [proj, proj, proj, proj, proj, gate_bias, norm_g.reshape(heads, 1, dk)]
    has_state = state is not None
    if has_state:
        c0, n0, m0 = state
        in_specs += [pl.BlockSpec((1, 2, 1, dk, dk), lambda b, h: (b, 0, h, 0, 0)),
                     pl.BlockSpec((1, 2, 1, 1, dk), lambda b, h: (b, 0, h, 0, 0)),
                     pl.BlockSpec((1, 2, 1, 1, LANES), lambda b, h: (b, 0, h, 0, 0))]
        args += [c0, n0.reshape(nb, 2, heads, 1, dk),
                 jnp.broadcast_to(m0[..., None, None], (nb, 2, heads, 1, LANES))]
    y_shape, aliases = _shared_rows_out(proj.shape[0], heads * dk, y_prev, in_specs, args)
    n_prev = len(aliases)
    out_shape = [y_shape]
    out_specs = [pl.BlockSpec((seq, dk), lambda b, h: (rb + b, h))]
    if emit_state:
        layer, depth, state_prev = emit_state
        out_shape += [jax.ShapeDtypeStruct((nb, depth, 2, heads, dk, dk), F32),
                      jax.ShapeDtypeStruct((nb, depth, 2, heads, 1, dk), F32),
                      jax.ShapeDtypeStruct((nb, depth, 2, heads, 1, LANES), F32)]
        out_specs += [pl.BlockSpec((1, None, 2, 1, dk, dk), lambda b, h: (b, layer, 0, h, 0, 0)),
                      pl.BlockSpec((1, None, 2, 1, 1, dk), lambda b, h: (b, layer, 0, h, 0, 0)),
                      pl.BlockSpec((1, None, 2, 1, 1, LANES), lambda b, h: (b, layer, 0, h, 0, 0))]
        if state_prev is not None:
            for k, prev in enumerate(state_prev):
                in_specs.append(pl.BlockSpec(memory_space=pl.ANY))
                args.append(prev)
                aliases[len(args) - 1] = 1 + k
            n_prev += len(state_prev)
    outs = pl.pallas_call(
        functools.partial(_mlstm_kernel, seq=seq, chunk=chunk, has_state=has_state,
                          has_prev=n_prev, emit_state=bool(emit_state), k_scale=float(dk) ** -0.5),
        out_shape=tuple(out_shape),
        grid=(nb, heads),
        in_specs=in_specs,
        out_specs=tuple(out_specs),
        input_output_aliases=aliases,
        scratch_shapes=[pltpu.VMEM((seq, dk), F32)],
        compiler_params=_cparams("parallel", "parallel"),
        name="mlstm",
    )(*args)
    if emit_state:
        return outs[0], tuple(outs[1:])
    return outs[0], None


def _dft_mats(seq):
    k = np.arange(seq, dtype=np.int64)
    ang = np.pi * ((k[:, None] * k[None, :]) % (2 * seq)).astype(np.float64) / seq
    cos, sin = np.cos(ang), np.sin(ang)
    alt = np.where(k % 2 == 0, 1.0, -1.0)
    fwd_b = -sin
    fwd_b[0, :] = alt
    fwd = np.concatenate([cos, fwd_b], axis=0)
    inv_a = cos.T / seq
    inv_a[:, 0] = 0.5 / seq
    inv_b = -sin.T / seq
    inv_b[:, 0] = alt * 0.5 / seq
    inv = np.concatenate([inv_a, inv_b], axis=1)
    return fwd.astype(np.float32), inv.astype(np.float32)


def _hy_features(seq):
    pos = np.arange(seq, dtype=np.float64)
    t = pos / (seq - 1)
    bands = np.arange(1, HY_BANDS + 1, dtype=np.float64)
    ang = (2.0 * np.pi / seq) * pos[:, None] * bands[None, :]
    feat = np.concatenate([t[:, None], np.cos(ang), np.sin(ang)], axis=-1)
    feat = np.pad(feat, ((0, 0), (0, LANES - feat.shape[1])))
    return feat.astype(np.float32), t.astype(np.float32)[:, None]


def _dot_split(w_hi, w_lo, g):
    g_hi = g.astype(BF16)
    g_lo = (g - g_hi.astype(F32)).astype(BF16)
    return (jnp.dot(w_hi, g_hi, preferred_element_type=F32)
            + (jnp.dot(w_hi, g_lo, preferred_element_type=F32) + jnp.dot(w_lo, g_hi, preferred_element_type=F32)))


def _hyfilt_kernel(feat_ref, t_ref, w1_ref, b1_ref, w2_ref, b2_ref, w3f_ref, w3b_ref, decf_ref, decb_ref,
                   fwd_ref, fwd_lo_ref, p_ref, fi_ref, s_ref):
    hp = lax.Precision.HIGHEST
    h = jnp.sin(jnp.dot(feat_ref[...], w1_ref[...], precision=hp, preferred_element_type=F32) + b1_ref[...])
    h = jnp.sin(jnp.dot(h, w2_ref[...], precision=hp, preferred_element_type=F32) + b2_ref[...])
    t = t_ref[...]
    seq = t.shape[0]
    row = lax.broadcasted_iota(jnp.int32, (seq, 1), 0)
    hf = jnp.dot(h, w3f_ref[...], precision=hp, preferred_element_type=F32) * jnp.exp(-t * decf_ref[0, 0])
    hb = jnp.dot(h, w3b_ref[...], precision=hp, preferred_element_type=F32) * jnp.exp(-t * decb_ref[0, 0])
    hb = jnp.where(row == 0, 0.0, hb)
    nrm = lax.rsqrt(jnp.sum(hf * hf, axis=0, keepdims=True) + jnp.sum(hb * hb, axis=0, keepdims=True) + EPS)
    gp = (hf + hb) * nrm
    gm = (hf - hb) * nrm
    fa = _dot_split(fwd_ref[0:seq, :], fwd_lo_ref[0:seq, :], gp)
    fb = _dot_split(fwd_ref[seq:2 * seq, :], fwd_lo_ref[seq:2 * seq, :], gm)
    alt = jnp.where(row % 2 == 0, 1.0, -1.0)
    f_nyq = jnp.sum(alt * gp, axis=0, keepdims=True)
    p_ref[0] = fa
    fi_ref[0] = jnp.where(row == 0, 0.0, fb)
    s_ref[0] = jnp.where(row == 0, f_nyq, fa)


def _hyena_filters(seq, w1, b1, w2, b2, w3, decay, fwd_hi, fwd_lo):
    feat_np, t_np = _hy_features(seq)
    nfeat, ffn = w1.shape
    width = decay.shape[-1]
    pf = LANES - ffn
    w1p = jnp.pad(w1, ((0, LANES - nfeat), (0, pf)))
    w2p = jnp.pad(w2, ((0, pf), (0, pf)))
    w3p = jnp.pad(w3, ((0, pf), (0, 0)))
    b1p = jnp.pad(b1, (0, pf)).reshape(1, LANES)
    b2p = jnp.pad(b2, (0, pf)).reshape(1, LANES)
    ct = _tile(width, FILTER_TILE)
    nct = width // ct
    dec = decay.reshape(HY_ORDER * 2, 1, width)
    full = lambda shape: pl.BlockSpec(shape, lambda o, j: (0,) * len(shape), pipeline_mode=pl.Buffered(1))
    out_sd = jax.ShapeDtypeStruct((HY_ORDER, seq, width), F32)
    out_spec = pl.BlockSpec((1, seq, ct), lambda o, j: (o, 0, j))
    return pl.pallas_call(
        _hyfilt_kernel,
        out_shape=(out_sd, out_sd, out_sd),
        grid=(HY_ORDER, nct),
        in_specs=[full(feat_np.shape), full(t_np.shape), full(w1p.shape), full(b1p.shape),
                  full(w2p.shape), full(b2p.shape),
                  pl.BlockSpec((LANES, ct), lambda o, j: (0, o * 2 * nct + j)),
                  pl.BlockSpec((LANES, ct), lambda o, j: (0, (o * 2 + 1) * nct + j)),
                  pl.BlockSpec((1, 1, ct), lambda o, j: (o * 2, 0, j)),
                  pl.BlockSpec((1, 1, ct), lambda o, j: (o * 2 + 1, 0, j)),
                  full(fwd_hi.shape), full(fwd_lo.shape)],
        out_specs=(out_spec, out_spec, out_spec),
        compiler_params=_cparams("parallel", "parallel"),
        name="hyena_filters",
    )(jnp.asarray(feat_np), jnp.asarray(t_np), w1p, b1p, w2p, b2p, w3p, w3p, dec, dec, fwd_hi, fwd_lo)


def _short_conv(u, w):
    seq = u.shape[0]
    row = lax.broadcasted_iota(jnp.int32, (seq, 1), 0)
    prev = jnp.where(row == 0, 0.0, pltpu.roll(u, 1, 0))
    nxt = jnp.where(row == seq - 1, 0.0, pltpu.roll(u, seq - 1, 0))
    return prev * w[0:1, :] + u * w[1:2, :] + nxt * w[2:3, :]


def _hyena_kernel(uv_ref, u1_ref, u2_ref, cwv_ref, cw1_ref, cw2_ref, fwd_ref, inv_ref,
                  p_ref, fi_ref, s_ref, skip_ref, *rest):
    y_ref = rest[-1]
    seq = uv_ref.shape[0]
    z = _short_conv(uv_ref[...], cwv_ref[...])
    gates = (_short_conv(u1_ref[...], cw1_ref[...]), _short_conv(u2_ref[...], cw2_ref[...]))
    for order in range(HY_ORDER):
        zf = jnp.dot(fwd_ref[...], z.astype(BF16), preferred_element_type=F32)
        a, b = zf[:seq], zf[seq:]
        p, fi, s = p_ref[order], fi_ref[order], s_ref[order]
        ya = a * p - b * fi
        yb = a * fi + b * s
        conv = (jnp.dot(inv_ref[:, :seq], ya.astype(BF16), preferred_element_type=F32)
                + jnp.dot(inv_ref[:, seq:], yb.astype(BF16), preferred_element_type=F32))
        z = gates[order] * (conv + skip_ref[order:order + 1, :] * z)
    y_ref[...] = z.astype(y_ref.dtype)


def _hyena_channel_tile(seq, width):
    const = 2 * (2 * seq * seq * 2)
    per_channel = 20 * seq * 4
    fit = (VMEM_LIMIT_BYTES * 5 // 6 - const) // per_channel
    ct = LANES
    while ct * 2 <= min(fit, width):
        ct *= 2
    return _tile(width, ct)


def _hyena(proj, cols, row0, nb, seq, conv_w, filt, skip, fwd_bf, inv_bf, y_prev=None):
    width = skip.shape[-1]
    ct = _hyena_channel_tile(seq, width)
    nct = width // ct
    rb = row0 // seq
    off = cols["hy"]
    p_arr, fi_arr, s_arr = filt

    def u_spec(part):
        return pl.BlockSpec((seq, ct), lambda j, b, part=part: (rb + b, off // ct + part * nct + j))

    def cw_spec(part):
        return pl.BlockSpec((3, ct), lambda j, b, part=part: (0, part * nct + j))

    once = pl.Buffered(1)
    full = lambda shape: pl.BlockSpec(shape, lambda j, b: (0,) * len(shape), pipeline_mode=once)
    f_spec = pl.BlockSpec((HY_ORDER, seq, ct), lambda j, b: (0, 0, j), pipeline_mode=once)
    in_specs = [u_spec(0), u_spec(1), u_spec(2), cw_spec(0), cw_spec(1), cw_spec(2),
                full(fwd_bf.shape), full(inv_bf.shape), f_spec, f_spec, f_spec,
                pl.BlockSpec((HY_ORDER, ct), lambda j, b: (0, j))]
    args = [proj, proj, proj, conv_w, conv_w, conv_w, fwd_bf, inv_bf, p_arr, fi_arr, s_arr, skip]
    y_shape, aliases = _shared_rows_out(proj.shape[0], width, y_prev, in_specs, args)
    return pl.pallas_call(
        _hyena_kernel,
        out_shape=y_shape,
        grid=(nct, nb),
        in_specs=in_specs,
        out_specs=pl.BlockSpec((seq, ct), lambda j, b: (rb + b, j)),
        input_output_aliases=aliases,
        compiler_params=_cparams("parallel", "parallel"),
        name="hyena",
    )(*args)


def _rope_tables(seq, head_dim):
    nfreq = head_dim // 4
    rows = seq // GRID_W
    row = np.repeat(np.arange(rows, dtype=np.float64), GRID_W)
    col = np.tile(np.arange(GRID_W, dtype=np.float64), rows)
    inv = (ROPE_BASE ** (-2.0 * np.arange(nfreq, dtype=np.float32) / (2 * nfreq))).astype(np.float64)
    ar, ac = row[:, None] * inv, col[:, None] * inv
    cos = np.concatenate([np.cos(ar), np.cos(ar), np.cos(ac), np.cos(ac)], axis=1)
    sin = np.concatenate([-np.sin(ar), np.sin(ar), -np.sin(ac), np.sin(ac)], axis=1)
    return cos.astype(np.float32), sin.astype(np.float32)


def _rope(x, cos, sin):
    hd = x.shape[1]
    q = hd // 4
    lane = lax.broadcasted_iota(jnp.int32, x.shape, 1)
    first = (lane % (2 * q)) < q
    partner = jnp.where(first, pltpu.roll(x, hd - q, 1), pltpu.roll(x, q, 1))
    return x * cos + partner * sin


def _attn_kernel(*refs, rope, has_ctx, has_prev, emit_kv, groups, bq, scale):
    q_ref, k_ref, v_ref, qg_ref, kg_ref = refs[:5]
    pos = 5
    if rope:
        cos_ref, sin_ref = refs[pos:pos + 2]
        pos += 2
    if has_ctx:
        kc_ref, vc_ref = refs[pos:pos + 2]
        pos += 2
    pos += has_prev
    y_ref = refs[pos]
    pos += 1
    if emit_kv:
        ko_ref, vo_ref = refs[pos:pos + 2]

    seq, hd = k_ref.shape
    k = k_ref[...]
    kn = k * lax.rsqrt(jnp.mean(k * k, axis=-1, keepdims=True) + EPS) * kg_ref[...]
    v = v_ref[...]
    if emit_kv:
        ko_ref[...] = kn
        vo_ref[...] = v
    if rope:
        kn = _rope(kn, cos_ref[...], sin_ref[...])
    kb, vb = kn.astype(BF16), v.astype(BF16)
    if has_ctx:
        kcb, vcb = kc_ref[0, 0].astype(BF16), vc_ref[0, 0].astype(BF16)
    for g in range(groups):
        for qi in range(seq // bq):
            lo, hi = qi * bq, (qi + 1) * bq
            q = q_ref[lo:hi, g * hd:(g + 1) * hd]
            qn = q * lax.rsqrt(jnp.mean(q * q, axis=-1, keepdims=True) + EPS) * qg_ref[...]
            if rope:
                qn = _rope(qn, cos_ref[lo:hi, :], sin_ref[lo:hi, :])
            qb = (qn * (scale * LOG2E)).astype(BF16)
            s1 = _nt_dot(qb, kb)
            mx = jnp.max(s1, axis=-1, keepdims=True)
            if has_ctx:
                s2 = _nt_dot(qb, kcb)
                mx = jnp.maximum(mx, jnp.max(s2, axis=-1, keepdims=True))
            p1 = jnp.exp2(s1 - mx)
            den = jnp.sum(p1, axis=-1, keepdims=True)
            o = jnp.dot(p1.astype(BF16), vb, preferred_element_type=F32)
            if has_ctx:
                p2 = jnp.exp2(s2 - mx)
                den = den + jnp.sum(p2, axis=-1, keepdims=True)
                o = o + jnp.dot(p2.astype(BF16), vcb, preferred_element_type=F32)
            y_ref[lo:hi, g * hd:(g + 1) * hd] = (o / den).astype(y_ref.dtype)


def _attention(proj, cols, row0, nb, seq, q_g, k_g, ctx, layer, rope, emit_kv, y_prev=None):
    hd = (cols["at_v"] - cols["at_k"]) // ATT_KV_HEADS
    groups = ATT_HEADS // ATT_KV_HEADS
    gw = groups * hd
    rb = row0 // seq
    in_specs = [pl.BlockSpec((seq, gw), lambda b, h: (rb + b, cols["at_q"] // gw + h)),
                pl.BlockSpec((seq, hd), lambda b, h: (rb + b, cols["at_k"] // hd + h)),
                pl.BlockSpec((seq, hd), lambda b, h: (rb + b, cols["at_v"] // hd + h)),
                pl.BlockSpec((1, hd), lambda b, h: (0, 0)),
                pl.BlockSpec((1, hd), lambda b, h: (0, 0))]
    args = [proj, proj, proj, q_g.reshape(1, hd), k_g.reshape(1, hd)]
    if rope:
        cos_np, sin_np = _rope_tables(seq, hd)
        in_specs += [pl.BlockSpec((seq, hd), lambda b, h: (0, 0))] * 2
        args += [jnp.asarray(cos_np), jnp.asarray(sin_np)]
    has_ctx = ctx is not None
    if has_ctx:
        ck, cv = ctx
        past = ck.shape[2]
        in_specs += [pl.BlockSpec((1, 1, past, hd), lambda b, h: (b, layer, 0, h))] * 2
        args += [ck, cv]
    y_shape, aliases = _shared_rows_out(proj.shape[0], ATT_HEADS * hd, y_prev, in_specs, args)
    n_prev = len(aliases)
    out_shape = [y_shape]
    out_specs = [pl.BlockSpec((seq, gw), lambda b, h: (rb + b, h))]
    if emit_kv:
        depth, kv_prev = emit_kv
        out_shape += [jax.ShapeDtypeStruct((nb, depth, seq, ATT_KV_HEADS * hd), F32)] * 2
        out_specs += [pl.BlockSpec((None, None, seq, hd), lambda b, h: (b, layer, 0, h))] * 2
        if kv_prev is not None:
            for k, prev in enumerate(kv_prev):
                in_specs.append(pl.BlockSpec(memory_space=pl.ANY))
                args.append(prev)
                aliases[len(args) - 1] = 1 + k
            n_prev += len(kv_prev)
    outs = pl.pallas_call(
        functools.partial(_attn_kernel, rope=rope, has_ctx=has_ctx, has_prev=n_prev,
                          emit_kv=bool(emit_kv), groups=groups, bq=min(seq, ATT_BLOCK_Q), scale=float(hd) ** -0.5),
        out_shape=tuple(out_shape),
        grid=(nb, ATT_KV_HEADS),
        in_specs=in_specs,
        out_specs=tuple(out_specs),
        input_output_aliases=aliases,
        compiler_params=_cparams("parallel", "parallel"),
        name="attention",
    )(*args)
    if emit_kv:
        return outs
    return outs[0], None, None


def _branch_kernel(y0_ref, y1_ref, y2_ref, g0_ref, g1_ref, g2_ref, w_ref, o_ref):
    acc = None
    for n, (y_ref, g_ref) in enumerate(((y0_ref, g0_ref), (y1_ref, g1_ref), (y2_ref, g2_ref))):
        p = jnp.dot(y_ref[...], w_ref[n], preferred_element_type=F32)
        term = jax.nn.sigmoid(g_ref[...]) * p
        acc = term if acc is None else acc + term
    o_ref[...] = acc.astype(o_ref.dtype)


def _branch_merge(ys, proj, cols, w_branch, layer, tm, tn):
    t, bw = ys[0].shape
    d = w_branch.shape[3]
    goff = cols["br_g"]
    y_spec = pl.BlockSpec((tm, bw), lambda i, j: (i, 0))

    def g_spec(n):
        return pl.BlockSpec((tm, tn), lambda i, j, n=n: (i, (goff + n * d) // tn + j))

    return pl.pallas_call(
        _branch_kernel,
        out_shape=jax.ShapeDtypeStruct((t, d), BF16),
        grid=(t // tm, d // tn),
        in_specs=[y_spec, y_spec, y_spec, g_spec(0), g_spec(1), g_spec(2),
                  pl.BlockSpec((None, N_BRANCH, bw, tn), lambda i, j: (layer, 0, 0, j))],
        out_specs=pl.BlockSpec((tm, tn), lambda i, j: (i, j)),
        compiler_params=_cparams("parallel", "arbitrary"),
        name="branch_merge",
    )(*ys, proj, proj, proj, w_branch)


def _proj_residual_kernel(a_ref, w_ref, x_ref, mod_ref, o_ref, *, gate_idx):
    y = jnp.dot(a_ref[...], w_ref[...], preferred_element_type=F32)
    o_ref[...] = x_ref[...] + mod_ref[0, gate_idx:gate_idx + 1, :] * y


def _proj_residual(a, w, layer, x, mod, seq_row, *, gate_idx, tm, tn):
    t, k = a.shape
    d = w.shape[2]
    return pl.pallas_call(
        functools.partial(_proj_residual_kernel, gate_idx=gate_idx),
        out_shape=jax.ShapeDtypeStruct((t, d), F32),
        grid=(t // tm, d // tn),
        in_specs=[pl.BlockSpec((tm, k), lambda i, j: (i, 0)),
                  pl.BlockSpec((None, k, tn), lambda i, j: (layer, 0, j)),
                  pl.BlockSpec((tm, tn), lambda i, j: (i, j)),
                  pl.BlockSpec((1, 6, tn), lambda i, j: (seq_row(i), 0, j))],
        out_specs=pl.BlockSpec((tm, tn), lambda i, j: (i, j)),
        compiler_params=_cparams("parallel", "arbitrary"),
        name="proj_residual",
    )(a, w, x, mod)


def _topk_rows(s, k, exact_ties):
    rows = s.shape[0]
    iota = lax.broadcasted_iota(jnp.int32, s.shape, 0).astype(F32)
    rank = jnp.full(s.shape, float(k), F32)
    vals = []
    for r in range(k):
        mx = jnp.max(s, axis=0, keepdims=True)
        sel = s == mx
        if exact_ties:
            sel = iota == jnp.min(jnp.where(sel, iota, float(rows)), axis=0, keepdims=True)
        rank = jnp.where(sel, float(r), rank)
        vals.append(mx)
        s = jnp.where(sel, -jnp.inf, s)
    count = jnp.sum(jnp.where(rank < float(k), 1.0, 0.0), axis=0, keepdims=True)
    return jnp.concatenate(vals, axis=0), rank, count


def _route_head(s1, s2, exact_ties):
    k = PEER_TOPK
    sv1, rk1, c1 = _topk_rows(s1, k, exact_ties)
    sv2, rk2, c2 = _topk_rows(s2, k, exact_ties)
    row8 = lax.broadcasted_iota(jnp.int32, (8, s1.shape[1]), 0)
    groups = [sv1[0:1, :] + sv2, sv1[1:2, :] + sv2[0:8, :]]
    for p in range(2, 8):
        groups.append(jnp.where(row8 < k // (p + 1), sv1[p:p + 1, :] + sv2[0:8, :], -jnp.inf))
    groups.append(sv1[8:16, :] + sv2[0:1, :])
    fv, rkc, c3 = _topk_rows(jnp.concatenate(groups, axis=0), k, exact_ties)
    z = jnp.sum(jnp.exp(fv - fv[0:1, :]), axis=0, keepdims=True)
    sel = jnp.where(rkc < float(k), 1.0, 0.0)
    cnt_i = jnp.zeros_like(s1)
    starts = [0, 16] + [24 + 8 * (p - 2) for p in range(2, 8)]
    sizes = [16, 8] + [8] * 6
    for p in range(k):
        if p < 8:
            cnt_p = jnp.sum(sel[starts[p]:starts[p] + sizes[p], :], axis=0, keepdims=True)
        else:
            cnt_p = sel[72 + p - 8:72 + p - 7, :]
        cnt_i = cnt_i + jnp.where(rk1 == float(p), cnt_p, 0.0)
    u1 = jnp.exp(s1 - sv1[0:1, :]) / z
    u2 = jnp.exp(s2 - sv2[0:1, :])
    return u1, cnt_i, u2, rk2, jnp.max(jnp.maximum(jnp.maximum(c1, c2), c3))


def _peer_route_kernel(q_ref, keys_ref, u1_ref, cnt_ref, u2_ref, rk2_ref):
    dh = keys_ref.shape[3]
    s1 = _nt_dot(keys_ref[0, 0], q_ref[:, 0:dh])
    s2 = _nt_dot(keys_ref[0, 1], q_ref[:, dh:2 * dh])

    def emit(u1, cnt, u2, rk2):
        u1_ref[0] = u1
        cnt_ref[0] = cnt
        u2_ref[0] = u2.astype(u2_ref.dtype)
        rk2_ref[0] = rk2.astype(rk2_ref.dtype)

    *fast, most = _route_head(s1, s2, exact_ties=False)
    emit(*fast)

    @pl.when(most > float(PEER_TOPK))
    def _():
        emit(*_route_head(s1, s2, exact_ties=True)[:4])


def _peer_route(q, keys_bf, tr):
    t = q.shape[0]
    heads, _, nkeys, dh = keys_bf.shape
    sd = jax.ShapeDtypeStruct((heads, nkeys, t), F32)
    sd16 = jax.ShapeDtypeStruct((heads, nkeys, t), BF16)
    spec = pl.BlockSpec((1, nkeys, tr), lambda i, h: (h, 0, i))
    return pl.pallas_call(
        _peer_route_kernel,
        out_shape=(sd, sd, sd16, sd16),
        grid=(t // tr, heads),
        in_specs=[pl.BlockSpec((tr, 2 * dh), lambda i, h: (i, h)),
                  pl.BlockSpec((1, 2, nkeys, dh), lambda i, h: (h, 0, 0, 0))],
        out_specs=(spec, spec, spec, spec),
        compiler_params=_cparams("parallel", "parallel"),
        name="peer_route",
    )(q, keys_bf)


def _peer_dense_kernel(ht_ref, u_ref, vt_ref, u1_ref, cnt_ref, u2_ref, rk2_ref, o_ref, w_ref, *, sub, group):
    e = pl.program_id(1)

    @pl.when(e == 0)
    def _():
        o_ref[...] = jnp.zeros_like(o_ref)

    te = u_ref.shape[0]
    nkeys, tm = u2_ref.shape[1:]
    ktiles = nkeys // BF16_ROWS
    for i in range(te // nkeys):
        w = None
        for h in range(PEER_HEADS):
            u1 = jnp.broadcast_to(u1_ref[h, i:i + 1, :], (BF16_ROWS, tm)).astype(BF16)[None]
            cn = jnp.broadcast_to(cnt_ref[h, i:i + 1, :], (BF16_ROWS, tm)).astype(BF16)[None]
            rk = rk2_ref[h].reshape(ktiles, BF16_ROWS, tm)
            u2 = u2_ref[h].reshape(ktiles, BF16_ROWS, tm)
            term = u1 * jnp.where(rk < cn, u2, jnp.zeros((), BF16))
            w = term if w is None else w + term
        w_ref[i * nkeys:(i + 1) * nkeys, :] = w.reshape(nkeys, tm)

    ht = ht_ref[...]
    nsub = te // sub
    pre = [jnp.dot(u_ref[s * sub:(s + 1) * sub, :], ht, preferred_element_type=F32) for s in range(nsub)]
    pending = []
    for s in range(nsub):
        act = jax.nn.gelu(pre[s].astype(BF16), approximate=True)
        pending.append(w_ref[s * sub:(s + 1) * sub, :] * act)
        if len(pending) == group or s + 1 == nsub:
            lo = (s + 1 - len(pending)) * sub
            wa = pending[0] if len(pending) == 1 else jnp.concatenate(pending, axis=0)
            o_ref[...] += jnp.dot(vt_ref[:, lo:(s + 1) * sub], wa, preferred_element_type=F32)
            pending = []


def _peer_dense(h2t, u_bf, vt_bf, layer, route, *, tm, te, sub, group):
    d, t = h2t.shape
    ne = u_bf.shape[1]
    heads, nkeys, _ = route[0].shape
    once = pl.Buffered(1)
    r_spec = pl.BlockSpec((heads, nkeys, tm), lambda i, e: (0, 0, i), pipeline_mode=once)
    k_spec = pl.BlockSpec((heads, te // nkeys, tm), lambda i, e: (0, e, i))
    return pl.pallas_call(
        functools.partial(_peer_dense_kernel, sub=sub, group=group),
        out_shape=jax.ShapeDtypeStruct((d, t), F32),
        grid=(t // tm, ne // te),
        scratch_shapes=[pltpu.VMEM((te, tm), BF16)],
        in_specs=[pl.BlockSpec((d, tm), lambda i, e: (0, i), pipeline_mode=once),
                  pl.BlockSpec((None, te, d), lambda i, e: (layer, e, 0)),
                  pl.BlockSpec((None, d, te), lambda i, e: (layer, 0, e)),
                  k_spec, k_spec, r_spec, r_spec],
        out_specs=pl.BlockSpec((d, tm), lambda i, e: (0, i)),
        compiler_params=_cparams("parallel", "arbitrary"),
        name="peer_dense",
    )(h2t, u_bf, vt_bf, *route)


def _peer_residual_kernel(x_ref, yt_ref, mod_ref, fg_ref, o_ref, *, final_norm):
    y = x_ref[...] + mod_ref[0, 5:6, :] * yt_ref[...].T
    if final_norm:
        y = y * lax.rsqrt(jnp.mean(y * y, axis=-1, keepdims=True) + EPS) * fg_ref[...]
    o_ref[...] = y


def _peer_residual(x, yt, mod, final_g, seq_row, *, final_norm, tm, first_tile=0, ntiles=None):
    t, d = x.shape
    ntiles = t // tm if ntiles is None else ntiles
    return pl.pallas_call(
        functools.partial(_peer_residual_kernel, final_norm=final_norm),
        out_shape=jax.ShapeDtypeStruct((ntiles * tm, d), F32),
        grid=(ntiles,),
        in_specs=[pl.BlockSpec((tm, d), lambda i: (first_tile + i, 0)),
                  pl.BlockSpec((d, tm), lambda i: (0, first_tile + i)),
                  pl.BlockSpec((1, 6, d), lambda i: (seq_row(first_tile + i), 0, 0)),
                  pl.BlockSpec((1, d), lambda i: (0, 0))],
        out_specs=pl.BlockSpec((tm, d), lambda i: (i, 0)),
        compiler_params=_cparams("parallel"),
        name="peer_residual",
    )(x, yt, mod, final_g.reshape(1, d))


def _in_layout(bw, kvw, d):
    sizes = (("ml_q", bw), ("ml_k", bw), ("ml_v", bw), ("ml_o", bw), ("hy", 3 * bw),
             ("at_q", bw), ("at_k", kvw), ("at_v", kvw), ("br_g", N_BRANCH * d), ("ml_g", GATE_PAD))
    cols, off = {}, 0
    for name, size in sizes:
        cols[name] = off
        off += size
    return cols, off


def _pack_w_in(w_in, bw):
    depth, d, _ = w_in.shape
    ngate = 4 * ML_HEADS
    gates = w_in[:, :, 4 * bw:4 * bw + ngate].reshape(depth, d, 4, ML_HEADS)
    gates = jnp.swapaxes(gates, 2, 3)
    gates = jnp.pad(gates, ((0, 0), (0, 0), (0, 0), (0, GATE_PAD // ML_HEADS - 4))).reshape(depth, d, GATE_PAD)
    return jnp.concatenate([w_in[:, :, :4 * bw], w_in[:, :, 4 * bw + ngate:], gates], axis=2).astype(BF16)


def kernel(x_prompt, x_sample, c, cache_k, cache_v, state_C, state_n, state_m, c_ctx, w_ada, b_ada, norm1_g, norm2_g, w_in, ml_gate_bias, ml_norm_g, hy_conv_w, hy_w1, hy_b1, hy_w2, hy_b2, hy_w3, hy_decay, hy_skip, q_norm_g, k_norm_g, w_branch, w_out, peer_wq, peer_keys, peer_u, peer_v, final_g):
    nbp, seq_p, d = x_prompt.shape
    nbs, seq_s, _ = x_sample.shape
    depth = w_ada.shape[0]
    bw = d // 2
    kvw = cache_k.shape[3] * cache_k.shape[4]
    tp, ts = nbp * seq_p, nbs * seq_s
    cols, ncols = _in_layout(bw, kvw, d)
    assert w_in.shape[2] == ncols - GATE_PAD + 4 * ML_HEADS

    tm = _tile(seq_s, TOKEN_TILE)
    tn = TOKEN_TILE
    assert tp % tm == 0 and PEER_TOPK == 16
    tiles_p, tiles_per_seq = tp // tm, seq_s // tm

    def seq_row(i):
        return jnp.where(i < tiles_p, 0, 1 + (i - tiles_p) // tiles_per_seq)

    x = jnp.concatenate([x_prompt.reshape(tp, d), x_sample.reshape(ts, d)], axis=0)
    cond = jnp.concatenate([c_ctx[None, :], c], axis=0)
    nrow = cond.shape[0]
    cond = jnp.pad(cond, ((0, (-nrow) % 8), (0, 0)))
    mod_all = _ada(cond, w_ada, b_ada).reshape(depth, cond.shape[0], 6, d)

    ctx_k = cache_k.reshape(nbs, depth, cache_k.shape[2], kvw)
    ctx_v = cache_v.reshape(nbs, depth, cache_v.shape[2], kvw)
    dft = {}
    for seq in {seq_p, seq_s}:
        fwd_np, inv_np = _dft_mats(seq)
        fwd_hi = jnp.asarray(fwd_np).astype(BF16)
        fwd_lo = (jnp.asarray(fwd_np) - fwd_hi.astype(F32)).astype(BF16)
        dft[seq] = (fwd_hi, fwd_lo, jnp.asarray(inv_np).astype(BF16))

    w_in_bf = _pack_w_in(w_in, bw)
    w_branch_bf, w_out_bf, wq_bf = w_branch.astype(BF16), w_out.astype(BF16), peer_wq.astype(BF16)
    u_bf = peer_u.astype(BF16)
    vt_bf = jnp.swapaxes(peer_v, 1, 2).astype(BF16)
    keys_bf = peer_keys.astype(BF16)

    new_state = new_kv = None
    for l in range(depth):
        mod = mod_all[l]
        proj = _modnorm_matmul(x, mod, norm1_g[l], w_in_bf, l, seq_row, shift_idx=0, out_dtype=F32,
                               tm=tm, tn=_tile(ncols, IN_PROJ_TILE))[0]
        gb = ml_gate_bias[l].reshape(4, ML_HEADS).T
        gb = jnp.pad(gb, ((0, 0), (0, LANES - 4))).reshape(ML_HEADS, 1, LANES)

        y_ml = y_hy = y_at = None
        for (row0, nb, seq, is_ctx) in ((0, nbp, seq_p, True), (tp, nbs, seq_s, False)):
            state = None if is_ctx else (state_C[:, l], state_n[:, l], state_m[:, l])
            y_ml, st = _mlstm(proj, cols, row0, nb, seq, gb, ml_norm_g[l], state,
                              emit_state=(l, depth, new_state) if is_ctx else None, y_prev=y_ml)
            fwd_bf, fwd_lo, inv_bf = dft[seq]
            filt = _hyena_filters(seq, hy_w1[l], hy_b1[l], hy_w2[l], hy_b2[l], hy_w3[l], hy_decay[l], fwd_bf, fwd_lo)
            y_hy = _hyena(proj, cols, row0, nb, seq, hy_conv_w[l], filt, hy_skip[l], fwd_bf, inv_bf, y_prev=y_hy)
            y_at, k_l, v_l = _attention(proj, cols, row0, nb, seq, q_norm_g[l], k_norm_g[l],
                                        None if is_ctx else (ctx_k, ctx_v), l, rope=not is_ctx,
                                        emit_kv=(depth, new_kv) if is_ctx else None, y_prev=y_at)
            if is_ctx:
                new_state, new_kv = st, (k_l, v_l)
        mixed = _branch_merge((y_ml, y_hy, y_at), proj, cols, w_branch_bf, l, tm, tn // 2)
        x = _proj_residual(mixed, w_out_bf, l, x, mod, seq_row, gate_idx=2, tm=tm, tn=tn)

        q, _, h2t = _modnorm_matmul(x, mod, norm2_g[l], wq_bf, l, seq_row, shift_idx=3,
                                    out_dtype=BF16, tm=tm, tn=tn, emit_ht=True)
        route = _peer_route(q, keys_bf[l], _tile(tm, ROUTE_TILE))
        yt = _peer_dense(h2t, u_bf, vt_bf, l, route, tm=_tile(tm, EXPERT_TOKEN_TILE), te=EXPERT_TILE,
                         sub=EXPERT_SUB, group=EXPERT_GROUP)
        if l < depth - 1:
            x = _peer_residual(x, yt, mod, final_g, seq_row, final_norm=False, tm=tm)

    y_prompt = _peer_residual(x, yt, mod, final_g, seq_row, final_norm=True, tm=tm,
                              first_tile=0, ntiles=tiles_p).reshape(nbp, seq_p, d)
    y_sample = _peer_residual(x, yt, mod, final_g, seq_row, final_norm=True, tm=tm,
                              first_tile=tiles_p, ntiles=ts // tm).reshape(nbs, seq_s, d)
    kv_shape = (nbp, depth, seq_p, ATT_KV_HEADS, kvw // ATT_KV_HEADS)
    new_c, new_n, new_m = new_state
    return (y_prompt, y_sample, new_kv[0].reshape(kv_shape), new_kv[1].reshape(kv_shape),
            new_c, new_n[:, :, :, :, 0, :], new_m[:, :, :, :, 0, 0])
```
